```python
import jax, jax.numpy as jnp
from jax import lax
import numpy as np


D_MODEL = 1024
BATCH = 16
SEQ = 256
DEPTH = 4
DEC_BATCH = 2
DEC_SEQ = 4096
PAST_LEN = 512

GRID_W = 64
GROUP_W = 256
D_MIX = 4 * GROUP_W
HEAD_DIM = 64
A_HEADS = 4
CHUNK = 128
C_Q_HEADS = 4
C_KV_HEADS = 2
C_GROUPS = C_Q_HEADS // C_KV_HEADS
WINDOW = 128
QBLK = 128
D_HEADS = 4
WIN_R = 8
WIN_C = 16
CONV_W = 3
ROPE_BASE = 10000.0
EPS = 1e-6
NEG = -1e30
SPLIT_SIZES = (GROUP_W, GROUP_W, GROUP_W,
               GROUP_W, GROUP_W, GROUP_W, GROUP_W,
               C_Q_HEADS * HEAD_DIM, C_KV_HEADS * HEAD_DIM, C_KV_HEADS * HEAD_DIM, GROUP_W,
               D_HEADS * HEAD_DIM, D_HEADS * HEAD_DIM, D_HEADS * HEAD_DIM, GROUP_W)
D_IN = 7 * GROUP_W + (C_Q_HEADS + 2 * C_KV_HEADS) * HEAD_DIM + GROUP_W + 3 * D_HEADS * HEAD_DIM + GROUP_W

kernel_name = 'hybrid_diffusion_denoise_step'


def rmsnorm(x, g):
    xf = x.astype(jnp.float32)
    y = xf * lax.rsqrt(jnp.mean(xf * xf, axis=-1, keepdims=True) + EPS)
    return (y * g.astype(jnp.float32)).astype(x.dtype)


def modulate(x, cond, g, w_mod, b_mod):
    mod = jax.nn.silu(cond) @ w_mod + b_mod
    shift, scale, gate = jnp.split(mod[:, None, :], 3, axis=-1)
    return rmsnorm(x, g) * (1 + scale) + shift, gate


def split_proj(p):
    idx = [int(i) for i in np.cumsum(SPLIT_SIZES)[:-1]]
    return jnp.split(p, idx, axis=-1)


def rope1d(x, pos):
    half = x.shape[-1] // 2
    freqs = ROPE_BASE ** (-jnp.arange(half, dtype=jnp.float32) / half)
    ang = pos.astype(jnp.float32)[:, None] * freqs
    cos = jnp.cos(ang)[:, None, :].astype(x.dtype)
    sin = jnp.sin(ang)[:, None, :].astype(x.dtype)
    x1, x2 = x[..., :half], x[..., half:]
    return jnp.concatenate([x1 * cos - x2 * sin, x1 * sin + x2 * cos], axis=-1)


def rope2d(x):
    t = jnp.arange(x.shape[1])
    half = x.shape[-1] // 2
    return jnp.concatenate([rope1d(x[..., :half], t // GRID_W),
                            rope1d(x[..., half:], t % GRID_W)], axis=-1)


def sink_softmax(s, sink):
    s = s.astype(jnp.float32)
    sink = sink.astype(jnp.float32)
    m = jnp.maximum(jnp.max(s, axis=-1, keepdims=True), sink)
    e = jnp.exp(s - m)
    return e / (jnp.sum(e, axis=-1, keepdims=True) + jnp.exp(sink - m))


def chunk_mlp(u, v, w_s, b_s):
    B, T, _ = v.shape
    n = T // CHUNK
    vh = v.reshape(B, n, CHUNK, A_HEADS, HEAD_DIM)
    vf = vh.astype(jnp.float32)
    vh = (vf * lax.rsqrt(jnp.mean(vf * vf, axis=-1, keepdims=True) + EPS)).astype(v.dtype)
    mixed = jnp.einsum('hpq,bnqhd->bnphd', w_s, vh) + b_s.T[None, None, :, :, None]
    return u * mixed.reshape(B, T, GROUP_W)


def short_conv(bg, cg, h, w_conv):
    xc = cg * h
    xp = jnp.pad(xc, ((0, 0), (1, 1), (0, 0)))
    y = w_conv[0] * xp[:, :-2] + w_conv[1] * xp[:, 1:-1] + w_conv[2] * xp[:, 2:]
    return bg * y


def dense_attn(q, k, v, sink):
    B, L, Hkv, G, dh = q.shape
    nb = L // QBLK
    scale = dh ** -0.5
    qb = jnp.moveaxis(q.reshape(B, nb, QBLK, Hkv, G, dh), 1, 0)

    def block(qi):
        s = jnp.einsum('bqhgd,bkhd->bhgqk', qi, k).astype(jnp.float32) * scale
        if sink is None:
            p = jax.nn.softmax(s, axis=-1)
        else:
            p = sink_softmax(s, sink.reshape(Hkv, G)[None, :, :, None, None])
        return jnp.einsum('bhgqk,bkhd->bqhgd', p.astype(v.dtype), v)

    o = lax.map(block, qb)
    return jnp.moveaxis(o, 0, 1).reshape(B, L, Hkv * G * dh)


def window_attn(q, k, v, kc, vc, sink):
    B, T, Hq, dh = q.shape
    nb = T // QBLK
    scale = dh ** -0.5
    qb = q.reshape(B, nb, QBLK, C_KV_HEADS, C_GROUPS, dh)
    pad = ((0, 0), (QBLK, QBLK), (0, 0), (0, 0))
    kp = jnp.pad(k, pad).reshape(B, nb + 2, QBLK, C_KV_HEADS, dh)
    vp = jnp.pad(v, pad).reshape(B, nb + 2, QBLK, C_KV_HEADS, dh)
    kw = jnp.concatenate([kp[:, :-2], kp[:, 1:-1], kp[:, 2:]], axis=2)
    vw = jnp.concatenate([vp[:, :-2], vp[:, 1:-1], vp[:, 2:]], axis=2)
    n = np.arange(nb)[:, None, None]
    p = np.arange(QBLK)[None, :, None]
    j = np.arange(3 * QBLK)[None, None, :]
    kpos = (n - 1) * QBLK + j
    qpos = n * QBLK + p
    mask = jnp.asarray((np.abs(kpos - qpos) <= WINDOW) & (kpos >= 0) & (kpos < T))
    s_loc = jnp.einsum('bnqhgd,bnjhd->bnhgqj', qb, kw).astype(jnp.float32) * scale
    s_loc = jnp.where(mask[None, :, None, None], s_loc, NEG)
    s_ctx = jnp.einsum('bnqhgd,blhd->bnhgql', qb, kc).astype(jnp.float32) * scale
    pr = sink_softmax(jnp.concatenate([s_loc, s_ctx], axis=-1),
                      sink.reshape(C_KV_HEADS, C_GROUPS)[None, None, :, :, None, None]).astype(v.dtype)
    p_loc, p_ctx = pr[..., :3 * QBLK], pr[..., 3 * QBLK:]
    o = (jnp.einsum('bnhgqj,bnjhd->bnqhgd', p_loc, vw)
         + jnp.einsum('bnhgql,blhd->bnqhgd', p_ctx, vc))
    return o.reshape(B, T, Hq * dh)


def neighbourhood_attn(q, k, v, kc, vc, rpb):
    B, T, H, dh = q.shape
    rows = T // GRID_W
    kr = min(WIN_R, rows)
    scale = dh ** -0.5
    qg = q.reshape(B, rows, GRID_W, H, dh)
    kg = k.reshape(B, rows, GRID_W, H, dh)
    vg = v.reshape(B, rows, GRID_W, H, dh)
    cols = np.arange(GRID_W)
    cstart = np.clip(cols - WIN_C // 2, 0, GRID_W - WIN_C)
    col_idx = cstart[:, None] + np.arange(WIN_C)
    dc = col_idx - cols[:, None] + (WIN_C - 1)
    rpb_c = rpb[:, :, dc]

    def one_row(r):
        rs = jnp.clip(r - kr // 2, 0, rows - kr)
        ridx = rs + jnp.arange(kr)
        k_nb = jnp.take(kg, ridx, axis=1)[:, :, col_idx]
        v_nb = jnp.take(vg, ridx, axis=1)[:, :, col_idx]
        qr = lax.dynamic_index_in_dim(qg, r, axis=1, keepdims=False)
        s_loc = jnp.einsum('bchd,brcwhd->bhcrw', qr, k_nb).astype(jnp.float32) * scale
        bias = jnp.transpose(jnp.take(rpb_c, ridx - r + (WIN_R - 1), axis=1), (0, 2, 1, 3))
        s_loc = s_loc + bias[None].astype(jnp.float32)
        s_ctx = jnp.einsum('bchd,blhd->bhcl', qr, kc).astype(jnp.float32) * scale
        s = jnp.concatenate([s_loc.reshape(B, H, GRID_W, kr * WIN_C), s_ctx], axis=-1)
        pr = jax.nn.softmax(s, axis=-1).astype(v.dtype)
        p_loc = pr[..., :kr * WIN_C].reshape(B, H, GRID_W, kr, WIN_C)
        p_ctx = pr[..., kr * WIN_C:]
        return (jnp.einsum('bhcrw,brcwhd->bchd', p_loc, v_nb)
                + jnp.einsum('bhcl,blhd->bchd', p_ctx, vc))

    o = lax.map(one_row, jnp.arange(rows))
    return jnp.moveaxis(o, 0, 1).reshape(B, T, H * dh)


def project(x, cond, g, w_mod, b_mod, w_in):
    h, gate = modulate(x, cond, g, w_mod, b_mod)
    return split_proj(h @ w_in), gate


def branches_ab(parts, w_s, b_s, w_conv):
    au, av, az, bb, bc, bh, bz = parts
    ya = chunk_mlp(au, av, w_s, b_s) * jax.nn.silu(az)
    yb = short_conv(bb, bc, bh, w_conv) * jax.nn.silu(bz)
    return ya, yb


def context_layer(x, c_ctx, g, w_mod, b_mod, w_in, w_out, w_s, b_s, w_conv, sink):
    B, L, _ = x.shape
    parts, gate = project(x, c_ctx[None, :], g, w_mod, b_mod, w_in)
    ya, yb = branches_ab(parts[:7], w_s, b_s, w_conv)
    cq, ck, cv, cz, dq, dk, dv, dz = parts[7:]
    ck = ck.reshape(B, L, C_KV_HEADS, HEAD_DIM)
    cv = cv.reshape(B, L, C_KV_HEADS, HEAD_DIM)
    dk = dk.reshape(B, L, D_HEADS, HEAD_DIM)
    dv = dv.reshape(B, L, D_HEADS, HEAD_DIM)
    yc = dense_attn(cq.reshape(B, L, C_KV_HEADS, C_GROUPS, HEAD_DIM), ck, cv, sink) * jax.nn.silu(cz)
    yd = dense_attn(dq.reshape(B, L, D_HEADS, 1, HEAD_DIM), dk, dv, None) * jax.nn.silu(dz)
    y = jnp.concatenate([ya, yb, yc, yd], axis=-1) @ w_out
    return x + gate * y, ck, cv, dk, dv


def latent_layer(x, c, kc, vc, kd, vd, g, w_mod, b_mod, w_in, w_out, w_s, b_s, w_conv, sink, rpb):
    B, T, _ = x.shape
    parts, gate = project(x, c, g, w_mod, b_mod, w_in)
    ya, yb = branches_ab(parts[:7], w_s, b_s, w_conv)
    cq, ck, cv, cz, dq, dk, dv, dz = parts[7:]
    cq = rope2d(cq.reshape(B, T, C_Q_HEADS, HEAD_DIM))
    ck = rope2d(ck.reshape(B, T, C_KV_HEADS, HEAD_DIM))
    cv = cv.reshape(B, T, C_KV_HEADS, HEAD_DIM)
    yc = window_attn(cq, ck, cv, kc, vc, sink) * jax.nn.silu(cz)
    yd = neighbourhood_attn(dq.reshape(B, T, D_HEADS, HEAD_DIM), dk.reshape(B, T, D_HEADS, HEAD_DIM),
                            dv.reshape(B, T, D_HEADS, HEAD_DIM), kd, vd, rpb) * jax.nn.silu(dz)
    y = jnp.concatenate([ya, yb, yc, yd], axis=-1) @ w_out
    return x + gate * y


def setup_inputs(seed: int = 0) -> dict:
    key = jax.random.key(seed)
    ks = jax.random.split(key, 20)
    nrm = jax.random.normal
    f32 = jnp.float32
    return {
        'x_prompt': nrm(ks[0], (BATCH, SEQ, D_MODEL), f32),
        'x_sample': nrm(ks[1], (DEC_BATCH, DEC_SEQ, D_MODEL), f32),
        'cache_c_k': nrm(ks[2], (DEC_BATCH, DEPTH, PAST_LEN, C_KV_HEADS, HEAD_DIM), f32),
        'cache_c_v': nrm(ks[3], (DEC_BATCH, DEPTH, PAST_LEN, C_KV_HEADS, HEAD_DIM), f32),
        'cache_d_k': nrm(ks[4], (DEC_BATCH, DEPTH, PAST_LEN, D_HEADS, HEAD_DIM), f32),
        'cache_d_v': nrm(ks[5], (DEC_BATCH, DEPTH, PAST_LEN, D_HEADS, HEAD_DIM), f32),
        'c': nrm(ks[6], (DEC_BATCH, D_MODEL), f32),
        'c_ctx': nrm(ks[7], (D_MODEL,), f32),
        'norm_g': 1.0 + 0.02 * nrm(ks[8], (DEPTH, D_MODEL), f32),
        'w_mod': 0.5 * D_MODEL ** -0.5 * nrm(ks[9], (DEPTH, D_MODEL, 3 * D_MODEL), f32),
        'b_mod': 0.02 * nrm(ks[10], (DEPTH, 3 * D_MODEL), f32),
        'w_in': D_MODEL ** -0.5 * nrm(ks[11], (DEPTH, D_MODEL, D_IN), f32),
        'w_out': D_MIX ** -0.5 * nrm(ks[12], (DEPTH, D_MIX, D_MODEL), f32),
        'w_s': CHUNK ** -0.5 * nrm(ks[13], (DEPTH, A_HEADS, CHUNK, CHUNK), f32),
        'b_s': 1.0 + 0.02 * nrm(ks[14], (DEPTH, A_HEADS, CHUNK), f32),
        'w_conv': CONV_W ** -0.5 * nrm(ks[15], (DEPTH, CONV_W, GROUP_W), f32),
        'sink': 0.5 * nrm(ks[16], (DEPTH, C_Q_HEADS), f32),
        'rpb': 0.1 * nrm(ks[17], (DEPTH, D_HEADS, 2 * WIN_R - 1, 2 * WIN_C - 1), f32),
        'final_g': 1.0 + 0.02 * nrm(ks[18], (D_MODEL,), f32),
    }


def reference(x_prompt, x_sample, cache_c_k, cache_c_v, cache_d_k, cache_d_v, c, c_ctx,
              norm_g, w_mod, b_mod, w_in, w_out, w_s, b_s, w_conv, sink, rpb, final_g):
    xp = x_prompt
    xs = x_sample
    ck_list, cv_list, dk_list, dv_list = [], [], [], []
    for l in range(DEPTH):
        xp, ck, cv, dk, dv = context_layer(xp, c_ctx, norm_g[l], w_mod[l], b_mod[l], w_in[l], w_out[l],
                                           w_s[l], b_s[l], w_conv[l], sink[l])
        ck_list.append(ck)
        cv_list.append(cv)
        dk_list.append(dk)
        dv_list.append(dv)
        xs = latent_layer(xs, c, cache_c_k[:, l], cache_c_v[:, l], cache_d_k[:, l], cache_d_v[:, l],
                          norm_g[l], w_mod[l], b_mod[l], w_in[l], w_out[l], w_s[l], b_s[l], w_conv[l],
                          sink[l], rpb[l])
    y_prompt = rmsnorm(xp, final_g)
    y_sample = rmsnorm(xs, final_g)
    state_c_k = jnp.stack(ck_list, axis=1)
    state_c_v = jnp.stack(cv_list, axis=1)
    state_d_k = jnp.stack(dk_list, axis=1)
    state_d_v = jnp.stack(dv_list, axis=1)
    return (y_prompt, y_sample, state_c_k, state_c_v, state_d_k, state_d_v)
```

```python
import functools

import numpy as np
import jax
import jax.numpy as jnp
from jax import lax
from jax.experimental import pallas as pl
from jax.experimental.pallas import tpu as pltpu

D_MODEL = 1024
BATCH = 16
SEQ = 256
DEPTH = 4
DEC_BATCH = 2
DEC_SEQ = 4096
PAST_LEN = 512
GRID_W = 64
GROUP_W = 256
HEAD_DIM = 64
A_HEADS = 4
CHUNK = 128
WINDOW = 128
QBLK = 128
WIN_R = 8
WIN_C = 16
ROPE_BASE = 10000.0
EPS = 1e-6
NEG = -1e30
D_IN = 3584

LANES = 128
TILE = 512
ROWS_PER_TILE = TILE // GRID_W
N_TILES = DEC_SEQ // TILE
N_F32 = 9 * GROUP_W
N_QKV = D_IN - N_F32
VMEM_LIMIT = 56 * 1024 * 1024

F32 = jnp.float32
BF16 = jnp.bfloat16

_C_PERM = np.concatenate([np.arange(0, 64), np.arange(128, 192), np.arange(64, 128), np.arange(192, 256)])


def _projection_column_order():
    off = {}
    names = ["au", "av", "az", "bb", "bc", "bh", "bz", "cq", "ck", "cv", "cz", "dq", "dk", "dv", "dz"]
    sizes = [256] * 7 + [256, 128, 128, 256, 256, 256, 256, 256]
    o = 0
    for n, s in zip(names, sizes):
        off[n] = np.arange(o, o + s)
        o += s
    order = [off[n] for n in ["au", "av", "az", "bb", "bc", "bh", "bz"]]
    order += [off["cz"][_C_PERM], off["dz"]]
    order += [off["cq"][_C_PERM], off["ck"], off["cv"], off["dq"], off["dk"], off["dv"]]
    return np.concatenate(order)


_COL_ORDER = _projection_column_order()
_OUT_ROW_ORDER = np.concatenate([np.arange(0, 512), 512 + _C_PERM, np.arange(768, 1024)])


def _silu(z):
    return z * (1.0 / (1.0 + jnp.exp(-z)))


def _dot(a, b):
    return jnp.dot(a, b, preferred_element_type=F32)


def _dot_nt(a, b):
    return lax.dot_general(a, b, (((1,), (1,)), ((), ())), preferred_element_type=F32)


def _norm_mod(x, g, shift, scale):
    ms = jnp.mean(x * x, axis=-1, keepdims=True)
    y = x * lax.rsqrt(ms + EPS) * g
    return y * (1.0 + scale) + shift


def _group_mean_matrix():
    r = lax.broadcasted_iota(jnp.int32, (GROUP_W, GROUP_W), 0) // HEAD_DIM
    c = lax.broadcasted_iota(jnp.int32, (GROUP_W, GROUP_W), 1) // HEAD_DIM
    return jnp.where(r == c, 1.0 / HEAD_DIM, 0.0).astype(BF16)


def _branch_a(au, av, az, ws_ref, ba_ref):
    t = av.shape[0]
    sq = av * av
    hi = sq.astype(BF16)
    lo = (sq - hi.astype(F32)).astype(BF16)
    gm = _group_mean_matrix()
    ms = _dot(hi, gm) + _dot(lo, gm)
    vh = (av * lax.rsqrt(ms + EPS)).astype(BF16)
    head = lax.broadcasted_iota(jnp.int32, (CHUNK, GROUP_W), 1) // HEAD_DIM
    ws = ws_ref[...]
    bias = ba_ref[...]
    outs = []
    for n in range(t // CHUNK):
        v = vh[n * CHUNK:(n + 1) * CHUNK]
        rhs = jnp.concatenate([jnp.where(head == h, v, jnp.zeros_like(v)) for h in range(A_HEADS)], axis=0)
        outs.append(_dot(ws, rhs) + bias)
    mixed = jnp.concatenate(outs, axis=0) if len(outs) > 1 else outs[0]
    return au * mixed * _silu(az)


def _branch_b(bb, bc, bh, bz, wc_ref, prev_row, next_row):
    t = bb.shape[0]
    xc = bc * bh
    row = lax.broadcasted_iota(jnp.int32, xc.shape, 0)
    xm = jnp.where(row == 0, prev_row, pltpu.roll(xc, 1, axis=0))
    xp = jnp.where(row == t - 1, next_row, pltpu.roll(xc, t - 1, axis=0))
    y = wc_ref[0:1, :] * xm + wc_ref[1:2, :] * xc + wc_ref[2:3, :] * xp
    return bb * y * _silu(bz)


def _softmax_pv(scores, values, sink=None):
    m = jnp.max(scores[0], axis=-1, keepdims=True)
    for s in scores[1:]:
        m = jnp.maximum(m, jnp.max(s, axis=-1, keepdims=True))
    if sink is not None:
        m = jnp.maximum(m, sink)
    l = None
    o = None
    for s, v in zip(scores, values):
        e = jnp.exp(s - m)
        ls = jnp.sum(e, axis=-1, keepdims=True)
        os = _dot(e.astype(BF16), v)
        l = ls if l is None else l + ls
        o = os if o is None else o + os
    if sink is not None:
        l = l + jnp.exp(sink - m)
    return o / l


def _half_masks(m):
    lane = lax.broadcasted_iota(jnp.int32, (m, LANES), 1)
    return lane < HEAD_DIM


def _out_proj(x, gate, ya, yb, yc, yd, w_out_ref):
    y = _dot(ya.astype(BF16), w_out_ref[0:256, :])
    y += _dot(yb.astype(BF16), w_out_ref[256:512, :])
    y += _dot(yc.astype(BF16), w_out_ref[512:768, :])
    y += _dot(yd.astype(BF16), w_out_ref[768:1024, :])
    return x + gate * y


def _mod_kernel(cond_ref, w_ref, b_ref, o_ref):
    c = cond_ref[...]
    o_ref[...] = jnp.dot(_silu(c), w_ref[...], preferred_element_type=F32,
                         precision=lax.Precision.HIGHEST) + b_ref[...]


def _modulation(cond8, w_mod, b_mod):
    nt = 3 * D_MODEL // 1024
    return pl.pallas_call(
        _mod_kernel,
        grid=(DEPTH, nt),
        in_specs=[
            pl.BlockSpec((8, D_MODEL), lambda l, j: (0, 0)),
            pl.BlockSpec((None, D_MODEL, 1024), lambda l, j: (l, 0, j)),
            pl.BlockSpec((None, 1, 1024), lambda l, j: (l, 0, j)),
        ],
        out_specs=pl.BlockSpec((None, 8, 1024), lambda l, j: (l, 0, j)),
        out_shape=jax.ShapeDtypeStruct((DEPTH, 8, 3 * D_MODEL), F32),
        compiler_params=pltpu.CompilerParams(dimension_semantics=("arbitrary", "arbitrary")),
        name="modulation",
    )(cond8, w_mod, b_mod.reshape(DEPTH, 1, 3 * D_MODEL))


def _ctx_kernel(sink_ref, x_ref, mod_ref, g_ref, w_in_ref, w_out_ref, ws_ref, ba_ref, wc_ref,
                xo_ref, ck_ref, cv_ref, dk_ref, dv_ref):
    x = x_ref[...]
    h = _norm_mod(x, g_ref[...], mod_ref[0:1, :], mod_ref[1:2, :]).astype(BF16)
    p32 = _dot(h, w_in_ref[:, 0:N_F32])
    pq = _dot(h, w_in_ref[:, N_F32:D_IN])
    kcf = pq[:, 256:384]
    vcf = pq[:, 384:512]
    kdf = pq[:, 768:1024]
    vdf = pq[:, 1024:1280]
    ck_ref[...] = kcf
    cv_ref[...] = vcf
    dk_ref[...] = kdf
    dv_ref[...] = vdf

    ya = _branch_a(p32[:, 0:256], p32[:, 256:512], p32[:, 512:768], ws_ref, ba_ref)
    zero_row = jnp.zeros((1, GROUP_W), F32)
    yb = _branch_b(p32[:, 768:1024], p32[:, 1024:1280], p32[:, 1280:1536], p32[:, 1536:1792],
                   wc_ref, zero_row, zero_row)

    lo = _half_masks(SEQ)
    zq = jnp.zeros((SEQ, LANES), BF16)

    kc = kcf.astype(BF16)
    vc = vcf.astype(BF16)
    c_tiles = []
    for t in range(2):
        q = (pq[:, t * LANES:(t + 1) * LANES] * 0.125).astype(BF16)
        o_lo = _softmax_pv([_dot_nt(jnp.where(lo, q, zq), kc)], [vc], sink_ref[t])
        o_hi = _softmax_pv([_dot_nt(jnp.where(lo, zq, q), kc)], [vc], sink_ref[2 + t])
        c_tiles.append(jnp.where(lo, o_lo, o_hi))
    yc = jnp.concatenate(c_tiles, axis=1) * _silu(p32[:, 1792:2048])

    d_tiles = []
    for t in range(2):
        q = (pq[:, 512 + t * LANES:512 + (t + 1) * LANES] * 0.125).astype(BF16)
        k = kdf[:, t * LANES:(t + 1) * LANES].astype(BF16)
        v = vdf[:, t * LANES:(t + 1) * LANES].astype(BF16)
        o_lo = _softmax_pv([_dot_nt(jnp.where(lo, q, zq), k)], [v])
        o_hi = _softmax_pv([_dot_nt(jnp.where(lo, zq, q), k)], [v])
        d_tiles.append(jnp.where(lo, o_lo, o_hi))
    yd = jnp.concatenate(d_tiles, axis=1) * _silu(p32[:, 2048:2304])

    xo_ref[...] = _out_proj(x, mod_ref[2:3, :], ya, yb, yc, yd, w_out_ref)


def _ctx_layer(x, mod_l, g, w_in, w_out, ws_cat, bias_a, w_conv, sink_perm):
    full = lambda shape: pl.BlockSpec(shape, lambda b: (0,) * len(shape))
    seq = lambda w: pl.BlockSpec((None, SEQ, w), lambda b: (b, 0, 0))
    return pl.pallas_call(
        _ctx_kernel,
        grid=(BATCH,),
        in_specs=[
            pl.BlockSpec(memory_space=pltpu.SMEM),
            seq(D_MODEL),
            full((3, D_MODEL)),
            full((1, D_MODEL)),
            full((D_MODEL, D_IN)),
            full((D_MODEL, D_MODEL)),
            full((CHUNK, A_HEADS * CHUNK)),
            full((CHUNK, GROUP_W)),
            full((3, GROUP_W)),
        ],
        out_specs=[seq(D_MODEL), seq(128), seq(128), seq(256), seq(256)],
        out_shape=[
            jax.ShapeDtypeStruct((BATCH, SEQ, D_MODEL), F32),
            jax.ShapeDtypeStruct((BATCH, SEQ, 128), F32),
            jax.ShapeDtypeStruct((BATCH, SEQ, 128), F32),
            jax.ShapeDtypeStruct((BATCH, SEQ, 256), F32),
            jax.ShapeDtypeStruct((BATCH, SEQ, 256), F32),
        ],
        compiler_params=pltpu.CompilerParams(dimension_semantics=("arbitrary",),
                                             vmem_limit_bytes=VMEM_LIMIT),
        name="ctx_layer",
    )(sink_perm, x, mod_l, g, w_in, w_out, ws_cat, bias_a, w_conv)


def _rope(x, cos, sin_signed, first_half):
    n = x.shape[1]
    swapped = jnp.where(first_half, pltpu.roll(x, n - 16, axis=1), pltpu.roll(x, 16, axis=1))
    return x * cos + swapped * sin_signed


def _lat_proj_kernel(x_ref, mod_ref, g_ref, w_ref, cos_ref, sin_ref,
                     p32_ref, qc_ref, kc_ref, vc_ref, qd_ref, kd_ref, vd_ref):
    h = _norm_mod(x_ref[...], g_ref[...], mod_ref[0:1, :], mod_ref[1:2, :]).astype(BF16)
    for c0 in range(0, N_F32, 768):
        p32_ref[:, c0:c0 + 768] = _dot(h, w_ref[:, c0:c0 + 768])
    cos = cos_ref[...]
    sin = sin_ref[...]
    first_half = (lax.broadcasted_iota(jnp.int32, (TILE, LANES), 1) % 32) < 16
    b = N_F32
    qc = _dot(h, w_ref[:, b:b + 256])
    for t in range(2):
        qt = _rope(qc[:, t * LANES:(t + 1) * LANES], cos, sin, first_half) * 0.125
        qc_ref[:, t * LANES:(t + 1) * LANES] = qt.astype(BF16)
    kv = _dot(h, w_ref[:, b + 256:b + 512])
    kc_ref[...] = _rope(kv[:, 0:LANES], cos, sin, first_half).astype(BF16)
    vc_ref[...] = kv[:, LANES:2 * LANES].astype(BF16)
    qd_ref[...] = (_dot(h, w_ref[:, b + 512:b + 768]) * 0.125).astype(BF16)
    kd_ref[...] = _dot(h, w_ref[:, b + 768:b + 1024]).astype(BF16)
    vd_ref[...] = _dot(h, w_ref[:, b + 1024:b + 1280]).astype(BF16)


def _lat_proj(x, mod_lat, g, w_in, cos_t, sin_t):
    tile = lambda w: pl.BlockSpec((None, TILE, w), lambda b, i: (b, i, 0))
    full = lambda shape: pl.BlockSpec(shape, lambda b, i: (0,) * len(shape))
    sds = lambda w, dt: jax.ShapeDtypeStruct((DEC_BATCH, DEC_SEQ, w), dt)
    return pl.pallas_call(
        _lat_proj_kernel,
        grid=(DEC_BATCH, N_TILES),
        in_specs=[
            tile(D_MODEL),
            pl.BlockSpec((None, 3, D_MODEL), lambda b, i: (b, 0, 0)),
            full((1, D_MODEL)),
            full((D_MODEL, D_IN)),
            pl.BlockSpec((TILE, LANES), lambda b, i: (i, 0)),
            pl.BlockSpec((TILE, LANES), lambda b, i: (i, 0)),
        ],
        out_specs=[tile(N_F32), tile(256), tile(128), tile(128), tile(256), tile(256), tile(256)],
        out_shape=[sds(N_F32, F32), sds(256, BF16), sds(128, BF16), sds(128, BF16),
                   sds(256, BF16), sds(256, BF16), sds(256, BF16)],
        compiler_params=pltpu.CompilerParams(dimension_semantics=("arbitrary", "arbitrary"),
                                             vmem_limit_bytes=VMEM_LIMIT),
        name="lat_proj",
    )(x, mod_lat, g, w_in, cos_t, sin_t)


def _lat_mix_kernel(sink_ref, x_ref, mod_ref, p32_ref, pprev_ref, pnext_ref,
                    qc_ref, kc_ref, kcp_ref, kcn_ref, vc_ref, vcp_ref, vcn_ref,
                    qd_ref, kd_ref, kdp_ref, kdn_ref, vd_ref, vdp_ref, vdn_ref,
                    cck_ref, ccv_ref, cdk_ref, cdv_ref, tab_ref,
                    w_out_ref, ws_ref, ba_ref, wc_ref,
                    xo_ref,
                    kce, vce, kde, vde, yc_scr, yd_scr):
    i = pl.program_id(1)

    ya = _branch_a(p32_ref[:, 0:256], p32_ref[:, 256:512], p32_ref[:, 512:768], ws_ref, ba_ref)

    prev_row = jnp.where(i > 0, pprev_ref[7:8, 0:256] * pprev_ref[7:8, 256:512], 0.0)
    next_row = jnp.where(i < N_TILES - 1, pnext_ref[0:1, 0:256] * pnext_ref[0:1, 256:512], 0.0)
    yb = _branch_b(p32_ref[:, 768:1024], p32_ref[:, 1024:1280], p32_ref[:, 1280:1536],
                   p32_ref[:, 1536:1792], wc_ref, prev_row, next_row)

    kce[0:QBLK, :] = kcp_ref[...]
    kce[QBLK:QBLK + TILE, :] = kc_ref[...]
    kce[QBLK + TILE:, :] = kcn_ref[...]
    vce[0:QBLK, :] = vcp_ref[...]
    vce[QBLK:QBLK + TILE, :] = vc_ref[...]
    vce[QBLK + TILE:, :] = vcn_ref[...]
    k_ctx = cck_ref[...].astype(BF16)
    v_ctx = ccv_ref[...].astype(BF16)
    lo_q = _half_masks(QBLK)
    zq = jnp.zeros((QBLK, LANES), BF16)
    p_i = lax.broadcasted_iota(jnp.int32, (QBLK, 3 * QBLK), 0)
    j_i = lax.broadcasted_iota(jnp.int32, (QBLK, 3 * QBLK), 1)
    n_blocks = DEC_SEQ // QBLK

    def c_block(j, carry):
        n = i * (TILE // QBLK) + j
        r0 = pl.multiple_of(j * QBLK, QBLK)
        k_loc = kce[pl.ds(r0, 3 * QBLK), :]
        v_loc = vce[pl.ds(r0, 3 * QBLK), :]
        lower = jnp.maximum(p_i, jnp.where(n == 0, QBLK, 0))
        upper = jnp.minimum(p_i + 2 * WINDOW, jnp.where(n == n_blocks - 1, 2 * QBLK - 1, 3 * QBLK - 1))
        tiles = []
        for t in range(2):
            q = qc_ref[pl.ds(r0, QBLK), t * LANES:(t + 1) * LANES]
            outs = []
            for half in range(2):
                qm = jnp.where(lo_q, q, zq) if half == 0 else jnp.where(lo_q, zq, q)
                s_loc = _dot_nt(qm, k_loc)
                s_loc = jnp.where(j_i >= lower, jnp.where(j_i <= upper, s_loc, NEG), NEG)
                s_ctx = _dot_nt(qm, k_ctx)
                outs.append(_softmax_pv([s_loc, s_ctx], [v_loc, v_ctx], sink_ref[2 * half + t]))
            tiles.append(jnp.where(lo_q, outs[0], outs[1]))
        yc_scr[pl.ds(r0, QBLK), :] = jnp.concatenate(tiles, axis=1)
        return carry

    lax.fori_loop(0, TILE // QBLK, c_block, 0)

    halo = 4 * GRID_W
    kde[0:halo, :] = kdp_ref[...]
    kde[halo:halo + TILE, :] = kd_ref[...]
    kde[halo + TILE:, :] = kdn_ref[...]
    vde[0:halo, :] = vdp_ref[...]
    vde[halo:halo + TILE, :] = vd_ref[...]
    vde[halo + TILE:, :] = vdn_ref[...]
    kd_ctx = cdk_ref[...].astype(BF16)
    vd_ctx = cdv_ref[...].astype(BF16)
    lo_r = _half_masks(GRID_W)
    zr = jnp.zeros((GRID_W, LANES), BF16)
    n_rows = DEC_SEQ // GRID_W

    def d_row(rl, carry):
        r = i * ROWS_PER_TILE + rl
        rs = jnp.clip(r - WIN_R // 2, 0, n_rows - WIN_R)
        didx = rs - r + (WIN_R - 1)
        off = pl.multiple_of((rs - i * ROWS_PER_TILE + 4) * GRID_W, GRID_W)
        q0 = pl.multiple_of(rl * GRID_W, GRID_W)
        tiles = []
        for t in range(2):
            q = qd_ref[pl.ds(q0, GRID_W), t * LANES:(t + 1) * LANES]
            k_loc = kde[pl.ds(off, WIN_R * GRID_W), t * LANES:(t + 1) * LANES]
            v_loc = vde[pl.ds(off, WIN_R * GRID_W), t * LANES:(t + 1) * LANES]
            k_c = kd_ctx[:, t * LANES:(t + 1) * LANES]
            v_c = vd_ctx[:, t * LANES:(t + 1) * LANES]
            outs = []
            for half in range(2):
                qm = jnp.where(lo_r, q, zr) if half == 0 else jnp.where(lo_r, zr, q)
                s_loc = _dot_nt(qm, k_loc) + tab_ref[2 * t + half, didx]
                s_ctx = _dot_nt(qm, k_c)
                outs.append(_softmax_pv([s_loc, s_ctx], [v_loc, v_c]))
            tiles.append(jnp.where(lo_r, outs[0], outs[1]))
        yd_scr[pl.ds(q0, GRID_W), :] = jnp.concatenate(tiles, axis=1)
        return carry

    lax.fori_loop(0, ROWS_PER_TILE, d_row, 0)

    yc = yc_scr[...] * _silu(p32_ref[:, 1792:2048])
    yd = yd_scr[...] * _silu(p32_ref[:, 2048:2304])
    xo_ref[...] = _out_proj(x_ref[...], mod_ref[2:3, :], ya, yb, yc, yd, w_out_ref)


def _lat_mix(layer, x, mod_lat, p32, qc, kc, vc, qd, kd, vd, cck, ccv, cdk, cdv, tab,
             w_out, ws_cat, bias_a, w_conv, sink_perm):
    tile = lambda w: pl.BlockSpec((None, TILE, w), lambda b, i: (b, i, 0))
    full = lambda shape: pl.BlockSpec(shape, lambda b, i: (0,) * len(shape))

    def halo(rows, w, col=0):
        per = TILE // rows
        last = DEC_SEQ // rows - 1
        prev = pl.BlockSpec((None, rows, w), lambda b, i: (b, jnp.maximum(i * per - 1, 0), col))
        nxt = pl.BlockSpec((None, rows, w), lambda b, i: (b, jnp.minimum((i + 1) * per, last), col))
        return prev, nxt

    pprev, pnext = halo(8, 512, col=2)
    cprev, cnext = halo(QBLK, 128)
    dprev, dnext = halo(4 * GRID_W, 256)
    cache = lambda w: pl.BlockSpec((None, None, PAST_LEN, w), lambda b, i: (b, layer, 0, 0))
    return pl.pallas_call(
        _lat_mix_kernel,
        grid=(DEC_BATCH, N_TILES),
        in_specs=[
            pl.BlockSpec(memory_space=pltpu.SMEM),
            tile(D_MODEL),
            pl.BlockSpec((None, 3, D_MODEL), lambda b, i: (b, 0, 0)),
            tile(N_F32), pprev, pnext,
            tile(256), tile(128), cprev, cnext, tile(128), cprev, cnext,
            tile(256), tile(256), dprev, dnext, tile(256), dprev, dnext,
            cache(128), cache(128), cache(256), cache(256),
            full((4, WIN_R, GRID_W, WIN_R * GRID_W)),
            full((D_MODEL, D_MODEL)),
            full((CHUNK, A_HEADS * CHUNK)),
            full((CHUNK, GROUP_W)),
            full((3, GROUP_W)),
        ],
        out_specs=tile(D_MODEL),
        out_shape=jax.ShapeDtypeStruct((DEC_BATCH, DEC_SEQ, D_MODEL), F32),
        scratch_shapes=[
            pltpu.VMEM((TILE + 2 * QBLK, 128), BF16),
            pltpu.VMEM((TILE + 2 * QBLK, 128), BF16),
            pltpu.VMEM((TILE + 8 * GRID_W, 256), BF16),
            pltpu.VMEM((TILE + 8 * GRID_W, 256), BF16),
            pltpu.VMEM((TILE, 256), F32),
            pltpu.VMEM((TILE, 256), F32),
        ],
        compiler_params=pltpu.CompilerParams(dimension_semantics=("arbitrary", "arbitrary"),
                                             vmem_limit_bytes=VMEM_LIMIT),
        name="lat_mix",
    )(sink_perm, x, mod_lat, p32, p32, p32, qc, kc, kc, kc, vc, vc, vc,
      qd, kd, kd, kd, vd, vd, vd, cck, ccv, cdk, cdv, tab, w_out, ws_cat, bias_a, w_conv)


def _final_norm_kernel(x_ref, g_ref, o_ref):
    x = x_ref[...]
    ms = jnp.mean(x * x, axis=-1, keepdims=True)
    o_ref[...] = x * lax.rsqrt(ms + EPS) * g_ref[...]


def _final_norm(x2d, g):
    rows = x2d.shape[0]
    return pl.pallas_call(
        _final_norm_kernel,
        grid=(rows // 1024,),
        in_specs=[pl.BlockSpec((1024, D_MODEL), lambda i: (i, 0)),
                  pl.BlockSpec((1, D_MODEL), lambda i: (0, 0))],
        out_specs=pl.BlockSpec((1024, D_MODEL), lambda i: (i, 0)),
        out_shape=jax.ShapeDtypeStruct((rows, D_MODEL), F32),
        compiler_params=pltpu.CompilerParams(dimension_semantics=("arbitrary",)),
        name="final_norm",
    )(x2d, g)


def _rope_tables():
    t = np.arange(DEC_SEQ)
    freqs = ROPE_BASE ** (-jnp.arange(16, dtype=F32) / 16)
    ang_r = jnp.asarray(t // GRID_W, F32)[:, None] * freqs
    ang_c = jnp.asarray(t % GRID_W, F32)[:, None] * freqs
    cos_h = jnp.concatenate([jnp.cos(ang_r), jnp.cos(ang_r), jnp.cos(ang_c), jnp.cos(ang_c)], axis=1)
    sin_h = jnp.concatenate([-jnp.sin(ang_r), jnp.sin(ang_r), -jnp.sin(ang_c), jnp.sin(ang_c)], axis=1)
    return jnp.tile(cos_h, (1, 2)), jnp.tile(sin_h, (1, 2))


def _nbr_bias_tables(rpb):
    cols = np.arange(GRID_W)
    cstart = np.clip(cols - WIN_C // 2, 0, GRID_W - WIN_C)
    cc = cols[None, :]
    valid = (cc >= cstart[:, None]) & (cc < cstart[:, None] + WIN_C)
    dcol = np.clip(cc - cols[:, None] + (WIN_C - 1), 0, 2 * WIN_C - 2)
    ro = np.arange(WIN_R)[:, None] + np.arange(WIN_R)[None, :]
    t = rpb[:, :, ro[:, None, :, None], dcol[None, :, None, :]]
    t = jnp.where(valid[None, None, None, :, None, :], t, NEG)
    return t.reshape(DEPTH, 4, WIN_R, GRID_W, WIN_R * GRID_W)


def kernel(x_prompt, x_sample, cache_c_k, cache_c_v, cache_d_k, cache_d_v, c, c_ctx, norm_g, w_mod, b_mod,
           w_in, w_out, w_s, b_s, w_conv, sink, rpb, final_g):
    w_in_p = w_in[:, :, _COL_ORDER].astype(BF16)
    w_out_p = w_out[:, _OUT_ROW_ORDER, :].astype(BF16)
    ws_cat = jnp.transpose(w_s, (0, 2, 1, 3)).reshape(DEPTH, CHUNK, A_HEADS * CHUNK).astype(BF16)
    bias_a = jnp.repeat(jnp.transpose(b_s, (0, 2, 1)), HEAD_DIM, axis=2)
    sink_perm = sink[:, np.array([0, 1, 2, 3])]
    tab = _nbr_bias_tables(rpb)
    cos_t, sin_t = _rope_tables()
    cck = cache_c_k.reshape(DEC_BATCH, DEPTH, PAST_LEN, 128)
    ccv = cache_c_v.reshape(DEC_BATCH, DEPTH, PAST_LEN, 128)
    cdk = cache_d_k.reshape(DEC_BATCH, DEPTH, PAST_LEN, 256)
    cdv = cache_d_v.reshape(DEC_BATCH, DEPTH, PAST_LEN, 256)

    cond8 = jnp.zeros((8, D_MODEL), F32).at[0].set(c_ctx).at[1:3].set(c)
    mod = _modulation(cond8, w_mod, b_mod).reshape(DEPTH, 8, 3, D_MODEL)

    xp, xs = x_prompt, x_sample
    states = [[], [], [], []]
    for l in range(DEPTH):
        g = norm_g[l].reshape(1, D_MODEL)
        xp, ck, cv, dk, dv = _ctx_layer(xp, mod[l, 0], g, w_in_p[l], w_out_p[l], ws_cat[l], bias_a[l],
                                        w_conv[l], sink_perm[l])
        for lst, s in zip(states, (ck, cv, dk, dv)):
            lst.append(s)
        mod_lat = mod[l, 1:3]
        p32, qc, kc, vc, qd, kd, vd = _lat_proj(xs, mod_lat, g, w_in_p[l], cos_t, sin_t)
        xs = _lat_mix(l, xs, mod_lat, p32, qc, kc, vc, qd, kd, vd, cck, ccv, cdk, cdv, tab[l],
                      w_out_p[l], ws_cat[l], bias_a[l], w_conv[l], sink_perm[l])

    fg = final_g.reshape(1, D_MODEL)
    y_prompt = _final_norm(xp.reshape(BATCH * SEQ, D_MODEL), fg).reshape(BATCH, SEQ, D_MODEL)
    y_sample = _final_norm(xs.reshape(DEC_BATCH * DEC_SEQ, D_MODEL), fg).reshape(DEC_BATCH, DEC_SEQ, D_MODEL)
    state_c_k = jnp.stack(states[0], axis=1).reshape(BATCH, DEPTH, SEQ, 2, HEAD_DIM)
    state_c_v = jnp.stack(states[1], axis=1).reshape(BATCH, DEPTH, SEQ, 2, HEAD_DIM)
    state_d_k = jnp.stack(states[2], axis=1).reshape(BATCH, DEPTH, SEQ, 4, HEAD_DIM)
    state_d_v = jnp.stack(states[3], axis=1).reshape(BATCH, DEPTH, SEQ, 4, HEAD_DIM)
    return (y_prompt, y_sample, state_c_k, state_c_v, state_d_k, state_d_v)
```

```python
import functools

import numpy as np
import jax
import jax.numpy as jnp
from jax import lax
from jax.experimental import pallas as pl
from jax.experimental.pallas import tpu as pltpu

D_MODEL = 1024
BATCH = 16
SEQ = 256
DEPTH = 4
DEC_BATCH = 2
DEC_SEQ = 4096
PAST_LEN = 512
GRID_W = 64
GROUP_W = 256
HEAD_DIM = 64
A_HEADS = 4
CHUNK = 128
WINDOW = 128
QBLK = 128
WIN_R = 8
WIN_C = 16
ROPE_BASE = 10000.0
EPS = 1e-6
NEG = -1e30
D_IN = 3584

AU, AV, AZ, BB, BC, BH, BZ = 0, 256, 512, 768, 1024, 1280, 1536
CQ, CKV, CZ, DQ, DK, DV, DZ = 1792, 2048, 2304, 2560, 2816, 3072, 3328
N_AB = 1792

LANES = 128
TILE = 512
ROWS_PER_TILE = TILE // GRID_W
N_TILES = DEC_SEQ // TILE
N_GROUPS = 8
VMEM_LIMIT = 56 * 1024 * 1024

F32 = jnp.float32
BF16 = jnp.bfloat16


def _silu(z):
    return z * (1.0 / (1.0 + jnp.exp(-z)))


def _dot(a, b):
    return jnp.dot(a, b, preferred_element_type=F32)


def _dot_nt(a, b):
    return lax.dot_general(a, b, (((1,), (1,)), ((), ())), preferred_element_type=F32)


def _rms(x):
    return x * lax.rsqrt(jnp.mean(x * x, axis=-1, keepdims=True) + EPS)


def _norm_mod(x, g, shift, scale):
    return _rms(x) * g * (1.0 + scale) + shift


def _swap64(x):
    return pltpu.roll(x, HEAD_DIM, axis=1)


def _group_mean_matrix():
    r = lax.broadcasted_iota(jnp.int32, (GROUP_W, GROUP_W), 0) // HEAD_DIM
    c = lax.broadcasted_iota(jnp.int32, (GROUP_W, GROUP_W), 1) // HEAD_DIM
    return jnp.where(r == c, 1.0 / HEAD_DIM, 0.0).astype(BF16)


def _branch_a(au, av, az, ws, bias):
    t = av.shape[0]
    sq = av * av
    hi = sq.astype(BF16)
    lo = (sq - hi.astype(F32)).astype(BF16)
    gm = _group_mean_matrix()
    ms = _dot(hi, gm) + _dot(lo, gm)
    vh = (av * lax.rsqrt(ms + EPS)).astype(BF16)
    head = lax.broadcasted_iota(jnp.int32, (CHUNK, GROUP_W), 1) // HEAD_DIM
    outs = []
    for n in range(t // CHUNK):
        v = vh[n * CHUNK:(n + 1) * CHUNK]
        rhs = jnp.concatenate([jnp.where(head == h, v, jnp.zeros_like(v)) for h in range(A_HEADS)], axis=0)
        outs.append(_dot(ws, rhs) + bias)
    mixed = jnp.concatenate(outs, axis=0)
    return au * mixed * _silu(az)


def _branch_b(bb, bc, bh, bz, wc, prev_row, next_row):
    t = bb.shape[0]
    xc = bc * bh
    row = lax.broadcasted_iota(jnp.int32, xc.shape, 0)
    xm = jnp.where(row == 0, prev_row, pltpu.roll(xc, 1, axis=0))
    xp = jnp.where(row == t - 1, next_row, pltpu.roll(xc, t - 1, axis=0))
    y = wc[0:1, :] * xm + wc[1:2, :] * xc + wc[2:3, :] * xp
    return bb * y * _silu(bz)


def _softmax_pv(scores, values, sink=None):
    m = jnp.max(scores[0], axis=-1, keepdims=True)
    for s in scores[1:]:
        m = jnp.maximum(m, jnp.max(s, axis=-1, keepdims=True))
    if sink is not None:
        m = jnp.maximum(m, sink)
    l = None
    o = None
    for s, v in zip(scores, values):
        e = jnp.exp(s - m)
        ls = jnp.sum(e, axis=-1, keepdims=True)
        os = _dot(e.astype(BF16), v)
        l = ls if l is None else l + ls
        o = os if o is None else o + os
    if sink is not None:
        l = l + jnp.exp(sink - m)
    return o / l


def _low_half(m):
    return lax.broadcasted_iota(jnp.int32, (m, LANES), 1) < HEAD_DIM


def _mask_half(q, low, half):
    z = jnp.zeros_like(q)
    return jnp.where(low, q, z) if half == 0 else jnp.where(low, z, q)


def _out_proj(x, gate, ya, yb, yc, yd, w_out_ref):
    y = _dot(ya.astype(BF16), w_out_ref[0:256, :])
    y += _dot(yb.astype(BF16), w_out_ref[256:512, :])
    y += _dot(yc.astype(BF16), w_out_ref[512:768, :])
    y += _dot(yd.astype(BF16), w_out_ref[768:1024, :])
    return x + gate * y


def _mod_kernel(cond_ref, w_ref, b_ref, o_ref):
    o_ref[...] = jnp.dot(_silu(cond_ref[...]), w_ref[...], preferred_element_type=F32,
                         precision=lax.Precision.HIGHEST) + b_ref[...]


def _modulation(cond, w_mod, b_mod):
    return pl.pallas_call(
        _mod_kernel,
        grid=(DEPTH, 3),
        in_specs=[
            pl.BlockSpec((N_GROUPS, D_MODEL), lambda l, j: (0, 0)),
            pl.BlockSpec((None, D_MODEL, D_MODEL), lambda l, j: (l, 0, j)),
            pl.BlockSpec((None, 1, D_MODEL), lambda l, j: (l, 0, j)),
        ],
        out_specs=pl.BlockSpec((None, None, N_GROUPS, D_MODEL), lambda l, j: (l, j, 0, 0)),
        out_shape=jax.ShapeDtypeStruct((DEPTH, 3, N_GROUPS, D_MODEL), F32),
        compiler_params=pltpu.CompilerParams(dimension_semantics=("arbitrary", "arbitrary")),
        name="modulation",
    )(cond, w_mod, b_mod.reshape(DEPTH, 1, 3 * D_MODEL))


def _nbr_table_kernel(rpb_ref, o_ref):
    c = lax.broadcasted_iota(jnp.int32, (GRID_W, LANES), 0)
    lane = lax.broadcasted_iota(jnp.int32, (GRID_W, LANES), 1)
    cc = lane % GRID_W
    cstart = jnp.clip(c - WIN_C // 2, 0, GRID_W - WIN_C)
    low = lane < GRID_W
    t_low = []
    t_high = []
    for ro in range(2 * WIN_R - 1):
        x = jnp.broadcast_to(rpb_ref[ro:ro + 1, :], (GRID_W, LANES))
        t_low.append(pltpu.roll(x, LANES - (WIN_C - 1), 1, stride=1, stride_axis=0))
        t_high.append(pltpu.roll(x, GRID_W - (WIN_C - 1), 1, stride=1, stride_axis=0))
    for d in range(WIN_R):
        for p in range(WIN_R // 2):
            v = jnp.where(low, t_low[d + 2 * p], t_high[d + 2 * p + 1])
            v = jnp.where(cc >= cstart, jnp.where(cc < cstart + WIN_C, v, NEG), NEG)
            o_ref[d, :, p * LANES:(p + 1) * LANES] = v


def _nbr_tables(rpb):
    rpb_p = jnp.pad(rpb.reshape(DEPTH * 4, 2 * WIN_R - 1, 2 * WIN_C - 1), ((0, 0), (0, 1), (0, LANES - 31)))
    out = pl.pallas_call(
        _nbr_table_kernel,
        grid=(DEPTH * 4,),
        in_specs=[pl.BlockSpec((None, 2 * WIN_R, LANES), lambda i: (i, 0, 0))],
        out_specs=pl.BlockSpec((None, WIN_R, GRID_W, WIN_R * GRID_W), lambda i: (i, 0, 0, 0)),
        out_shape=jax.ShapeDtypeStruct((DEPTH * 4, WIN_R, GRID_W, WIN_R * GRID_W), F32),
        compiler_params=pltpu.CompilerParams(dimension_semantics=("arbitrary",)),
        name="nbr_tables",
    )(rpb_p)
    return out.reshape(DEPTH, 4, WIN_R, GRID_W, WIN_R * GRID_W)


def _ctx_kernel(sink_ref, x_ref, mod_ref, g_ref, fg_ref, w_in_ref, w_out_ref, ws_ref, ba_ref, wc_ref,
                y_ref, ck_ref, cv_ref, dk_ref, dv_ref, xs):
    l = pl.program_id(0)
    b = pl.program_id(1)

    @pl.when(l == 0)
    def _():
        xs[b] = x_ref[...]

    x = xs[b]
    h = _norm_mod(x, g_ref[...], mod_ref[0, 0:1, :], mod_ref[1, 0:1, :]).astype(BF16)
    p = _dot(h, w_in_ref[...])
    kcf = p[:, CKV:CKV + 128]
    vcf = p[:, CKV + 128:CKV + 256]
    kdf = p[:, DK:DK + 256]
    vdf = p[:, DV:DV + 256]
    ck_ref[...] = kcf
    cv_ref[...] = vcf
    dk_ref[...] = kdf
    dv_ref[...] = vdf

    ya = _branch_a(p[:, AU:AU + 256], p[:, AV:AV + 256], p[:, AZ:AZ + 256], ws_ref[...], ba_ref[...])
    zero_row = jnp.zeros((1, GROUP_W), F32)
    yb = _branch_b(p[:, BB:BB + 256], p[:, BC:BC + 256], p[:, BH:BH + 256], p[:, BZ:BZ + 256],
                   wc_ref[...], zero_row, zero_row)

    low = _low_half(SEQ)

    kc = (kcf.astype(BF16), _swap64(kcf).astype(BF16))
    vc = (vcf.astype(BF16), _swap64(vcf).astype(BF16))
    c_tiles = []
    for t in range(2):
        q = (p[:, CQ + t * LANES:CQ + (t + 1) * LANES] * 0.125).astype(BF16)
        outs = []
        for half in range(2):
            sw = (t + half) % 2
            s = _dot_nt(_mask_half(q, low, half), kc[sw])
            outs.append(_softmax_pv([s], [vc[sw]], sink_ref[l, 2 * t + half]))
        c_tiles.append(jnp.where(low, outs[0], outs[1]))
    yc = jnp.concatenate(c_tiles, axis=1) * _silu(p[:, CZ:CZ + 256])

    d_tiles = []
    for t in range(2):
        q = (p[:, DQ + t * LANES:DQ + (t + 1) * LANES] * 0.125).astype(BF16)
        k = kdf[:, t * LANES:(t + 1) * LANES].astype(BF16)
        v = vdf[:, t * LANES:(t + 1) * LANES].astype(BF16)
        outs = [_softmax_pv([_dot_nt(_mask_half(q, low, half), k)], [v]) for half in range(2)]
        d_tiles.append(jnp.where(low, outs[0], outs[1]))
    yd = jnp.concatenate(d_tiles, axis=1) * _silu(p[:, DZ:DZ + 256])

    x_new = _out_proj(x, mod_ref[2, 0:1, :], ya, yb, yc, yd, w_out_ref)
    xs[b] = x_new

    @pl.when(l == DEPTH - 1)
    def _():
        y_ref[...] = _rms(x_new) * fg_ref[...]


def _ctx_layers(x, mod, norm_g, final_g, w_in, w_out, ws_cat, bias_a, w_conv, sink):
    per_layer = lambda *shape: pl.BlockSpec((None,) + shape, lambda l, b: (l,) + (0,) * len(shape))
    state = lambda w: pl.BlockSpec((None, None, SEQ, w), lambda l, b: (b, l, 0, 0))
    return pl.pallas_call(
        _ctx_kernel,
        grid=(DEPTH, BATCH),
        in_specs=[
            pl.BlockSpec(memory_space=pltpu.SMEM),
            pl.BlockSpec((None, SEQ, D_MODEL), lambda l, b: (jnp.where(l == 0, b, BATCH - 1), 0, 0)),
            per_layer(3, N_GROUPS, D_MODEL),
            per_layer(1, D_MODEL),
            pl.BlockSpec((1, D_MODEL), lambda l, b: (0, 0)),
            pl.BlockSpec((None, D_MODEL, D_IN), lambda l, b: (l, 0, 0), pipeline_mode=pl.Buffered(1)),
            pl.BlockSpec((None, D_MODEL, D_MODEL), lambda l, b: (l, 0, 0), pipeline_mode=pl.Buffered(1)),
            per_layer(CHUNK, A_HEADS * CHUNK),
            per_layer(CHUNK, GROUP_W),
            per_layer(3, GROUP_W),
        ],
        out_specs=[
            pl.BlockSpec((None, SEQ, D_MODEL), lambda l, b: (jnp.where(l == DEPTH - 1, b, 0), 0, 0)),
            state(128), state(128), state(256), state(256),
        ],
        out_shape=[
            jax.ShapeDtypeStruct((BATCH, SEQ, D_MODEL), F32),
            jax.ShapeDtypeStruct((BATCH, DEPTH, SEQ, 128), F32),
            jax.ShapeDtypeStruct((BATCH, DEPTH, SEQ, 128), F32),
            jax.ShapeDtypeStruct((BATCH, DEPTH, SEQ, 256), F32),
            jax.ShapeDtypeStruct((BATCH, DEPTH, SEQ, 256), F32),
        ],
        scratch_shapes=[pltpu.VMEM((BATCH, SEQ, D_MODEL), F32)],
        compiler_params=pltpu.CompilerParams(dimension_semantics=("arbitrary", "arbitrary"),
                                             vmem_limit_bytes=VMEM_LIMIT),
        name="ctx_layers",
    )(sink, x, mod, norm_g, final_g, w_in, w_out, ws_cat, bias_a, w_conv)


def _rope(x, cos, sin_signed, first_half):
    swapped = jnp.where(first_half, pltpu.roll(x, LANES - 16, axis=1), pltpu.roll(x, 16, axis=1))
    return x * cos + swapped * sin_signed


def _lat_proj_kernel(x_ref, mod_ref, g_ref, w_ref, cos_ref, sin_ref,
                     ab_ref, z_ref, qc_ref, kc_ref, vc_ref, qd_ref, kd_ref, vd_ref):
    grp = pl.ds(1 + pl.program_id(0), 1)
    h = _norm_mod(x_ref[...], g_ref[...], mod_ref[0, grp, :], mod_ref[1, grp, :]).astype(BF16)
    for c0 in range(0, N_AB, 896):
        ab_ref[:, c0:c0 + 896] = _dot(h, w_ref[:, c0:c0 + 896])
    z_ref[:, 0:256] = _dot(h, w_ref[:, CZ:CZ + 256])
    z_ref[:, 256:512] = _dot(h, w_ref[:, DZ:DZ + 256])
    cos = cos_ref[...]
    sin = sin_ref[...]
    first_half = (lax.broadcasted_iota(jnp.int32, (TILE, LANES), 1) % 32) < 16
    qc = _dot(h, w_ref[:, CQ:CQ + 256])
    for t in range(2):
        qt = _rope(qc[:, t * LANES:(t + 1) * LANES], cos, sin, first_half) * 0.125
        qc_ref[:, t * LANES:(t + 1) * LANES] = qt.astype(BF16)
    kv = _dot(h, w_ref[:, CKV:CKV + 256])
    k = _rope(kv[:, 0:LANES], cos, sin, first_half)
    v = kv[:, LANES:2 * LANES]
    kc_ref[:, 0:LANES] = k.astype(BF16)
    kc_ref[:, LANES:2 * LANES] = _swap64(k).astype(BF16)
    vc_ref[:, 0:LANES] = v.astype(BF16)
    vc_ref[:, LANES:2 * LANES] = _swap64(v).astype(BF16)
    qd_ref[...] = (_dot(h, w_ref[:, DQ:DQ + 256]) * 0.125).astype(BF16)
    kd_ref[...] = _dot(h, w_ref[:, DK:DK + 256]).astype(BF16)
    vd_ref[...] = _dot(h, w_ref[:, DV:DV + 256]).astype(BF16)


def _lat_proj(layer, x, mod, norm_g, w_in, cos_t, sin_t):
    tile = lambda w: pl.BlockSpec((None, TILE, w), lambda b, i: (b, i, 0))
    per_layer = lambda *shape: pl.BlockSpec((None,) + shape, lambda b, i: (layer,) + (0,) * len(shape))
    sds = lambda w, dt: jax.ShapeDtypeStruct((DEC_BATCH, DEC_SEQ, w), dt)
    return pl.pallas_call(
        _lat_proj_kernel,
        grid=(DEC_BATCH, N_TILES),
        in_specs=[
            tile(D_MODEL),
            per_layer(3, N_GROUPS, D_MODEL),
            per_layer(1, D_MODEL),
            per_layer(D_MODEL, D_IN),
            pl.BlockSpec((TILE, LANES), lambda b, i: (i, 0)),
            pl.BlockSpec((TILE, LANES), lambda b, i: (i, 0)),
        ],
        out_specs=[tile(N_AB), tile(512)] + [tile(256)] * 6,
        out_shape=[sds(N_AB, F32), sds(512, F32)] + [sds(256, BF16)] * 6,
        compiler_params=pltpu.CompilerParams(dimension_semantics=("arbitrary", "arbitrary"),
                                             vmem_limit_bytes=VMEM_LIMIT),
        name="lat_proj",
    )(x, mod, norm_g, w_in, cos_t, sin_t)


def _lat_mix_kernel(layer, sink_ref, x_ref, mod_ref, fg_ref, ab_ref, abp_ref, abn_ref, z_ref,
                    qc_ref, kc_ref, kcp_ref, kcn_ref, vc_ref, vcp_ref, vcn_ref,
                    qd_ref, kd_ref, kdp_ref, kdn_ref, vd_ref, vdp_ref, vdn_ref,
                    cck_ref, ccv_ref, cdk_ref, cdv_ref, tab_ref,
                    w_out_ref, ws_ref, ba_ref, wc_ref,
                    xo_ref,
                    kce, vce, kde, vde, yc_scr, yd_scr):
    b = pl.program_id(0)
    i = pl.program_id(1)

    ya = _branch_a(ab_ref[:, AU:AU + 256], ab_ref[:, AV:AV + 256], ab_ref[:, AZ:AZ + 256],
                   ws_ref[...], ba_ref[...])

    prev_row = jnp.where(i > 0, abp_ref[7:8, 0:256] * abp_ref[7:8, 256:512], 0.0)
    next_row = jnp.where(i < N_TILES - 1, abn_ref[0:1, 0:256] * abn_ref[0:1, 256:512], 0.0)
    yb = _branch_b(ab_ref[:, BB:BB + 256], ab_ref[:, BC:BC + 256], ab_ref[:, BH:BH + 256],
                   ab_ref[:, BZ:BZ + 256], wc_ref[...], prev_row, next_row)

    kce[0:QBLK, :] = kcp_ref[...]
    kce[QBLK:QBLK + TILE, :] = kc_ref[...]
    kce[QBLK + TILE:, :] = kcn_ref[...]
    vce[0:QBLK, :] = vcp_ref[...]
    vce[QBLK:QBLK + TILE, :] = vc_ref[...]
    vce[QBLK + TILE:, :] = vcn_ref[...]
    cck = cck_ref[...]
    ccv = ccv_ref[...]
    k_ctx = (cck.astype(BF16), _swap64(cck).astype(BF16))
    v_ctx = (ccv.astype(BF16), _swap64(ccv).astype(BF16))
    low_q = _low_half(QBLK)
    p_i = lax.broadcasted_iota(jnp.int32, (QBLK, 3 * QBLK), 0)
    j_i = lax.broadcasted_iota(jnp.int32, (QBLK, 3 * QBLK), 1)
    n_blocks = DEC_SEQ // QBLK

    def c_block(j, carry):
        n = i * (TILE // QBLK) + j
        r0 = pl.multiple_of(j * QBLK, QBLK)
        lower = jnp.maximum(p_i, jnp.where(n == 0, QBLK, 0))
        upper = jnp.minimum(p_i + 2 * WINDOW, jnp.where(n == n_blocks - 1, 2 * QBLK - 1, 3 * QBLK - 1))
        tiles = []
        for t in range(2):
            q = qc_ref[pl.ds(r0, QBLK), t * LANES:(t + 1) * LANES]
            outs = []
            for half in range(2):
                sw = (t + half) % 2
                k_loc = kce[pl.ds(r0, 3 * QBLK), sw * LANES:(sw + 1) * LANES]
                v_loc = vce[pl.ds(r0, 3 * QBLK), sw * LANES:(sw + 1) * LANES]
                qm = _mask_half(q, low_q, half)
                s_loc = _dot_nt(qm, k_loc)
                s_loc = jnp.where(j_i >= lower, jnp.where(j_i <= upper, s_loc, NEG), NEG)
                s_ctx = _dot_nt(qm, k_ctx[sw])
                outs.append(_softmax_pv([s_loc, s_ctx], [v_loc, v_ctx[sw]], sink_ref[layer, 2 * t + half]))
            tiles.append(jnp.where(low_q, outs[0], outs[1]))
        yc_scr[pl.ds(r0, QBLK), :] = jnp.concatenate(tiles, axis=1)
        return carry

    lax.fori_loop(0, TILE // QBLK, c_block, 0)

    halo = 4 * GRID_W
    kde[0:halo, :] = kdp_ref[...]
    kde[halo:halo + TILE, :] = kd_ref[...]
    kde[halo + TILE:, :] = kdn_ref[...]
    vde[0:halo, :] = vdp_ref[...]
    vde[halo:halo + TILE, :] = vd_ref[...]
    vde[halo + TILE:, :] = vdn_ref[...]
    kd_ctx = cdk_ref[...].astype(BF16)
    vd_ctx = cdv_ref[...].astype(BF16)
    low_r = _low_half(GRID_W)
    n_rows = DEC_SEQ // GRID_W

    def d_row(rl, carry):
        r = i * ROWS_PER_TILE + rl
        rs = jnp.clip(r - WIN_R // 2, 0, n_rows - WIN_R)
        didx = rs - r + (WIN_R - 1)
        off = pl.multiple_of((rs - i * ROWS_PER_TILE + 4) * GRID_W, GRID_W)
        q0 = pl.multiple_of(rl * GRID_W, GRID_W)
        tiles = []
        for t in range(2):
            q = qd_ref[pl.ds(q0, GRID_W), t * LANES:(t + 1) * LANES]
            k_loc = kde[pl.ds(off, WIN_R * GRID_W), t * LANES:(t + 1) * LANES]
            v_loc = vde[pl.ds(off, WIN_R * GRID_W), t * LANES:(t + 1) * LANES]
            k_c = kd_ctx[:, t * LANES:(t + 1) * LANES]
            v_c = vd_ctx[:, t * LANES:(t + 1) * LANES]
            outs = []
            for half in range(2):
                qm = _mask_half(q, low_r, half)
                s_loc = _dot_nt(qm, k_loc) + tab_ref[2 * t + half, didx]
                s_ctx = _dot_nt(qm, k_c)
                outs.append(_softmax_pv([s_loc, s_ctx], [v_loc, v_c]))
            tiles.append(jnp.where(low_r, outs[0], outs[1]))
        yd_scr[pl.ds(q0, GRID_W), :] = jnp.concatenate(tiles, axis=1)
        return carry

    lax.fori_loop(0, ROWS_PER_TILE, d_row, 0)

    yc = yc_scr[...] * _silu(z_ref[:, 0:256])
    yd = yd_scr[...] * _silu(z_ref[:, 256:512])
    x_new = _out_proj(x_ref[...], mod_ref[2, pl.ds(1 + b, 1), :], ya, yb, yc, yd, w_out_ref)
    if layer == DEPTH - 1:
        x_new = _rms(x_new) * fg_ref[...]
    xo_ref[...] = x_new


def _lat_mix(layer, x, mod, final_g, ab, z, qc, kc, vc, qd, kd, vd, cck, ccv, cdk, cdv, tab,
             w_out, ws_cat, bias_a, w_conv, sink):
    tile = lambda w: pl.BlockSpec((None, TILE, w), lambda b, i: (b, i, 0))
    per_layer = lambda *shape: pl.BlockSpec((None,) + shape, lambda b, i: (layer,) + (0,) * len(shape))

    def halo(rows, w, col=0):
        per = TILE // rows
        last = DEC_SEQ // rows - 1
        prev = pl.BlockSpec((None, rows, w), lambda b, i: (b, jnp.maximum(i * per - 1, 0), col))
        nxt = pl.BlockSpec((None, rows, w), lambda b, i: (b, jnp.minimum((i + 1) * per, last), col))
        return prev, nxt

    abp, abn = halo(8, 512, col=BC // 512)
    cprev, cnext = halo(QBLK, 256)
    dprev, dnext = halo(4 * GRID_W, 256)
    cache = lambda w: pl.BlockSpec((None, None, PAST_LEN, w), lambda b, i: (b, layer, 0, 0))
    return pl.pallas_call(
        functools.partial(_lat_mix_kernel, layer),
        grid=(DEC_BATCH, N_TILES),
        in_specs=[
            pl.BlockSpec(memory_space=pltpu.SMEM),
            tile(D_MODEL),
            per_layer(3, N_GROUPS, D_MODEL),
            pl.BlockSpec((1, D_MODEL), lambda b, i: (0, 0)),
            tile(N_AB), abp, abn, tile(512),
            tile(256), tile(256), cprev, cnext, tile(256), cprev, cnext,
            tile(256), tile(256), dprev, dnext, tile(256), dprev, dnext,
            cache(128), cache(128), cache(256), cache(256),
            per_layer(4, WIN_R, GRID_W, WIN_R * GRID_W),
            per_layer(D_MODEL, D_MODEL),
            per_layer(CHUNK, A_HEADS * CHUNK),
            per_layer(CHUNK, GROUP_W),
            per_layer(3, GROUP_W),
        ],
        out_specs=tile(D_MODEL),
        out_shape=jax.ShapeDtypeStruct((DEC_BATCH, DEC_SEQ, D_MODEL), F32),
        scratch_shapes=[
            pltpu.VMEM((TILE + 2 * QBLK, 256), BF16),
            pltpu.VMEM((TILE + 2 * QBLK, 256), BF16),
            pltpu.VMEM((TILE + 8 * GRID_W, 256), BF16),
            pltpu.VMEM((TILE + 8 * GRID_W, 256), BF16),
            pltpu.VMEM((TILE, 256), F32),
            pltpu.VMEM((TILE, 256), F32),
        ],
        compiler_params=pltpu.CompilerParams(dimension_semantics=("arbitrary", "arbitrary"),
                                             vmem_limit_bytes=VMEM_LIMIT),
        name="lat_mix",
    )(sink, x, mod, final_g, ab, ab, ab, z, qc, kc, kc, kc, vc, vc, vc,
      qd, kd, kd, kd, vd, vd, vd, cck, ccv, cdk, cdv, tab, w_out, ws_cat, bias_a, w_conv)


def _rope_tables():
    t = np.arange(DEC_SEQ)
    freqs = (np.float32(ROPE_BASE) ** (-np.arange(16, dtype=np.float32) / np.float32(16))).astype(np.float32)
    ang_r = (t // GRID_W).astype(np.float32)[:, None] * freqs
    ang_c = (t % GRID_W).astype(np.float32)[:, None] * freqs
    cos_h = np.concatenate([np.cos(ang_r), np.cos(ang_r), np.cos(ang_c), np.cos(ang_c)], axis=1)
    sin_h = np.concatenate([-np.sin(ang_r), np.sin(ang_r), -np.sin(ang_c), np.sin(ang_c)], axis=1)
    return (jnp.asarray(np.tile(cos_h, (1, 2)), F32), jnp.asarray(np.tile(sin_h, (1, 2)), F32))


def kernel(x_prompt, x_sample, cache_c_k, cache_c_v, cache_d_k, cache_d_v, c, c_ctx, norm_g, w_mod, b_mod,
           w_in, w_out, w_s, b_s, w_conv, sink, rpb, final_g):
    w_in_b = w_in.astype(BF16)
    w_out_b = w_out.astype(BF16)
    ws_cat = jnp.transpose(w_s, (0, 2, 1, 3)).reshape(DEPTH, CHUNK, A_HEADS * CHUNK).astype(BF16)
    bias_a = jnp.repeat(jnp.transpose(b_s, (0, 2, 1)), HEAD_DIM, axis=2)
    norm_g3 = norm_g.reshape(DEPTH, 1, D_MODEL)
    fg = final_g.reshape(1, D_MODEL)
    cos_t, sin_t = _rope_tables()
    cck = cache_c_k.reshape(DEC_BATCH, DEPTH, PAST_LEN, 128)
    ccv = cache_c_v.reshape(DEC_BATCH, DEPTH, PAST_LEN, 128)
    cdk = cache_d_k.reshape(DEC_BATCH, DEPTH, PAST_LEN, 256)
    cdv = cache_d_v.reshape(DEC_BATCH, DEPTH, PAST_LEN, 256)

    cond = jnp.concatenate([c_ctx[None, :], c, jnp.zeros((N_GROUPS - 1 - DEC_BATCH, D_MODEL), F32)], axis=0)
    mod = _modulation(cond, w_mod, b_mod)
    tab = _nbr_tables(rpb)

    y_prompt, s_ck, s_cv, s_dk, s_dv = _ctx_layers(x_prompt, mod, norm_g3, fg, w_in_b, w_out_b, ws_cat,
                                                   bias_a, w_conv, sink)
    xs = x_sample
    for l in range(DEPTH):
        ab, z, qc, kc, vc, qd, kd, vd = _lat_proj(l, xs, mod, norm_g3, w_in_b, cos_t, sin_t)
        xs = _lat_mix(l, xs, mod, fg, ab, z, qc, kc, vc, qd, kd, vd, cck, ccv, cdk, cdv, tab,
                      w_out_b, ws_cat, bias_a, w_conv, sink)

    shape_c = (BATCH, DEPTH, SEQ, 2, HEAD_DIM)
    shape_d = (BATCH, DEPTH, SEQ, 4, HEAD_DIM)
    return (y_prompt, xs, s_ck.reshape(shape_c), s_cv.reshape(shape_c), s_dk.reshape(shape_d),
            s_dv.reshape(shape_d))
```

```python
import functools

import numpy as np
import jax
import jax.numpy as jnp
from jax import lax
from jax.experimental import pallas as pl
from jax.experimental.pallas import tpu as pltpu

D_MODEL = 1024
BATCH = 16
SEQ = 256
DEPTH = 4
DEC_BATCH = 2
DEC_SEQ = 4096
PAST_LEN = 512
GRID_W = 64
GROUP_W = 256
HEAD_DIM = 64
A_HEADS = 4
CHUNK = 128
WINDOW = 128
QBLK = 128
WIN_R = 8
WIN_C = 16
ROPE_BASE = 10000.0
EPS = 1e-6
NEG = -1e30
D_IN = 3584
LOG2E = 1.4426950408889634
Q_SCALE = HEAD_DIM ** -0.5 * LOG2E

AU, AV, AZ, BB, BC, BH, BZ = 0, 256, 512, 768, 1024, 1280, 1536
CQ, CKV, CZ, DQ, DK, DV, DZ = 1792, 2048, 2304, 2560, 2816, 3072, 3328
N_AB = 1792

LANES = 128
TILE = 512
ROWS_PER_TILE = TILE // GRID_W
N_TILES = DEC_SEQ // TILE
N_GROUPS = 8
VMEM_LIMIT = 56 * 1024 * 1024

F32 = jnp.float32
BF16 = jnp.bfloat16


def _silu(z):
    return z * (1.0 / (1.0 + jnp.exp(-z)))


def _dot(a, b):
    return jnp.dot(a, b, preferred_element_type=F32)


def _dot_nt(a, b):
    return lax.dot_general(a, b, (((1,), (1,)), ((), ())), preferred_element_type=F32)


def _rms(x):
    return x * lax.rsqrt(jnp.mean(x * x, axis=-1, keepdims=True) + EPS)


def _norm_mod(x, g, shift, scale):
    return _rms(x) * g * (1.0 + scale) + shift


def _swap64(x):
    return pltpu.roll(x, HEAD_DIM, axis=1)


def _group_mean_matrix():
    r = lax.broadcasted_iota(jnp.int32, (GROUP_W, GROUP_W), 0) // HEAD_DIM
    c = lax.broadcasted_iota(jnp.int32, (GROUP_W, GROUP_W), 1) // HEAD_DIM
    return jnp.where(r == c, 1.0 / HEAD_DIM, 0.0).astype(BF16)


def _branch_a(au, av, az, ws, bias):
    t = av.shape[0]
    sq = av * av
    hi = sq.astype(BF16)
    lo = (sq - hi.astype(F32)).astype(BF16)
    gm = _group_mean_matrix()
    ms = _dot(hi, gm) + _dot(lo, gm)
    vh = (av * lax.rsqrt(ms + EPS)).astype(BF16)
    head = lax.broadcasted_iota(jnp.int32, (CHUNK, GROUP_W), 1) // HEAD_DIM
    outs = []
    for n in range(t // CHUNK):
        v = vh[n * CHUNK:(n + 1) * CHUNK]
        rhs = jnp.concatenate([jnp.where(head == h, v, jnp.zeros_like(v)) for h in range(A_HEADS)], axis=0)
        outs.append(_dot(ws, rhs) + bias)
    mixed = jnp.concatenate(outs, axis=0)
    return au * mixed * _silu(az)


def _branch_b(bb, bc, bh, bz, wc, prev_row, next_row):
    t = bb.shape[0]
    xc = bc * bh
    row = lax.broadcasted_iota(jnp.int32, xc.shape, 0)
    xm = jnp.where(row == 0, prev_row, pltpu.roll(xc, 1, axis=0))
    xp = jnp.where(row == t - 1, next_row, pltpu.roll(xc, t - 1, axis=0))
    y = wc[0:1, :] * xm + wc[1:2, :] * xc + wc[2:3, :] * xp
    return bb * y * _silu(bz)


def _attend_heads(scores, values, sinks=None):
    n_heads = len(scores[0])
    rows = scores[0][0].shape[0]
    stacked = [jnp.concatenate(seg, axis=0) for seg in scores]
    m = functools.reduce(jnp.maximum, [jnp.max(s, axis=-1, keepdims=True) for s in stacked])
    if sinks is not None:
        sink = jnp.concatenate([jnp.full((rows, 1), s * LOG2E, F32) for s in sinks], axis=0)
        m = jnp.maximum(m, sink)
        sink_term = jnp.exp2(sink - m)
    ebs = [jnp.exp2(s - m).astype(BF16) for s in stacked]
    outs = []
    for h in range(n_heads):
        r = slice(h * rows, (h + 1) * rows)
        acc = functools.reduce(jnp.add, [_dot(eb[r], v[h][0]) for eb, v in zip(ebs, values)])
        pos = values[0][h][1]
        o = acc[:, pos * LANES:(pos + 1) * LANES]
        l = acc[:, (1 - pos) * LANES:(2 - pos) * LANES]
        if sinks is not None:
            l = l + sink_term[r]
        outs.append(o * (1.0 / l))
    return outs


def _low_half(m):
    return lax.broadcasted_iota(jnp.int32, (m, LANES), 1) < HEAD_DIM


def _mask_half(q, low, half):
    z = jnp.zeros_like(q)
    return jnp.where(low, q, z) if half == 0 else jnp.where(low, z, q)


def _out_proj(x, gate, ya, yb, yc, yd, w_out_ref):
    y = _dot(ya.astype(BF16), w_out_ref[0:256, :])
    y += _dot(yb.astype(BF16), w_out_ref[256:512, :])
    y += _dot(yc.astype(BF16), w_out_ref[512:768, :])
    y += _dot(yd.astype(BF16), w_out_ref[768:1024, :])
    return x + gate * y


def _mod_kernel(cond_ref, w_ref, b_ref, o_ref):
    o_ref[...] = jnp.dot(_silu(cond_ref[...]), w_ref[...], preferred_element_type=F32,
                         precision=lax.Precision.HIGHEST) + b_ref[...]


def _modulation(cond, w_mod, b_mod):
    return pl.pallas_call(
        _mod_kernel,
        grid=(DEPTH, 3),
        in_specs=[
            pl.BlockSpec((N_GROUPS, D_MODEL), lambda l, j: (0, 0)),
            pl.BlockSpec((None, D_MODEL, D_MODEL), lambda l, j: (l, 0, j)),
            pl.BlockSpec((None, 1, D_MODEL), lambda l, j: (l, 0, j)),
        ],
        out_specs=pl.BlockSpec((None, None, N_GROUPS, D_MODEL), lambda l, j: (l, j, 0, 0)),
        out_shape=jax.ShapeDtypeStruct((DEPTH, 3, N_GROUPS, D_MODEL), F32),
        compiler_params=pltpu.CompilerParams(dimension_semantics=("arbitrary", "arbitrary")),
        name="modulation",
    )(cond, w_mod, b_mod.reshape(DEPTH, 1, 3 * D_MODEL))


def _nbr_table_kernel(rpb_ref, o_ref):
    c = lax.broadcasted_iota(jnp.int32, (GRID_W, LANES), 0)
    lane = lax.broadcasted_iota(jnp.int32, (GRID_W, LANES), 1)
    cc = lane % GRID_W
    cstart = jnp.clip(c - WIN_C // 2, 0, GRID_W - WIN_C)
    low = lane < GRID_W
    t_low = []
    t_high = []
    for ro in range(2 * WIN_R - 1):
        x = jnp.broadcast_to(rpb_ref[ro:ro + 1, :] * LOG2E, (GRID_W, LANES))
        t_low.append(pltpu.roll(x, LANES - (WIN_C - 1), 1, stride=1, stride_axis=0))
        t_high.append(pltpu.roll(x, GRID_W - (WIN_C - 1), 1, stride=1, stride_axis=0))
    for d in range(WIN_R):
        for p in range(WIN_R // 2):
            v = jnp.where(low, t_low[d + 2 * p], t_high[d + 2 * p + 1])
            v = jnp.where(cc >= cstart, jnp.where(cc < cstart + WIN_C, v, NEG), NEG)
            o_ref[d, :, p * LANES:(p + 1) * LANES] = v


def _nbr_tables(rpb):
    rpb_p = jnp.pad(rpb.reshape(DEPTH * 4, 2 * WIN_R - 1, 2 * WIN_C - 1), ((0, 0), (0, 1), (0, LANES - 31)))
    out = pl.pallas_call(
        _nbr_table_kernel,
        grid=(DEPTH * 4,),
        in_specs=[pl.BlockSpec((None, 2 * WIN_R, LANES), lambda i: (i, 0, 0))],
        out_specs=pl.BlockSpec((None, WIN_R, GRID_W, WIN_R * GRID_W), lambda i: (i, 0, 0, 0)),
        out_shape=jax.ShapeDtypeStruct((DEPTH * 4, WIN_R, GRID_W, WIN_R * GRID_W), F32),
        compiler_params=pltpu.CompilerParams(dimension_semantics=("arbitrary",)),
        name="nbr_tables",
    )(rpb_p)
    return out.reshape(DEPTH, 4, WIN_R, GRID_W, WIN_R * GRID_W)


def _ctx_kernel(sink_ref, x_ref, mod_ref, g_ref, fg_ref, w_in_ref, w_out_ref, ws_ref, ba_ref, wc_ref,
                y_ref, ck_ref, cv_ref, dk_ref, dv_ref, xs):
    l = pl.program_id(0)
    b = pl.program_id(1)

    @pl.when(l == 0)
    def _():
        xs[b] = x_ref[...]

    x = xs[b]
    h = _norm_mod(x, g_ref[...], mod_ref[0, 0:1, :], mod_ref[1, 0:1, :]).astype(BF16)
    p = _dot(h, w_in_ref[...])
    kcf = p[:, CKV:CKV + 128]
    vcf = p[:, CKV + 128:CKV + 256]
    kdf = p[:, DK:DK + 256]
    vdf = p[:, DV:DV + 256]
    ck_ref[...] = kcf
    cv_ref[...] = vcf
    dk_ref[...] = kdf
    dv_ref[...] = vdf

    ya = _branch_a(p[:, AU:AU + 256], p[:, AV:AV + 256], p[:, AZ:AZ + 256], ws_ref[...], ba_ref[...])
    zero_row = jnp.zeros((1, GROUP_W), F32)
    yb = _branch_b(p[:, BB:BB + 256], p[:, BC:BC + 256], p[:, BH:BH + 256], p[:, BZ:BZ + 256],
                   wc_ref[...], zero_row, zero_row)

    low = _low_half(SEQ)

    ones = jnp.ones((SEQ, LANES), BF16)
    kc = (kcf.astype(BF16), _swap64(kcf).astype(BF16))
    vc = ((jnp.concatenate([vcf.astype(BF16), ones], axis=1), 0),
          (jnp.concatenate([ones, _swap64(vcf).astype(BF16)], axis=1), 1))
    scores, values, sinks = [], [], []
    for t in range(2):
        q = (p[:, CQ + t * LANES:CQ + (t + 1) * LANES] * Q_SCALE).astype(BF16)
        for half in range(2):
            sw = (t + half) % 2
            scores.append(_dot_nt(_mask_half(q, low, half), kc[sw]))
            values.append(vc[sw])
            sinks.append(sink_ref[l, 2 * t + half])
    o = _attend_heads([scores], [values], sinks)
    yc = jnp.concatenate([jnp.where(low, o[0], o[1]), jnp.where(low, o[2], o[3])], axis=1)
    yc = yc * _silu(p[:, CZ:CZ + 256])

    scores, values = [], []
    for t in range(2):
        q = (p[:, DQ + t * LANES:DQ + (t + 1) * LANES] * Q_SCALE).astype(BF16)
        k = kdf[:, t * LANES:(t + 1) * LANES].astype(BF16)
        v = (jnp.concatenate([vdf[:, t * LANES:(t + 1) * LANES].astype(BF16), ones], axis=1), 0)
        for half in range(2):
            scores.append(_dot_nt(_mask_half(q, low, half), k))
            values.append(v)
    o = _attend_heads([scores], [values])
    yd = jnp.concatenate([jnp.where(low, o[0], o[1]), jnp.where(low, o[2], o[3])], axis=1)
    yd = yd * _silu(p[:, DZ:DZ + 256])

    x_new = _out_proj(x, mod_ref[2, 0:1, :], ya, yb, yc, yd, w_out_ref)
    xs[b] = x_new

    @pl.when(l == DEPTH - 1)
    def _():
        y_ref[...] = _rms(x_new) * fg_ref[...]


def _ctx_layers(x, mod, norm_g, final_g, w_in, w_out, ws_cat, bias_a, w_conv, sink):
    per_layer = lambda *shape: pl.BlockSpec((None,) + shape, lambda l, b: (l,) + (0,) * len(shape))
    state = lambda w: pl.BlockSpec((None, None, SEQ, w), lambda l, b: (b, l, 0, 0))
    return pl.pallas_call(
        _ctx_kernel,
        grid=(DEPTH, BATCH),
        in_specs=[
            pl.BlockSpec(memory_space=pltpu.SMEM),
            pl.BlockSpec((None, SEQ, D_MODEL), lambda l, b: (jnp.where(l == 0, b, BATCH - 1), 0, 0)),
            per_layer(3, N_GROUPS, D_MODEL),
            per_layer(1, D_MODEL),
            pl.BlockSpec((1, D_MODEL), lambda l, b: (0, 0)),
            pl.BlockSpec((None, D_MODEL, D_IN), lambda l, b: (l, 0, 0), pipeline_mode=pl.Buffered(1)),
            pl.BlockSpec((None, D_MODEL, D_MODEL), lambda l, b: (l, 0, 0), pipeline_mode=pl.Buffered(1)),
            per_layer(CHUNK, A_HEADS * CHUNK),
            per_layer(CHUNK, GROUP_W),
            per_layer(3, GROUP_W),
        ],
        out_specs=[
            pl.BlockSpec((None, SEQ, D_MODEL), lambda l, b: (jnp.where(l == DEPTH - 1, b, 0), 0, 0)),
            state(128), state(128), state(256), state(256),
        ],
        out_shape=[
            jax.ShapeDtypeStruct((BATCH, SEQ, D_MODEL), F32),
            jax.ShapeDtypeStruct((BATCH, DEPTH, SEQ, 128), F32),
            jax.ShapeDtypeStruct((BATCH, DEPTH, SEQ, 128), F32),
            jax.ShapeDtypeStruct((BATCH, DEPTH, SEQ, 256), F32),
            jax.ShapeDtypeStruct((BATCH, DEPTH, SEQ, 256), F32),
        ],
        scratch_shapes=[pltpu.VMEM((BATCH, SEQ, D_MODEL), F32)],
        compiler_params=pltpu.CompilerParams(dimension_semantics=("arbitrary", "arbitrary"),
                                             vmem_limit_bytes=VMEM_LIMIT),
        name="ctx_layers",
    )(sink, x, mod, norm_g, final_g, w_in, w_out, ws_cat, bias_a, w_conv)


def _rope(x, cos, sin_signed, first_half):
    swapped = jnp.where(first_half, pltpu.roll(x, LANES - 16, axis=1), pltpu.roll(x, 16, axis=1))
    return x * cos + swapped * sin_signed


def _lat_proj_kernel(x_ref, mod_ref, g_ref, w_ref, cos_ref, sin_ref,
                     ab_ref, z_ref, qc_ref, kc_ref, vc_ref, qd_ref, kd_ref, vd_ref):
    grp = pl.ds(1 + pl.program_id(0), 1)
    h = _norm_mod(x_ref[...], g_ref[...], mod_ref[0, grp, :], mod_ref[1, grp, :]).astype(BF16)
    for c0 in range(0, N_AB, 896):
        ab_ref[:, c0:c0 + 896] = _dot(h, w_ref[:, c0:c0 + 896])
    z_ref[:, 0:256] = _dot(h, w_ref[:, CZ:CZ + 256])
    z_ref[:, 256:512] = _dot(h, w_ref[:, DZ:DZ + 256])
    cos = cos_ref[...]
    sin = sin_ref[...]
    first_half = (lax.broadcasted_iota(jnp.int32, (TILE, LANES), 1) % 32) < 16
    qc = _dot(h, w_ref[:, CQ:CQ + 256])
    for t in range(2):
        qt = _rope(qc[:, t * LANES:(t + 1) * LANES], cos, sin, first_half) * Q_SCALE
        qc_ref[:, t * LANES:(t + 1) * LANES] = qt.astype(BF16)
    kv = _dot(h, w_ref[:, CKV:CKV + 256])
    k = _rope(kv[:, 0:LANES], cos, sin, first_half)
    v = kv[:, LANES:2 * LANES]
    kc_ref[:, 0:LANES] = k.astype(BF16)
    kc_ref[:, LANES:2 * LANES] = _swap64(k).astype(BF16)
    vc_ref[:, 0:LANES] = v.astype(BF16)
    vc_ref[:, LANES:2 * LANES] = _swap64(v).astype(BF16)
    qd_ref[...] = (_dot(h, w_ref[:, DQ:DQ + 256]) * Q_SCALE).astype(BF16)
    kd_ref[...] = _dot(h, w_ref[:, DK:DK + 256]).astype(BF16)
    vd_ref[...] = _dot(h, w_ref[:, DV:DV + 256]).astype(BF16)


def _lat_proj(layer, x, mod, norm_g, w_in, cos_t, sin_t):
    tile = lambda w: pl.BlockSpec((None, TILE, w), lambda b, i: (b, i, 0))
    per_layer = lambda *shape: pl.BlockSpec((None,) + shape, lambda b, i: (layer,) + (0,) * len(shape))
    sds = lambda w, dt: jax.ShapeDtypeStruct((DEC_BATCH, DEC_SEQ, w), dt)
    return pl.pallas_call(
        _lat_proj_kernel,
        grid=(DEC_BATCH, N_TILES),
        in_specs=[
            tile(D_MODEL),
            per_layer(3, N_GROUPS, D_MODEL),
            per_layer(1, D_MODEL),
            per_layer(D_MODEL, D_IN),
            pl.BlockSpec((TILE, LANES), lambda b, i: (i, 0)),
            pl.BlockSpec((TILE, LANES), lambda b, i: (i, 0)),
        ],
        out_specs=[tile(N_AB), tile(512)] + [tile(256)] * 6,
        out_shape=[sds(N_AB, F32), sds(512, F32)] + [sds(256, BF16)] * 6,
        compiler_params=pltpu.CompilerParams(dimension_semantics=("arbitrary", "arbitrary"),
                                             vmem_limit_bytes=VMEM_LIMIT),
        name="lat_proj",
    )(x, mod, norm_g, w_in, cos_t, sin_t)


def _lat_mix_kernel(layer, sink_ref, x_ref, mod_ref, fg_ref, ab_ref, abp_ref, abn_ref, z_ref,
                    qc_ref, kc_ref, kcp_ref, kcn_ref, vc_ref, vcp_ref, vcn_ref,
                    qd_ref, kd_ref, kdp_ref, kdn_ref, vd_ref, vdp_ref, vdn_ref,
                    cck_ref, ccv_ref, cdk_ref, cdv_ref, cmask_ref, tab_ref,
                    w_out_ref, ws_ref, ba_ref, wc_ref,
                    xo_ref,
                    kce, vce, kde, vde, yc_scr, yd_scr):
    b = pl.program_id(0)
    i = pl.program_id(1)

    ya = _branch_a(ab_ref[:, AU:AU + 256], ab_ref[:, AV:AV + 256], ab_ref[:, AZ:AZ + 256],
                   ws_ref[...], ba_ref[...])

    prev_row = jnp.where(i > 0, abp_ref[7:8, 0:256] * abp_ref[7:8, 256:512], 0.0)
    next_row = jnp.where(i < N_TILES - 1, abn_ref[0:1, 0:256] * abn_ref[0:1, 256:512], 0.0)
    yb = _branch_b(ab_ref[:, BB:BB + 256], ab_ref[:, BC:BC + 256], ab_ref[:, BH:BH + 256],
                   ab_ref[:, BZ:BZ + 256], wc_ref[...], prev_row, next_row)

    kce[0:QBLK, :] = kcp_ref[...]
    kce[QBLK:QBLK + TILE, :] = kc_ref[...]
    kce[QBLK + TILE:, :] = kcn_ref[...]
    for c_src, c_dst in ((0, 0), (LANES, 2 * LANES)):
        vce[0:QBLK, c_dst:c_dst + LANES] = vcp_ref[:, c_src:c_src + LANES]
        vce[QBLK:QBLK + TILE, c_dst:c_dst + LANES] = vc_ref[:, c_src:c_src + LANES]
        vce[QBLK + TILE:, c_dst:c_dst + LANES] = vcn_ref[:, c_src:c_src + LANES]
    vce[:, LANES:2 * LANES] = jnp.ones((TILE + 2 * QBLK, LANES), BF16)
    cck = cck_ref[...]
    ccv = ccv_ref[...]
    ones_ctx = jnp.ones((PAST_LEN, LANES), BF16)
    k_ctx = (cck.astype(BF16), _swap64(cck).astype(BF16))
    v_ctx = jnp.concatenate([ccv.astype(BF16), ones_ctx, _swap64(ccv).astype(BF16)], axis=1)
    low_q = _low_half(QBLK)
    n_blocks = DEC_SEQ // QBLK

    def c_block(j, carry):
        n = i * (TILE // QBLK) + j
        r0 = pl.multiple_of(j * QBLK, QBLK)
        edge = jnp.where(n == 0, 1, jnp.where(n == n_blocks - 1, 2, 0))
        window = cmask_ref[edge]
        s_loc, s_ctx, v_loc, v_c, sinks = [], [], [], [], []
        for t in range(2):
            q = qc_ref[pl.ds(r0, QBLK), t * LANES:(t + 1) * LANES]
            for half in range(2):
                sw = (t + half) % 2
                k_loc = kce[pl.ds(r0, 3 * QBLK), sw * LANES:(sw + 1) * LANES]
                qm = _mask_half(q, low_q, half)
                s_loc.append(_dot_nt(qm, k_loc) + window)
                s_ctx.append(_dot_nt(qm, k_ctx[sw]))
                v_loc.append((vce[pl.ds(r0, 3 * QBLK), sw * LANES:(sw + 2) * LANES], sw))
                v_c.append((v_ctx[:, sw * LANES:(sw + 2) * LANES], sw))
                sinks.append(sink_ref[layer, 2 * t + half])
        o = _attend_heads([s_loc, s_ctx], [v_loc, v_c], sinks)
        yc_scr[pl.ds(r0, QBLK), :] = jnp.concatenate(
            [jnp.where(low_q, o[0], o[1]), jnp.where(low_q, o[2], o[3])], axis=1)
        return carry

    lax.fori_loop(0, TILE // QBLK, c_block, 0)

    halo = 4 * GRID_W
    kde[0:halo, :] = kdp_ref[...]
    kde[halo:halo + TILE, :] = kd_ref[...]
    kde[halo + TILE:, :] = kdn_ref[...]
    for c_src, c_dst in ((0, 0), (LANES, 2 * LANES)):
        vde[0:halo, c_dst:c_dst + LANES] = vdp_ref[:, c_src:c_src + LANES]
        vde[halo:halo + TILE, c_dst:c_dst + LANES] = vd_ref[:, c_src:c_src + LANES]
        vde[halo + TILE:, c_dst:c_dst + LANES] = vdn_ref[:, c_src:c_src + LANES]
    vde[:, LANES:2 * LANES] = jnp.ones((TILE + 2 * halo, LANES), BF16)
    kd_ctx = cdk_ref[...].astype(BF16)
    cdv = cdv_ref[...].astype(BF16)
    vd_ctx = jnp.concatenate([cdv[:, 0:LANES], ones_ctx, cdv[:, LANES:2 * LANES]], axis=1)
    low_r = _low_half(GRID_W)
    n_rows = DEC_SEQ // GRID_W

    def d_row(rl, carry):
        r = i * ROWS_PER_TILE + rl
        rs = jnp.clip(r - WIN_R // 2, 0, n_rows - WIN_R)
        didx = rs - r + (WIN_R - 1)
        off = pl.multiple_of((rs - i * ROWS_PER_TILE + 4) * GRID_W, GRID_W)
        q0 = pl.multiple_of(rl * GRID_W, GRID_W)
        s_loc, s_ctx, v_loc, v_c = [], [], [], []
        for t in range(2):
            q = qd_ref[pl.ds(q0, GRID_W), t * LANES:(t + 1) * LANES]
            k_loc = kde[pl.ds(off, WIN_R * GRID_W), t * LANES:(t + 1) * LANES]
            k_c = kd_ctx[:, t * LANES:(t + 1) * LANES]
            for half in range(2):
                qm = _mask_half(q, low_r, half)
                s_loc.append(_dot_nt(qm, k_loc) + tab_ref[2 * t + half, didx])
                s_ctx.append(_dot_nt(qm, k_c))
                v_loc.append((vde[pl.ds(off, WIN_R * GRID_W), t * LANES:(t + 2) * LANES], t))
                v_c.append((vd_ctx[:, t * LANES:(t + 2) * LANES], t))
        o = _attend_heads([s_loc, s_ctx], [v_loc, v_c])
        yd_scr[pl.ds(q0, GRID_W), :] = jnp.concatenate(
            [jnp.where(low_r, o[0], o[1]), jnp.where(low_r, o[2], o[3])], axis=1)
        return carry

    lax.fori_loop(0, ROWS_PER_TILE, d_row, 0, unroll=2)

    yc = yc_scr[...] * _silu(z_ref[:, 0:256])
    yd = yd_scr[...] * _silu(z_ref[:, 256:512])
    x_new = _out_proj(x_ref[...], mod_ref[2, pl.ds(1 + b, 1), :], ya, yb, yc, yd, w_out_ref)
    if layer == DEPTH - 1:
        x_new = _rms(x_new) * fg_ref[...]
    xo_ref[...] = x_new


def _lat_mix(layer, x, mod, final_g, ab, z, qc, kc, vc, qd, kd, vd, cck, ccv, cdk, cdv, tab,
             w_out, ws_cat, bias_a, w_conv, sink):
    tile = lambda w: pl.BlockSpec((None, TILE, w), lambda b, i: (b, i, 0))
    per_layer = lambda *shape: pl.BlockSpec((None,) + shape, lambda b, i: (layer,) + (0,) * len(shape))

    def halo(rows, w, col=0):
        per = TILE // rows
        last = DEC_SEQ // rows - 1
        prev = pl.BlockSpec((None, rows, w), lambda b, i: (b, jnp.maximum(i * per - 1, 0), col))
        nxt = pl.BlockSpec((None, rows, w), lambda b, i: (b, jnp.minimum((i + 1) * per, last), col))
        return prev, nxt

    abp, abn = halo(8, 512, col=BC // 512)
    cprev, cnext = halo(QBLK, 256)
    dprev, dnext = halo(4 * GRID_W, 256)
    cache = lambda w: pl.BlockSpec((None, None, PAST_LEN, w), lambda b, i: (b, layer, 0, 0))
    return pl.pallas_call(
        functools.partial(_lat_mix_kernel, layer),
        grid=(DEC_BATCH, N_TILES),
        in_specs=[
            pl.BlockSpec(memory_space=pltpu.SMEM),
            tile(D_MODEL),
            per_layer(3, N_GROUPS, D_MODEL),
            pl.BlockSpec((1, D_MODEL), lambda b, i: (0, 0)),
            tile(N_AB), abp, abn, tile(512),
            tile(256), tile(256), cprev, cnext, tile(256), cprev, cnext,
            tile(256), tile(256), dprev, dnext, tile(256), dprev, dnext,
            cache(128), cache(128), cache(256), cache(256),
            pl.BlockSpec((3, QBLK, 3 * QBLK), lambda b, i: (0, 0, 0)),
            per_layer(4, WIN_R, GRID_W, WIN_R * GRID_W),
            per_layer(D_MODEL, D_MODEL),
            per_layer(CHUNK, A_HEADS * CHUNK),
            per_layer(CHUNK, GROUP_W),
            per_layer(3, GROUP_W),
        ],
        out_specs=tile(D_MODEL),
        out_shape=jax.ShapeDtypeStruct((DEC_BATCH, DEC_SEQ, D_MODEL), F32),
        scratch_shapes=[
            pltpu.VMEM((TILE + 2 * QBLK, 2 * LANES), BF16),
            pltpu.VMEM((TILE + 2 * QBLK, 3 * LANES), BF16),
            pltpu.VMEM((TILE + 8 * GRID_W, 2 * LANES), BF16),
            pltpu.VMEM((TILE + 8 * GRID_W, 3 * LANES), BF16),
            pltpu.VMEM((TILE, 256), F32),
            pltpu.VMEM((TILE, 256), F32),
        ],
        compiler_params=pltpu.CompilerParams(dimension_semantics=("arbitrary", "arbitrary"),
                                             vmem_limit_bytes=VMEM_LIMIT),
        name="lat_mix",
    )(sink, x, mod, final_g, ab, ab, ab, z, qc, kc, kc, kc, vc, vc, vc,
      qd, kd, kd, kd, vd, vd, vd, cck, ccv, cdk, cdv, _window_masks(), tab, w_out, ws_cat, bias_a, w_conv)


def _window_masks():
    p = np.arange(QBLK)[:, None]
    j = np.arange(3 * QBLK)[None, :]
    band = np.abs(j - QBLK - p) <= WINDOW
    masks = [band, band & (j >= QBLK), band & (j < 2 * QBLK)]
    return jnp.asarray(np.where(np.stack(masks), 0.0, NEG), F32)


def _rope_tables():
    t = np.arange(DEC_SEQ)
    freqs = (np.float32(ROPE_BASE) ** (-np.arange(16, dtype=np.float32) / np.float32(16))).astype(np.float32)
    ang_r = (t // GRID_W).astype(np.float32)[:, None] * freqs
    ang_c = (t % GRID_W).astype(np.float32)[:, None] * freqs
    cos_h = np.concatenate([np.cos(ang_r), np.cos(ang_r), np.cos(ang_c), np.cos(ang_c)], axis=1)
    sin_h = np.concatenate([-np.sin(ang_r), np.sin(ang_r), -np.sin(ang_c), np.sin(ang_c)], axis=1)
    return (jnp.asarray(np.tile(cos_h, (1, 2)), F32), jnp.asarray(np.tile(sin_h, (1, 2)), F32))


def kernel(x_prompt, x_sample, cache_c_k, cache_c_v, cache_d_k, cache_d_v, c, c_ctx, norm_g, w_mod, b_mod,
           w_in, w_out, w_s, b_s, w_conv, sink, rpb, final_g):
    w_in_b = w_in.astype(BF16)
    w_out_b = w_out.astype(BF16)
    ws_cat = jnp.transpose(w_s, (0, 2, 1, 3)).reshape(DEPTH, CHUNK, A_HEADS * CHUNK).astype(BF16)
    bias_a = jnp.repeat(jnp.transpose(b_s, (0, 2, 1)), HEAD_DIM, axis=2)
    norm_g3 = norm_g.reshape(DEPTH, 1, D_MODEL)
    fg = final_g.reshape(1, D_MODEL)
    cos_t, sin_t = _rope_tables()
    cck = cache_c_k.reshape(DEC_BATCH, DEPTH, PAST_LEN, 128)
    ccv = cache_c_v.reshape(DEC_BATCH, DEPTH, PAST_LEN, 128)
    cdk = cache_d_k.reshape(DEC_BATCH, DEPTH, PAST_LEN, 256)
    cdv = cache_d_v.reshape(DEC_BATCH, DEPTH, PAST_LEN, 256)

    cond = jnp.concatenate([c_ctx[None, :], c, jnp.zeros((N_GROUPS - 1 - DEC_BATCH, D_MODEL), F32)], axis=0)
    mod = _modulation(cond, w_mod, b_mod)
    tab = _nbr_tables(rpb)

    y_prompt, s_ck, s_cv, s_dk, s_dv = _ctx_layers(x_prompt, mod, norm_g3, fg, w_in_b, w_out_b, ws_cat,
                                                   bias_a, w_conv, sink)
    xs = x_sample
    for l in range(DEPTH):
        ab, z, qc, kc, vc, qd, kd, vd = _lat_proj(l, xs, mod, norm_g3, w_in_b, cos_t, sin_t)
        xs = _lat_mix(l, xs, mod, fg, ab, z, qc, kc, vc, qd, kd, vd, cck, ccv, cdk, cdv, tab,
                      w_out_b, ws_cat, bias_a, w_conv, sink)

    shape_c = (BATCH, DEPTH, SEQ, 2, HEAD_DIM)
    shape_d = (BATCH, DEPTH, SEQ, 4, HEAD_DIM)
    return (y_prompt, xs, s_ck.reshape(shape_c), s_cv.reshape(shape_c), s_dk.reshape(shape_d),
            s_dv.reshape(shape_d))
```

```python
import functools

import numpy as np
import jax
import jax.numpy as jnp
from jax import lax
from jax.experimental import pallas as pl
from jax.experimental.pallas import tpu as pltpu

D_MODEL = 1024
BATCH = 16
SEQ = 256
DEPTH = 4
DEC_BATCH = 2
DEC_SEQ = 4096
PAST_LEN = 512
GRID_W = 64
GROUP_W = 256
HEAD_DIM = 64
A_HEADS = 4
CHUNK = 128
WINDOW = 128
QBLK = 128
WIN_R = 8
WIN_C = 16
ROPE_BASE = 10000.0
EPS = 1e-6
NEG = -1e30
D_IN = 3584
LOG2E = 1.4426950408889634
Q_SCALE = HEAD_DIM ** -0.5 * LOG2E

AU, AV, AZ, BB, BC, BH, BZ = 0, 256, 512, 768, 1024, 1280, 1536
CQ, CKV, CZ, DQ, DK, DV, DZ = 1792, 2048, 2304, 2560, 2816, 3072, 3328
N_AB = 1792

LANES = 128
TILE = 512
ROWS_PER_TILE = TILE // GRID_W
N_TILES = DEC_SEQ // TILE
N_GROUPS = 8
VMEM_LIMIT = 56 * 1024 * 1024

F32 = jnp.float32
BF16 = jnp.bfloat16


def _silu(z):
    return z * (1.0 / (1.0 + jnp.exp(-z)))


def _dot(a, b):
    return jnp.dot(a, b, preferred_element_type=F32)


def _dot_nt(a, b):
    return lax.dot_general(a, b, (((1,), (1,)), ((), ())), preferred_element_type=F32)


def _rms(x):
    return x * lax.rsqrt(jnp.mean(x * x, axis=-1, keepdims=True) + EPS)


def _norm_mod(x, g, shift, scale):
    return _rms(x) * g * (1.0 + scale) + shift


def _swap64(x):
    return pltpu.roll(x, HEAD_DIM, axis=1)


def _group_mean_matrix():
    r = lax.broadcasted_iota(jnp.int32, (GROUP_W, GROUP_W), 0) // HEAD_DIM
    c = lax.broadcasted_iota(jnp.int32, (GROUP_W, GROUP_W), 1) // HEAD_DIM
    return jnp.where(r == c, 1.0 / HEAD_DIM, 0.0).astype(BF16)


def _branch_a(au, av, az, ws, bias):
    t = av.shape[0]
    sq = av * av
    hi = sq.astype(BF16)
    lo = (sq - hi.astype(F32)).astype(BF16)
    gm = _group_mean_matrix()
    ms = _dot(hi, gm) + _dot(lo, gm)
    vh = (av * lax.rsqrt(ms + EPS)).astype(BF16)
    head = lax.broadcasted_iota(jnp.int32, (CHUNK, GROUP_W), 1) // HEAD_DIM
    outs = []
    for n in range(t // CHUNK):
        v = vh[n * CHUNK:(n + 1) * CHUNK]
        rhs = jnp.concatenate([jnp.where(head == h, v, jnp.zeros_like(v)) for h in range(A_HEADS)], axis=0)
        outs.append(_dot(ws, rhs) + bias)
    mixed = jnp.concatenate(outs, axis=0)
    return au * mixed * _silu(az)


def _branch_b(bb, bc, bh, bz, wc, prev_row, next_row):
    t = bb.shape[0]
    xc = bc * bh
    row = lax.broadcasted_iota(jnp.int32, xc.shape, 0)
    xm = jnp.where(row == 0, prev_row, pltpu.roll(xc, 1, axis=0))
    xp = jnp.where(row == t - 1, next_row, pltpu.roll(xc, t - 1, axis=0))
    y = wc[0:1, :] * xm + wc[1:2, :] * xc + wc[2:3, :] * xp
    return bb * y * _silu(bz)


def _attend_heads(scores, values, sinks=None):
    return _attend_stacked([jnp.concatenate(seg, axis=0) for seg in scores], values, sinks)


def _attend_stacked(stacked, values, sinks=None):
    n_heads = len(values[0])
    rows = stacked[0].shape[0] // n_heads
    m = functools.reduce(jnp.maximum, [jnp.max(s, axis=-1, keepdims=True) for s in stacked])
    if sinks is not None:
        sink = jnp.concatenate([jnp.full((rows, 1), s * LOG2E, F32) for s in sinks], axis=0)
        m = jnp.maximum(m, sink)
        sink_term = jnp.exp2(sink - m)
    ebs = [jnp.exp2(s - m).astype(BF16) for s in stacked]
    outs = []
    for h in range(n_heads):
        r = slice(h * rows, (h + 1) * rows)
        acc = functools.reduce(jnp.add, [_dot(eb[r], v[h][0]) for eb, v in zip(ebs, values)])
        pos = values[0][h][1]
        o = acc[:, pos * LANES:(pos + 1) * LANES]
        l = acc[:, (1 - pos) * LANES:(2 - pos) * LANES]
        if sinks is not None:
            l = l + sink_term[r]
        outs.append(o * (1.0 / l))
    return outs


def _skewed(n_steps, first_stage, second_stage):
    first_stage(0)
    for k in range(n_steps - 1):
        first_stage(k + 1)
        second_stage(k)
    second_stage(n_steps - 1)


def _low_half(m):
    return lax.broadcasted_iota(jnp.int32, (m, LANES), 1) < HEAD_DIM


def _mask_half(q, low, half):
    z = jnp.zeros_like(q)
    return jnp.where(low, q, z) if half == 0 else jnp.where(low, z, q)


def _out_proj(x, gate, ya, yb, yc, yd, w_out_ref):
    y = _dot(ya.astype(BF16), w_out_ref[0:256, :])
    y += _dot(yb.astype(BF16), w_out_ref[256:512, :])
    y += _dot(yc.astype(BF16), w_out_ref[512:768, :])
    y += _dot(yd.astype(BF16), w_out_ref[768:1024, :])
    return x + gate * y


def _mod_kernel(cond_ref, w_ref, b_ref, o_ref):
    o_ref[...] = jnp.dot(_silu(cond_ref[...]), w_ref[...], preferred_element_type=F32,
                         precision=lax.Precision.HIGHEST) + b_ref[...]


def _modulation(cond, w_mod, b_mod):
    return pl.pallas_call(
        _mod_kernel,
        grid=(DEPTH, 3),
        in_specs=[
            pl.BlockSpec((N_GROUPS, D_MODEL), lambda l, j: (0, 0)),
            pl.BlockSpec((None, D_MODEL, D_MODEL), lambda l, j: (l, 0, j)),
            pl.BlockSpec((None, 1, D_MODEL), lambda l, j: (l, 0, j)),
        ],
        out_specs=pl.BlockSpec((None, None, N_GROUPS, D_MODEL), lambda l, j: (l, j, 0, 0)),
        out_shape=jax.ShapeDtypeStruct((DEPTH, 3, N_GROUPS, D_MODEL), F32),
        compiler_params=pltpu.CompilerParams(dimension_semantics=("arbitrary", "arbitrary")),
        name="modulation",
    )(cond, w_mod, b_mod.reshape(DEPTH, 1, 3 * D_MODEL))


def _nbr_table_kernel(rpb_ref, o_ref):
    c = lax.broadcasted_iota(jnp.int32, (GRID_W, LANES), 0)
    lane = lax.broadcasted_iota(jnp.int32, (GRID_W, LANES), 1)
    cc = lane % GRID_W
    cstart = jnp.clip(c - WIN_C // 2, 0, GRID_W - WIN_C)
    low = lane < GRID_W
    t_low = []
    t_high = []
    for ro in range(2 * WIN_R - 1):
        x = jnp.broadcast_to(rpb_ref[ro:ro + 1, :] * LOG2E, (GRID_W, LANES))
        t_low.append(pltpu.roll(x, LANES - (WIN_C - 1), 1, stride=1, stride_axis=0))
        t_high.append(pltpu.roll(x, GRID_W - (WIN_C - 1), 1, stride=1, stride_axis=0))
    for d in range(WIN_R):
        for p in range(WIN_R // 2):
            v = jnp.where(low, t_low[d + 2 * p], t_high[d + 2 * p + 1])
            v = jnp.where(cc >= cstart, jnp.where(cc < cstart + WIN_C, v, NEG), NEG)
            o_ref[d, :, p * LANES:(p + 1) * LANES] = v


def _nbr_tables(rpb):
    rpb_p = jnp.pad(rpb.reshape(DEPTH * 4, 2 * WIN_R - 1, 2 * WIN_C - 1), ((0, 0), (0, 1), (0, LANES - 31)))
    out = pl.pallas_call(
        _nbr_table_kernel,
        grid=(DEPTH * 4,),
        in_specs=[pl.BlockSpec((None, 2 * WIN_R, LANES), lambda i: (i, 0, 0))],
        out_specs=pl.BlockSpec((None, WIN_R, GRID_W, WIN_R * GRID_W), lambda i: (i, 0, 0, 0)),
        out_shape=jax.ShapeDtypeStruct((DEPTH * 4, WIN_R, GRID_W, WIN_R * GRID_W), F32),
        compiler_params=pltpu.CompilerParams(dimension_semantics=("arbitrary",)),
        name="nbr_tables",
    )(rpb_p)
    return out.reshape(DEPTH, 4, WIN_R, GRID_W, WIN_R * GRID_W)


def _ctx_kernel(sink_ref, x_ref, mod_ref, g_ref, fg_ref, w_in_ref, w_out_ref, ws_ref, ba_ref, wc_ref,
                y_ref, ck_ref, cv_ref, dk_ref, dv_ref, xs):
    l = pl.program_id(0)
    b = pl.program_id(1)

    @pl.when(l == 0)
    def _():
        xs[b] = x_ref[...]

    x = xs[b]
    h = _norm_mod(x, g_ref[...], mod_ref[0, 0:1, :], mod_ref[1, 0:1, :]).astype(BF16)
    p = _dot(h, w_in_ref[...])
    kcf = p[:, CKV:CKV + 128]
    vcf = p[:, CKV + 128:CKV + 256]
    kdf = p[:, DK:DK + 256]
    vdf = p[:, DV:DV + 256]
    ck_ref[...] = kcf
    cv_ref[...] = vcf
    dk_ref[...] = kdf
    dv_ref[...] = vdf

    ya = _branch_a(p[:, AU:AU + 256], p[:, AV:AV + 256], p[:, AZ:AZ + 256], ws_ref[...], ba_ref[...])
    zero_row = jnp.zeros((1, GROUP_W), F32)
    yb = _branch_b(p[:, BB:BB + 256], p[:, BC:BC + 256], p[:, BH:BH + 256], p[:, BZ:BZ + 256],
                   wc_ref[...], zero_row, zero_row)

    low = _low_half(SEQ)

    ones = jnp.ones((SEQ, LANES), BF16)
    kc = (kcf.astype(BF16), _swap64(kcf).astype(BF16))
    vc = ((jnp.concatenate([vcf.astype(BF16), ones], axis=1), 0),
          (jnp.concatenate([ones, _swap64(vcf).astype(BF16)], axis=1), 1))
    scores, values, sinks = [], [], []
    for t in range(2):
        q = (p[:, CQ + t * LANES:CQ + (t + 1) * LANES] * Q_SCALE).astype(BF16)
        for half in range(2):
            sw = (t + half) % 2
            scores.append(_dot_nt(_mask_half(q, low, half), kc[sw]))
            values.append(vc[sw])
            sinks.append(sink_ref[l, 2 * t + half])
    o = [_attend_heads([[sc]], [[va]], [sk])[0] for sc, va, sk in zip(scores, values, sinks)]
    yc = jnp.concatenate([jnp.where(low, o[0], o[1]), jnp.where(low, o[2], o[3])], axis=1)
    yc = yc * _silu(p[:, CZ:CZ + 256])

    scores, values = [], []
    for t in range(2):
        q = (p[:, DQ + t * LANES:DQ + (t + 1) * LANES] * Q_SCALE).astype(BF16)
        k = kdf[:, t * LANES:(t + 1) * LANES].astype(BF16)
        v = (jnp.concatenate([vdf[:, t * LANES:(t + 1) * LANES].astype(BF16), ones], axis=1), 0)
        for half in range(2):
            scores.append(_dot_nt(_mask_half(q, low, half), k))
            values.append(v)
    o = [_attend_heads([[sc]], [[va]])[0] for sc, va in zip(scores, values)]
    yd = jnp.concatenate([jnp.where(low, o[0], o[1]), jnp.where(low, o[2], o[3])], axis=1)
    yd = yd * _silu(p[:, DZ:DZ + 256])

    x_new = _out_proj(x, mod_ref[2, 0:1, :], ya, yb, yc, yd, w_out_ref)
    xs[b] = x_new

    @pl.when(l == DEPTH - 1)
    def _():
        y_ref[...] = _rms(x_new) * fg_ref[...]


def _ctx_layers(x, mod, norm_g, final_g, w_in, w_out, ws_cat, bias_a, w_conv, sink):
    per_layer = lambda *shape: pl.BlockSpec((None,) + shape, lambda l, b: (l,) + (0,) * len(shape))
    state = lambda w: pl.BlockSpec((None, None, SEQ, w), lambda l, b: (b, l, 0, 0))
    return pl.pallas_call(
        _ctx_kernel,
        grid=(DEPTH, BATCH),
        in_specs=[
            pl.BlockSpec(memory_space=pltpu.SMEM),
            pl.BlockSpec((None, SEQ, D_MODEL), lambda l, b: (jnp.where(l == 0, b, BATCH - 1), 0, 0)),
            per_layer(3, N_GROUPS, D_MODEL),
            per_layer(1, D_MODEL),
            pl.BlockSpec((1, D_MODEL), lambda l, b: (0, 0)),
            pl.BlockSpec((None, D_MODEL, D_IN), lambda l, b: (l, 0, 0), pipeline_mode=pl.Buffered(1)),
            pl.BlockSpec((None, D_MODEL, D_MODEL), lambda l, b: (l, 0, 0), pipeline_mode=pl.Buffered(1)),
            per_layer(CHUNK, A_HEADS * CHUNK),
            per_layer(CHUNK, GROUP_W),
            per_layer(3, GROUP_W),
        ],
        out_specs=[
            pl.BlockSpec((None, SEQ, D_MODEL), lambda l, b: (jnp.where(l == DEPTH - 1, b, 0), 0, 0)),
            state(128), state(128), state(256), state(256),
        ],
        out_shape=[
            jax.ShapeDtypeStruct((BATCH, SEQ, D_MODEL), F32),
            jax.ShapeDtypeStruct((BATCH, DEPTH, SEQ, 128), F32),
            jax.ShapeDtypeStruct((BATCH, DEPTH, SEQ, 128), F32),
            jax.ShapeDtypeStruct((BATCH, DEPTH, SEQ, 256), F32),
            jax.ShapeDtypeStruct((BATCH, DEPTH, SEQ, 256), F32),
        ],
        scratch_shapes=[pltpu.VMEM((BATCH, SEQ, D_MODEL), F32)],
        compiler_params=pltpu.CompilerParams(dimension_semantics=("arbitrary", "arbitrary"),
                                             vmem_limit_bytes=VMEM_LIMIT),
        name="ctx_layers",
    )(sink, x, mod, norm_g, final_g, w_in, w_out, ws_cat, bias_a, w_conv)


def _rope(x, cos, sin_signed, first_half):
    swapped = jnp.where(first_half, pltpu.roll(x, LANES - 16, axis=1), pltpu.roll(x, 16, axis=1))
    return x * cos + swapped * sin_signed


def _lat_proj_kernel(x_ref, mod_ref, g_ref, w_ref, cos_ref, sin_ref,
                     ab_ref, z_ref, qc_ref, kc_ref, vc_ref, qd_ref, kd_ref, vd_ref):
    grp = pl.ds(1 + pl.program_id(0), 1)
    h = _norm_mod(x_ref[...], g_ref[...], mod_ref[0, grp, :], mod_ref[1, grp, :]).astype(BF16)
    for c0 in range(0, N_AB, 896):
        ab_ref[:, c0:c0 + 896] = _dot(h, w_ref[:, c0:c0 + 896])
    z_ref[:, 0:256] = _dot(h, w_ref[:, CZ:CZ + 256])
    z_ref[:, 256:512] = _dot(h, w_ref[:, DZ:DZ + 256])
    cos = cos_ref[...]
    sin = sin_ref[...]
    first_half = (lax.broadcasted_iota(jnp.int32, (TILE, LANES), 1) % 32) < 16
    qc = _dot(h, w_ref[:, CQ:CQ + 256])
    for t in range(2):
        qt = _rope(qc[:, t * LANES:(t + 1) * LANES], cos, sin, first_half) * Q_SCALE
        qc_ref[:, t * LANES:(t + 1) * LANES] = qt.astype(BF16)
    kv = _dot(h, w_ref[:, CKV:CKV + 256])
    k = _rope(kv[:, 0:LANES], cos, sin, first_half)
    v = kv[:, LANES:2 * LANES]
    kc_ref[:, 0:LANES] = k.astype(BF16)
    kc_ref[:, LANES:2 * LANES] = _swap64(k).astype(BF16)
    vc_ref[:, 0:LANES] = v.astype(BF16)
    vc_ref[:, LANES:2 * LANES] = _swap64(v).astype(BF16)
    qd_ref[...] = (_dot(h, w_ref[:, DQ:DQ + 256]) * Q_SCALE).astype(BF16)
    kd_ref[...] = _dot(h, w_ref[:, DK:DK + 256]).astype(BF16)
    vd_ref[...] = _dot(h, w_ref[:, DV:DV + 256]).astype(BF16)


def _lat_proj(layer, x, mod, norm_g, w_in, cos_t, sin_t):
    tile = lambda w: pl.BlockSpec((None, TILE, w), lambda b, i: (b, i, 0))
    per_layer = lambda *shape: pl.BlockSpec((None,) + shape, lambda b, i: (layer,) + (0,) * len(shape))
    sds = lambda w, dt: jax.ShapeDtypeStruct((DEC_BATCH, DEC_SEQ, w), dt)
    return pl.pallas_call(
        _lat_proj_kernel,
        grid=(DEC_BATCH, N_TILES),
        in_specs=[
            tile(D_MODEL),
            per_layer(3, N_GROUPS, D_MODEL),
            per_layer(1, D_MODEL),
            per_layer(D_MODEL, D_IN),
            pl.BlockSpec((TILE, LANES), lambda b, i: (i, 0)),
            pl.BlockSpec((TILE, LANES), lambda b, i: (i, 0)),
        ],
        out_specs=[tile(N_AB), tile(512)] + [tile(256)] * 6,
        out_shape=[sds(N_AB, F32), sds(512, F32)] + [sds(256, BF16)] * 6,
        compiler_params=pltpu.CompilerParams(dimension_semantics=("arbitrary", "arbitrary"),
                                             vmem_limit_bytes=VMEM_LIMIT),
        name="lat_proj",
    )(x, mod, norm_g, w_in, cos_t, sin_t)


def _lat_mix_kernel(layer, sink_ref, x_ref, mod_ref, fg_ref, ab_ref, abp_ref, abn_ref, z_ref,
                    qc_ref, kc_ref, kcp_ref, kcn_ref, vc_ref, vcp_ref, vcn_ref,
                    qd_ref, kd_ref, kdp_ref, kdn_ref, vd_ref, vdp_ref, vdn_ref,
                    cck_ref, ccv_ref, cdk_ref, cdv_ref, cmask_ref, tab_ref,
                    w_out_ref, ws_ref, ba_ref, wc_ref,
                    xo_ref,
                    kce, vce, kde, vde, yc_scr, yd_scr, s_ctx, s_c0, s_c1, s_d0, s_d1):
    b = pl.program_id(0)
    i = pl.program_id(1)
    s_c = (s_c0, s_c1)
    s_d = (s_d0, s_d1)

    ya = _branch_a(ab_ref[:, AU:AU + 256], ab_ref[:, AV:AV + 256], ab_ref[:, AZ:AZ + 256],
                   ws_ref[...], ba_ref[...])

    prev_row = jnp.where(i > 0, abp_ref[7:8, 0:256] * abp_ref[7:8, 256:512], 0.0)
    next_row = jnp.where(i < N_TILES - 1, abn_ref[0:1, 0:256] * abn_ref[0:1, 256:512], 0.0)
    yb = _branch_b(ab_ref[:, BB:BB + 256], ab_ref[:, BC:BC + 256], ab_ref[:, BH:BH + 256],
                   ab_ref[:, BZ:BZ + 256], wc_ref[...], prev_row, next_row)

    kce[0:QBLK, :] = kcp_ref[...]
    kce[QBLK:QBLK + TILE, :] = kc_ref[...]
    kce[QBLK + TILE:, :] = kcn_ref[...]
    for c_src, c_dst in ((0, 0), (LANES, 2 * LANES)):
        vce[0:QBLK, c_dst:c_dst + LANES] = vcp_ref[:, c_src:c_src + LANES]
        vce[QBLK:QBLK + TILE, c_dst:c_dst + LANES] = vc_ref[:, c_src:c_src + LANES]
        vce[QBLK + TILE:, c_dst:c_dst + LANES] = vcn_ref[:, c_src:c_src + LANES]
    vce[:, LANES:2 * LANES] = jnp.ones((TILE + 2 * QBLK, LANES), BF16)
    cck = cck_ref[...]
    ccv = ccv_ref[...]
    ones_ctx = jnp.ones((PAST_LEN, LANES), BF16)
    k_ctx = (cck.astype(BF16), _swap64(cck).astype(BF16))
    v_ctx = jnp.concatenate([ccv.astype(BF16), ones_ctx, _swap64(ccv).astype(BF16)], axis=1)
    low_q = _low_half(QBLK)
    low_t = _low_half(TILE)
    n_blocks = DEC_SEQ // QBLK
    blocks_per_tile = TILE // QBLK
    heads = [(t, half) for t in range(2) for half in range(2)]

    for h, (t, half) in enumerate(heads):
        s = _dot_nt(_mask_half(qc_ref[:, t * LANES:(t + 1) * LANES], low_t, half), k_ctx[(t + half) % 2])
        for j in range(blocks_per_tile):
            s_ctx[(4 * j + h) * QBLK:(4 * j + h + 1) * QBLK, :] = s[j * QBLK:(j + 1) * QBLK]

    def c_scores(j):
        n = i * blocks_per_tile + j
        window = cmask_ref[jnp.where(n == 0, 1, jnp.where(n == n_blocks - 1, 2, 0))]
        for h, (t, half) in enumerate(heads):
            sw = (t + half) % 2
            qm = _mask_half(qc_ref[j * QBLK:(j + 1) * QBLK, t * LANES:(t + 1) * LANES], low_q, half)
            k_loc = kce[j * QBLK:(j + 3) * QBLK, sw * LANES:(sw + 1) * LANES]
            s_c[j % 2][h * QBLK:(h + 1) * QBLK, :] = _dot_nt(qm, k_loc) + window

    def c_attend(j):
        v_loc = [(vce[j * QBLK:(j + 3) * QBLK, ((t + half) % 2) * LANES:((t + half) % 2 + 2) * LANES],
                  (t + half) % 2) for t, half in heads]
        v_c = [(v_ctx[:, ((t + half) % 2) * LANES:((t + half) % 2 + 2) * LANES], (t + half) % 2)
               for t, half in heads]
        sinks = [sink_ref[layer, 2 * t + half] for t, half in heads]
        o = _attend_stacked([s_c[j % 2][...], s_ctx[4 * j * QBLK:4 * (j + 1) * QBLK, :]], [v_loc, v_c], sinks)
        yc_scr[j * QBLK:(j + 1) * QBLK, :] = jnp.concatenate(
            [jnp.where(low_q, o[0], o[1]), jnp.where(low_q, o[2], o[3])], axis=1)

    _skewed(blocks_per_tile, c_scores, c_attend)

    halo = 4 * GRID_W
    kde[0:halo, :] = kdp_ref[...]
    kde[halo:halo + TILE, :] = kd_ref[...]
    kde[halo + TILE:, :] = kdn_ref[...]
    for c_src, c_dst in ((0, 0), (LANES, 2 * LANES)):
        vde[0:halo, c_dst:c_dst + LANES] = vdp_ref[:, c_src:c_src + LANES]
        vde[halo:halo + TILE, c_dst:c_dst + LANES] = vd_ref[:, c_src:c_src + LANES]
        vde[halo + TILE:, c_dst:c_dst + LANES] = vdn_ref[:, c_src:c_src + LANES]
    vde[:, LANES:2 * LANES] = jnp.ones((TILE + 2 * halo, LANES), BF16)
    kd_ctx = cdk_ref[...].astype(BF16)
    cdv = cdv_ref[...].astype(BF16)
    vd_ctx = jnp.concatenate([cdv[:, 0:LANES], ones_ctx, cdv[:, LANES:2 * LANES]], axis=1)
    low_r = _low_half(GRID_W)
    n_rows = DEC_SEQ // GRID_W

    for h, (t, half) in enumerate(heads):
        s = _dot_nt(_mask_half(qd_ref[:, t * LANES:(t + 1) * LANES], low_t, half),
                    kd_ctx[:, t * LANES:(t + 1) * LANES])
        for rl in range(ROWS_PER_TILE):
            s_ctx[(4 * rl + h) * GRID_W:(4 * rl + h + 1) * GRID_W, :] = s[rl * GRID_W:(rl + 1) * GRID_W]

    def window_start(rl):
        r = i * ROWS_PER_TILE + rl
        rs = jnp.clip(r - WIN_R // 2, 0, n_rows - WIN_R)
        return rs - r + (WIN_R - 1), pl.multiple_of((rs - i * ROWS_PER_TILE + 4) * GRID_W, GRID_W)

    def d_scores(rl):
        didx, off = window_start(rl)
        for h, (t, half) in enumerate(heads):
            qm = _mask_half(qd_ref[rl * GRID_W:(rl + 1) * GRID_W, t * LANES:(t + 1) * LANES], low_r, half)
            k_loc = kde[pl.ds(off, WIN_R * GRID_W), t * LANES:(t + 1) * LANES]
            s_d[rl % 2][h * GRID_W:(h + 1) * GRID_W, :] = _dot_nt(qm, k_loc) + tab_ref[h, didx]

    def d_attend(rl):
        _, off = window_start(rl)
        v_loc = [(vde[pl.ds(off, WIN_R * GRID_W), t * LANES:(t + 2) * LANES], t) for t, _ in heads]
        v_c = [(vd_ctx[:, t * LANES:(t + 2) * LANES], t) for t, _ in heads]
        o = _attend_stacked([s_d[rl % 2][...], s_ctx[4 * rl * GRID_W:4 * (rl + 1) * GRID_W, :]], [v_loc, v_c])
        yd_scr[rl * GRID_W:(rl + 1) * GRID_W, :] = jnp.concatenate(
            [jnp.where(low_r, o[0], o[1]), jnp.where(low_r, o[2], o[3])], axis=1)

    _skewed(ROWS_PER_TILE, d_scores, d_attend)

    yc = yc_scr[...] * _silu(z_ref[:, 0:256])
    yd = yd_scr[...] * _silu(z_ref[:, 256:512])
    x_new = _out_proj(x_ref[...], mod_ref[2, pl.ds(1 + b, 1), :], ya, yb, yc, yd, w_out_ref)
    if layer == DEPTH - 1:
        x_new = _rms(x_new) * fg_ref[...]
    xo_ref[...] = x_new


def _lat_mix(layer, x, mod, final_g, ab, z, qc, kc, vc, qd, kd, vd, cck, ccv, cdk, cdv, tab,
             w_out, ws_cat, bias_a, w_conv, sink):
    tile = lambda w: pl.BlockSpec((None, TILE, w), lambda b, i: (b, i, 0))
    per_layer = lambda *shape: pl.BlockSpec((None,) + shape, lambda b, i: (layer,) + (0,) * len(shape))

    def halo(rows, w, col=0):
        per = TILE // rows
        last = DEC_SEQ // rows - 1
        prev = pl.BlockSpec((None, rows, w), lambda b, i: (b, jnp.maximum(i * per - 1, 0), col))
        nxt = pl.BlockSpec((None, rows, w), lambda b, i: (b, jnp.minimum((i + 1) * per, last), col))
        return prev, nxt

    abp, abn = halo(8, 512, col=BC // 512)
    cprev, cnext = halo(QBLK, 256)
    dprev, dnext = halo(4 * GRID_W, 256)
    cache = lambda w: pl.BlockSpec((None, None, PAST_LEN, w), lambda b, i: (b, layer, 0, 0))
    return pl.pallas_call(
        functools.partial(_lat_mix_kernel, layer),
        grid=(DEC_BATCH, N_TILES),
        in_specs=[
            pl.BlockSpec(memory_space=pltpu.SMEM),
            tile(D_MODEL),
            per_layer(3, N_GROUPS, D_MODEL),
            pl.BlockSpec((1, D_MODEL), lambda b, i: (0, 0)),
            tile(N_AB), abp, abn, tile(512),
            tile(256), tile(256), cprev, cnext, tile(256), cprev, cnext,
            tile(256), tile(256), dprev, dnext, tile(256), dprev, dnext,
            cache(128), cache(128), cache(256), cache(256),
            pl.BlockSpec((3, QBLK, 3 * QBLK), lambda b, i: (0, 0, 0)),
            per_layer(4, WIN_R, GRID_W, WIN_R * GRID_W),
            per_layer(D_MODEL, D_MODEL),
            per_layer(CHUNK, A_HEADS * CHUNK),
            per_layer(CHUNK, GROUP_W),
            per_layer(3, GROUP_W),
        ],
        out_specs=tile(D_MODEL),
        out_shape=jax.ShapeDtypeStruct((DEC_BATCH, DEC_SEQ, D_MODEL), F32),
        scratch_shapes=[
            pltpu.VMEM((TILE + 2 * QBLK, 2 * LANES), BF16),
            pltpu.VMEM((TILE + 2 * QBLK, 3 * LANES), BF16),
            pltpu.VMEM((TILE + 8 * GRID_W, 2 * LANES), BF16),
            pltpu.VMEM((TILE + 8 * GRID_W, 3 * LANES), BF16),
            pltpu.VMEM((TILE, 256), F32),
            pltpu.VMEM((TILE, 256), F32),
            pltpu.VMEM((4 * TILE, PAST_LEN), F32),
            pltpu.VMEM((4 * QBLK, 3 * QBLK), F32),
            pltpu.VMEM((4 * QBLK, 3 * QBLK), F32),
            pltpu.VMEM((4 * GRID_W, WIN_R * GRID_W), F32),
            pltpu.VMEM((4 * GRID_W, WIN_R * GRID_W), F32),
        ],
        compiler_params=pltpu.CompilerParams(dimension_semantics=("arbitrary", "arbitrary"),
                                             vmem_limit_bytes=VMEM_LIMIT),
        name="lat_mix",
    )(sink, x, mod, final_g, ab, ab, ab, z, qc, kc, kc, kc, vc, vc, vc,
      qd, kd, kd, kd, vd, vd, vd, cck, ccv, cdk, cdv, _window_masks(), tab, w_out, ws_cat, bias_a, w_conv)


def _window_masks():
    p = np.arange(QBLK)[:, None]
    j = np.arange(3 * QBLK)[None, :]
    band = np.abs(j - QBLK - p) <= WINDOW
    masks = [band, band & (j >= QBLK), band & (j < 2 * QBLK)]
    return jnp.asarray(np.where(np.stack(masks), 0.0, NEG), F32)


def _rope_tables():
    t = np.arange(DEC_SEQ)
    freqs = (np.float32(ROPE_BASE) ** (-np.arange(16, dtype=np.float32) / np.float32(16))).astype(np.float32)
    ang_r = (t // GRID_W).astype(np.float32)[:, None] * freqs
    ang_c = (t % GRID_W).astype(np.float32)[:, None] * freqs
    cos_h = np.concatenate([np.cos(ang_r), np.cos(ang_r), np.cos(ang_c), np.cos(ang_c)], axis=1)
    sin_h = np.concatenate([-np.sin(ang_r), np.sin(ang_r), -np.sin(ang_c), np.sin(ang_c)], axis=1)
    return (jnp.asarray(np.tile(cos_h, (1, 2)), F32), jnp.asarray(np.tile(sin_h, (1, 2)), F32))


def kernel(x_prompt, x_sample, cache_c_k, cache_c_v, cache_d_k, cache_d_v, c, c_ctx, norm_g, w_mod, b_mod,
           w_in, w_out, w_s, b_s, w_conv, sink, rpb, final_g):
    w_in_b = w_in.astype(BF16)
    w_out_b = w_out.astype(BF16)
    ws_cat = jnp.transpose(w_s, (0, 2, 1, 3)).reshape(DEPTH, CHUNK, A_HEADS * CHUNK).astype(BF16)
    bias_a = jnp.repeat(jnp.transpose(b_s, (0, 2, 1)), HEAD_DIM, axis=2)
    norm_g3 = norm_g.reshape(DEPTH, 1, D_MODEL)
    fg = final_g.reshape(1, D_MODEL)
    cos_t, sin_t = _rope_tables()
    cck = cache_c_k.reshape(DEC_BATCH, DEPTH, PAST_LEN, 128)
    ccv = cache_c_v.reshape(DEC_BATCH, DEPTH, PAST_LEN, 128)
    cdk = cache_d_k.reshape(DEC_BATCH, DEPTH, PAST_LEN, 256)
    cdv = cache_d_v.reshape(DEC_BATCH, DEPTH, PAST_LEN, 256)

    cond = jnp.concatenate([c_ctx[None, :], c, jnp.zeros((N_GROUPS - 1 - DEC_BATCH, D_MODEL), F32)], axis=0)
    mod = _modulation(cond, w_mod, b_mod)
    tab = _nbr_tables(rpb)

    y_prompt, s_ck, s_cv, s_dk, s_dv = _ctx_layers(x_prompt, mod, norm_g3, fg, w_in_b, w_out_b, ws_cat,
                                                   bias_a, w_conv, sink)
    xs = x_sample
    for l in range(DEPTH):
        ab, z, qc, kc, vc, qd, kd, vd = _lat_proj(l, xs, mod, norm_g3, w_in_b, cos_t, sin_t)
        xs = _lat_mix(l, xs, mod, fg, ab, z, qc, kc, vc, qd, kd, vd, cck, ccv, cdk, cdv, tab,
                      w_out_b, ws_cat, bias_a, w_conv, sink)

    shape_c = (BATCH, DEPTH, SEQ, 2, HEAD_DIM)
    shape_d = (BATCH, DEPTH, SEQ, 4, HEAD_DIM)
    return (y_prompt, xs, s_ck.reshape(shape_c), s_cv.reshape(shape_c), s_dk.reshape(shape_d),
            s_dv.reshape(shape_d))
```

```python
import functools

import numpy as np
import jax
import jax.numpy as jnp
from jax import lax
from jax.experimental import pallas as pl
from jax.experimental.pallas import tpu as pltpu

D_MODEL = 1024
BATCH = 16
SEQ = 256
DEPTH = 4
DEC_BATCH = 2
DEC_SEQ = 4096
PAST_LEN = 512
GRID_W = 64
GROUP_W = 256
HEAD_DIM = 64
A_HEADS = 4
CHUNK = 128
WINDOW = 128
QBLK = 128
WIN_R = 8
WIN_C = 16
ROPE_BASE = 10000.0
EPS = 1e-6
NEG = -1e30
D_IN = 3584
LOG2E = 1.4426950408889634
Q_SCALE = HEAD_DIM ** -0.5 * LOG2E

AU, AV, AZ, BB, BC, BH, BZ = 0, 256, 512, 768, 1024, 1280, 1536
CQ, CKV, CZ, DQ, DK, DV, DZ = 1792, 2048, 2304, 2560, 2816, 3072, 3328
N_AB = 1792

LANES = 128
TILE = 512
ROWS_PER_TILE = TILE // GRID_W
N_TILES = DEC_SEQ // TILE
N_GROUPS = 8
VMEM_LIMIT = 56 * 1024 * 1024

F32 = jnp.float32
BF16 = jnp.bfloat16


def _silu(z):
    return z * (1.0 / (1.0 + jnp.exp(-z)))


def _dot(a, b):
    return jnp.dot(a, b, preferred_element_type=F32)


def _dot_nt(a, b):
    return lax.dot_general(a, b, (((1,), (1,)), ((), ())), preferred_element_type=F32)


def _rms(x):
    return x * lax.rsqrt(jnp.mean(x * x, axis=-1, keepdims=True) + EPS)


def _norm_mod(x, g, shift, scale):
    return _rms(x) * g * (1.0 + scale) + shift


def _swap64(x):
    return pltpu.roll(x, HEAD_DIM, axis=1)


def _group_mean_matrix():
    r = lax.broadcasted_iota(jnp.int32, (GROUP_W, GROUP_W), 0) // HEAD_DIM
    c = lax.broadcasted_iota(jnp.int32, (GROUP_W, GROUP_W), 1) // HEAD_DIM
    return jnp.where(r == c, 1.0 / HEAD_DIM, 0.0).astype(BF16)


def _branch_a(au, av, az, ws, bias):
    t = av.shape[0]
    sq = av * av
    hi = sq.astype(BF16)
    lo = (sq - hi.astype(F32)).astype(BF16)
    gm = _group_mean_matrix()
    ms = _dot(hi, gm) + _dot(lo, gm)
    vh = (av * lax.rsqrt(ms + EPS)).astype(BF16)
    head = lax.broadcasted_iota(jnp.int32, (CHUNK, GROUP_W), 1) // HEAD_DIM
    outs = []
    for n in range(t // CHUNK):
        v = vh[n * CHUNK:(n + 1) * CHUNK]
        rhs = jnp.concatenate([jnp.where(head == h, v, jnp.zeros_like(v)) for h in range(A_HEADS)], axis=0)
        outs.append(_dot(ws, rhs) + bias)
    mixed = jnp.concatenate(outs, axis=0)
    return au * mixed * _silu(az)


def _branch_b(bb, bc, bh, bz, wc, prev_row, next_row):
    t = bb.shape[0]
    xc = bc * bh
    row = lax.broadcasted_iota(jnp.int32, xc.shape, 0)
    xm = jnp.where(row == 0, prev_row, pltpu.roll(xc, 1, axis=0))
    xp = jnp.where(row == t - 1, next_row, pltpu.roll(xc, t - 1, axis=0))
    y = wc[0:1, :] * xm + wc[1:2, :] * xc + wc[2:3, :] * xp
    return bb * y * _silu(bz)


def _attend(score, value, pos, sink=None):
    m = jnp.max(score, axis=-1, keepdims=True)
    if sink is not None:
        m = jnp.maximum(m, sink * LOG2E)
    acc = _dot(jnp.exp2(score - m).astype(BF16), value)
    o = acc[:, pos * LANES:(pos + 1) * LANES]
    l = acc[:, (1 - pos) * LANES:(2 - pos) * LANES]
    if sink is not None:
        l = l + jnp.exp2(sink * LOG2E - m)
    return o * (1.0 / l)


def _context_pass(q_stacked, k, value):
    s = _dot_nt(q_stacked, k)
    m = jnp.max(s, axis=-1, keepdims=True)
    return m, _dot(jnp.exp2(s - m).astype(BF16), value)


def _local_pass(s_loc, values, m_ctx, acc_ctx, sink=None):
    m = jnp.maximum(jnp.max(s_loc, axis=-1, keepdims=True), m_ctx)
    if sink is not None:
        m = jnp.maximum(m, sink)
    e = jnp.exp2(s_loc - m).astype(BF16)
    ctx_scale = jnp.exp2(m_ctx - m)
    rows = s_loc.shape[0] // len(values)
    outs = []
    for p, (v, pos) in enumerate(values):
        r = slice(p * rows, (p + 1) * rows)
        acc = _dot(e[r], v) + acc_ctx[r] * ctx_scale[r]
        o = acc[:, pos * LANES:(pos + 1) * LANES]
        l = acc[:, (1 - pos) * LANES:(2 - pos) * LANES]
        if sink is not None:
            l = l + jnp.exp2(sink[r] - m[r])
        outs.append(o * (1.0 / l))
    return outs


def _skewed(n_steps, first_stage, second_stage):
    first_stage(0)
    for k in range(n_steps - 1):
        first_stage(k + 1)
        second_stage(k)
    second_stage(n_steps - 1)


def _low_half(m):
    return lax.broadcasted_iota(jnp.int32, (m, LANES), 1) < HEAD_DIM


def _mask_half(q, low, half):
    z = jnp.zeros_like(q)
    return jnp.where(low, q, z) if half == 0 else jnp.where(low, z, q)


def _out_proj(x, gate, ya, yb, yc, yd, w_out_ref):
    y = _dot(ya.astype(BF16), w_out_ref[0:256, :])
    y += _dot(yb.astype(BF16), w_out_ref[256:512, :])
    y += _dot(yc.astype(BF16), w_out_ref[512:768, :])
    y += _dot(yd.astype(BF16), w_out_ref[768:1024, :])
    return x + gate * y


def _mod_kernel(cond_ref, w_ref, b_ref, o_ref):
    o_ref[...] = jnp.dot(_silu(cond_ref[...]), w_ref[...], preferred_element_type=F32,
                         precision=lax.Precision.HIGHEST) + b_ref[...]


def _modulation(cond, w_mod, b_mod):
    return pl.pallas_call(
        _mod_kernel,
        grid=(DEPTH, 3),
        in_specs=[
            pl.BlockSpec((N_GROUPS, D_MODEL), lambda l, j: (0, 0)),
            pl.BlockSpec((None, D_MODEL, D_MODEL), lambda l, j: (l, 0, j)),
            pl.BlockSpec((None, 1, D_MODEL), lambda l, j: (l, 0, j)),
        ],
        out_specs=pl.BlockSpec((None, None, N_GROUPS, D_MODEL), lambda l, j: (l, j, 0, 0)),
        out_shape=jax.ShapeDtypeStruct((DEPTH, 3, N_GROUPS, D_MODEL), F32),
        compiler_params=pltpu.CompilerParams(dimension_semantics=("arbitrary", "arbitrary")),
        name="modulation",
    )(cond, w_mod, b_mod.reshape(DEPTH, 1, 3 * D_MODEL))


def _nbr_table_kernel(rpb_ref, o_ref):
    c = lax.broadcasted_iota(jnp.int32, (GRID_W, LANES), 0)
    lane = lax.broadcasted_iota(jnp.int32, (GRID_W, LANES), 1)
    cc = lane % GRID_W
    cstart = jnp.clip(c - WIN_C // 2, 0, GRID_W - WIN_C)
    low = lane < GRID_W
    t_low = []
    t_high = []
    for ro in range(2 * WIN_R - 1):
        x = jnp.broadcast_to(rpb_ref[ro:ro + 1, :] * LOG2E, (GRID_W, LANES))
        t_low.append(pltpu.roll(x, LANES - (WIN_C - 1), 1, stride=1, stride_axis=0))
        t_high.append(pltpu.roll(x, GRID_W - (WIN_C - 1), 1, stride=1, stride_axis=0))
    for d in range(WIN_R):
        for p in range(WIN_R // 2):
            v = jnp.where(low, t_low[d + 2 * p], t_high[d + 2 * p + 1])
            v = jnp.where(cc >= cstart, jnp.where(cc < cstart + WIN_C, v, NEG), NEG)
            o_ref[d, :, p * LANES:(p + 1) * LANES] = v


def _nbr_tables(rpb):
    rpb_p = jnp.pad(rpb.reshape(DEPTH * 4, 2 * WIN_R - 1, 2 * WIN_C - 1), ((0, 0), (0, 1), (0, LANES - 31)))
    out = pl.pallas_call(
        _nbr_table_kernel,
        grid=(DEPTH * 4,),
        in_specs=[pl.BlockSpec((None, 2 * WIN_R, LANES), lambda i: (i, 0, 0))],
        out_specs=pl.BlockSpec((None, WIN_R, GRID_W, WIN_R * GRID_W), lambda i: (i, 0, 0, 0)),
        out_shape=jax.ShapeDtypeStruct((DEPTH * 4, WIN_R, GRID_W, WIN_R * GRID_W), F32),
        compiler_params=pltpu.CompilerParams(dimension_semantics=("arbitrary",)),
        name="nbr_tables",
    )(rpb_p)
    return out.reshape(DEPTH, 4, WIN_R, GRID_W, WIN_R * GRID_W)


def _ctx_kernel(sink_ref, x_ref, mod_ref, g_ref, fg_ref, w_in_ref, w_out_ref, ws_ref, ba_ref, wc_ref,
                y_ref, ck_ref, cv_ref, dk_ref, dv_ref, xs):
    l = pl.program_id(0)
    b = pl.program_id(1)

    @pl.when(l == 0)
    def _():
        xs[b] = x_ref[...]

    x = xs[b]
    h = _norm_mod(x, g_ref[...], mod_ref[0, 0:1, :], mod_ref[1, 0:1, :]).astype(BF16)
    p = _dot(h, w_in_ref[...])
    kcf = p[:, CKV:CKV + 128]
    vcf = p[:, CKV + 128:CKV + 256]
    kdf = p[:, DK:DK + 256]
    vdf = p[:, DV:DV + 256]
    ck_ref[...] = kcf
    cv_ref[...] = vcf
    dk_ref[...] = kdf
    dv_ref[...] = vdf

    ya = _branch_a(p[:, AU:AU + 256], p[:, AV:AV + 256], p[:, AZ:AZ + 256], ws_ref[...], ba_ref[...])
    zero_row = jnp.zeros((1, GROUP_W), F32)
    yb = _branch_b(p[:, BB:BB + 256], p[:, BC:BC + 256], p[:, BH:BH + 256], p[:, BZ:BZ + 256],
                   wc_ref[...], zero_row, zero_row)

    low = _low_half(SEQ)

    ones = jnp.ones((SEQ, LANES), BF16)
    kc = (kcf.astype(BF16), _swap64(kcf).astype(BF16))
    vc = (jnp.concatenate([vcf.astype(BF16), ones], axis=1),
          jnp.concatenate([ones, _swap64(vcf).astype(BF16)], axis=1))
    o = []
    for t in range(2):
        q = (p[:, CQ + t * LANES:CQ + (t + 1) * LANES] * Q_SCALE).astype(BF16)
        for half in range(2):
            sw = (t + half) % 2
            o.append(_attend(_dot_nt(_mask_half(q, low, half), kc[sw]), vc[sw], sw, sink_ref[l, 2 * t + half]))
    yc = jnp.concatenate([jnp.where(low, o[0], o[1]), jnp.where(low, o[2], o[3])], axis=1)
    yc = yc * _silu(p[:, CZ:CZ + 256])

    o = []
    for t in range(2):
        q = (p[:, DQ + t * LANES:DQ + (t + 1) * LANES] * Q_SCALE).astype(BF16)
        k = kdf[:, t * LANES:(t + 1) * LANES].astype(BF16)
        v = jnp.concatenate([vdf[:, t * LANES:(t + 1) * LANES].astype(BF16), ones], axis=1)
        for half in range(2):
            o.append(_attend(_dot_nt(_mask_half(q, low, half), k), v, 0))
    yd = jnp.concatenate([jnp.where(low, o[0], o[1]), jnp.where(low, o[2], o[3])], axis=1)
    yd = yd * _silu(p[:, DZ:DZ + 256])

    x_new = _out_proj(x, mod_ref[2, 0:1, :], ya, yb, yc, yd, w_out_ref)
    xs[b] = x_new

    @pl.when(l == DEPTH - 1)
    def _():
        y_ref[...] = _rms(x_new) * fg_ref[...]


def _ctx_layers(x, mod, norm_g, final_g, w_in, w_out, ws_cat, bias_a, w_conv, sink):
    per_layer = lambda *shape: pl.BlockSpec((None,) + shape, lambda l, b: (l,) + (0,) * len(shape))
    state = lambda w: pl.BlockSpec((None, None, SEQ, w), lambda l, b: (b, l, 0, 0))
    return pl.pallas_call(
        _ctx_kernel,
        grid=(DEPTH, BATCH),
        in_specs=[
            pl.BlockSpec(memory_space=pltpu.SMEM),
            pl.BlockSpec((None, SEQ, D_MODEL), lambda l, b: (jnp.where(l == 0, b, BATCH - 1), 0, 0)),
            per_layer(3, N_GROUPS, D_MODEL),
            per_layer(1, D_MODEL),
            pl.BlockSpec((1, D_MODEL), lambda l, b: (0, 0)),
            pl.BlockSpec((None, D_MODEL, D_IN), lambda l, b: (l, 0, 0), pipeline_mode=pl.Buffered(1)),
            pl.BlockSpec((None, D_MODEL, D_MODEL), lambda l, b: (l, 0, 0), pipeline_mode=pl.Buffered(1)),
            per_layer(CHUNK, A_HEADS * CHUNK),
            per_layer(CHUNK, GROUP_W),
            per_layer(3, GROUP_W),
        ],
        out_specs=[
            pl.BlockSpec((None, SEQ, D_MODEL), lambda l, b: (jnp.where(l == DEPTH - 1, b, 0), 0, 0)),
            state(128), state(128), state(256), state(256),
        ],
        out_shape=[
            jax.ShapeDtypeStruct((BATCH, SEQ, D_MODEL), F32),
            jax.ShapeDtypeStruct((BATCH, DEPTH, SEQ, 128), F32),
            jax.ShapeDtypeStruct((BATCH, DEPTH, SEQ, 128), F32),
            jax.ShapeDtypeStruct((BATCH, DEPTH, SEQ, 256), F32),
            jax.ShapeDtypeStruct((BATCH, DEPTH, SEQ, 256), F32),
        ],
        scratch_shapes=[pltpu.VMEM((BATCH, SEQ, D_MODEL), F32)],
        compiler_params=pltpu.CompilerParams(dimension_semantics=("arbitrary", "arbitrary"),
                                             vmem_limit_bytes=VMEM_LIMIT),
        name="ctx_layers",
    )(sink, x, mod, norm_g, final_g, w_in, w_out, ws_cat, bias_a, w_conv)


def _rope(x, cos, sin_signed, first_half):
    swapped = jnp.where(first_half, pltpu.roll(x, LANES - 16, axis=1), pltpu.roll(x, 16, axis=1))
    return x * cos + swapped * sin_signed


def _lat_proj_kernel(x_ref, mod_ref, g_ref, w_ref, cos_ref, sin_ref,
                     ab_ref, z_ref, qc_ref, kc_ref, vc_ref, qd_ref, kd_ref, vd_ref):
    grp = pl.ds(1 + pl.program_id(0), 1)
    h = _norm_mod(x_ref[...], g_ref[...], mod_ref[0, grp, :], mod_ref[1, grp, :]).astype(BF16)
    for c0 in range(0, N_AB, 896):
        ab_ref[:, c0:c0 + 896] = _dot(h, w_ref[:, c0:c0 + 896])
    z_ref[:, 0:256] = _dot(h, w_ref[:, CZ:CZ + 256])
    z_ref[:, 256:512] = _dot(h, w_ref[:, DZ:DZ + 256])
    cos = cos_ref[...]
    sin = sin_ref[...]
    first_half = (lax.broadcasted_iota(jnp.int32, (TILE, LANES), 1) % 32) < 16
    qc = _dot(h, w_ref[:, CQ:CQ + 256])
    for t in range(2):
        qt = _rope(qc[:, t * LANES:(t + 1) * LANES], cos, sin, first_half) * Q_SCALE
        qc_ref[:, t * LANES:(t + 1) * LANES] = qt.astype(BF16)
    kv = _dot(h, w_ref[:, CKV:CKV + 256])
    k = _rope(kv[:, 0:LANES], cos, sin, first_half)
    v = kv[:, LANES:2 * LANES]
    kc_ref[:, 0:LANES] = k.astype(BF16)
    kc_ref[:, LANES:2 * LANES] = _swap64(k).astype(BF16)
    vc_ref[:, 0:LANES] = v.astype(BF16)
    vc_ref[:, LANES:2 * LANES] = _swap64(v).astype(BF16)
    qd_ref[...] = (_dot(h, w_ref[:, DQ:DQ + 256]) * Q_SCALE).astype(BF16)
    kd_ref[...] = _dot(h, w_ref[:, DK:DK + 256]).astype(BF16)
    vd_ref[...] = _dot(h, w_ref[:, DV:DV + 256]).astype(BF16)


def _lat_proj(layer, x, mod, norm_g, w_in, cos_t, sin_t):
    tile = lambda w: pl.BlockSpec((None, TILE, w), lambda b, i: (b, i, 0))
    per_layer = lambda *shape: pl.BlockSpec((None,) + shape, lambda b, i: (layer,) + (0,) * len(shape))
    sds = lambda w, dt: jax.ShapeDtypeStruct((DEC_BATCH, DEC_SEQ, w), dt)
    return pl.pallas_call(
        _lat_proj_kernel,
        grid=(DEC_BATCH, N_TILES),
        in_specs=[
            tile(D_MODEL),
            per_layer(3, N_GROUPS, D_MODEL),
            per_layer(1, D_MODEL),
            per_layer(D_MODEL, D_IN),
            pl.BlockSpec((TILE, LANES), lambda b, i: (i, 0)),
            pl.BlockSpec((TILE, LANES), lambda b, i: (i, 0)),
        ],
        out_specs=[tile(N_AB), tile(512)] + [tile(256)] * 6,
        out_shape=[sds(N_AB, F32), sds(512, F32)] + [sds(256, BF16)] * 6,
        compiler_params=pltpu.CompilerParams(dimension_semantics=("arbitrary", "arbitrary"),
                                             vmem_limit_bytes=VMEM_LIMIT),
        name="lat_proj",
    )(x, mod, norm_g, w_in, cos_t, sin_t)


_C_GROUPS = (((0, 0), (1, 1)), ((0, 1), (1, 0)))
_D_GROUPS = (((0, 0), (0, 1)), ((1, 0), (1, 1)))


def _stacked_queries(q_ref, rows, low, groups):
    return [jnp.concatenate([_mask_half(q_ref[rows, t * LANES:(t + 1) * LANES], low, half)
                             for t, half in group], axis=0) for group in groups]


def _lat_mix_kernel(layer, sink_ref, x_ref, mod_ref, fg_ref, ab_ref, abp_ref, abn_ref, z_ref,
                    qc_ref, kc_ref, kcp_ref, kcn_ref, vc_ref, vcp_ref, vcn_ref,
                    qd_ref, kd_ref, kdp_ref, kdn_ref, vd_ref, vdp_ref, vdn_ref,
                    cck_ref, ccv_ref, cdk_ref, cdv_ref, cmask_ref, tab_ref,
                    w_out_ref, ws_ref, ba_ref, wc_ref,
                    xo_ref,
                    kce, vce, kde, vde, yc_scr, yd_scr, m_ctx, acc_ctx, s_c0, s_c1, s_d0, s_d1):
    b = pl.program_id(0)
    i = pl.program_id(1)
    s_c = (s_c0, s_c1)
    s_d = (s_d0, s_d1)

    ya = _branch_a(ab_ref[:, AU:AU + 256], ab_ref[:, AV:AV + 256], ab_ref[:, AZ:AZ + 256],
                   ws_ref[...], ba_ref[...])

    prev_row = jnp.where(i > 0, abp_ref[7:8, 0:256] * abp_ref[7:8, 256:512], 0.0)
    next_row = jnp.where(i < N_TILES - 1, abn_ref[0:1, 0:256] * abn_ref[0:1, 256:512], 0.0)
    yb = _branch_b(ab_ref[:, BB:BB + 256], ab_ref[:, BC:BC + 256], ab_ref[:, BH:BH + 256],
                   ab_ref[:, BZ:BZ + 256], wc_ref[...], prev_row, next_row)

    def file_context(m, acc, group, step_rows, n_steps):
        for which in range(2):
            for k in range(n_steps):
                src = slice(which * TILE + k * step_rows, which * TILE + (k + 1) * step_rows)
                dst = slice((4 * k + 2 * group + which) * step_rows, (4 * k + 2 * group + which + 1) * step_rows)
                m_ctx[dst, :] = m[src]
                acc_ctx[dst, :] = acc[src]

    kce[0:QBLK, :] = kcp_ref[...]
    kce[QBLK:QBLK + TILE, :] = kc_ref[...]
    kce[QBLK + TILE:, :] = kcn_ref[...]
    for c_src, c_dst in ((0, 0), (LANES, 2 * LANES)):
        vce[0:QBLK, c_dst:c_dst + LANES] = vcp_ref[:, c_src:c_src + LANES]
        vce[QBLK:QBLK + TILE, c_dst:c_dst + LANES] = vc_ref[:, c_src:c_src + LANES]
        vce[QBLK + TILE:, c_dst:c_dst + LANES] = vcn_ref[:, c_src:c_src + LANES]
    vce[:, LANES:2 * LANES] = jnp.ones((TILE + 2 * QBLK, LANES), BF16)
    cck = cck_ref[...]
    ccv = ccv_ref[...]
    ones_ctx = jnp.ones((PAST_LEN, LANES), BF16)
    k_ctx = (cck.astype(BF16), _swap64(cck).astype(BF16))
    v_ctx = jnp.concatenate([ccv.astype(BF16), ones_ctx, _swap64(ccv).astype(BF16)], axis=1)
    low_q = _low_half(QBLK)
    low_t = _low_half(TILE)
    n_blocks = DEC_SEQ // QBLK
    blocks_per_tile = TILE // QBLK

    for sw, q in enumerate(_stacked_queries(qc_ref, slice(None), low_t, _C_GROUPS)):
        m, acc = _context_pass(q, k_ctx[sw], v_ctx[:, sw * LANES:(sw + 2) * LANES])
        file_context(m, acc, sw, QBLK, blocks_per_tile)

    def c_scores(j):
        n = i * blocks_per_tile + j
        window = cmask_ref[jnp.where(n == 0, 1, jnp.where(n == n_blocks - 1, 2, 0))]
        window = jnp.concatenate([window, window], axis=0)
        rows = slice(j * QBLK, (j + 1) * QBLK)
        for sw, q in enumerate(_stacked_queries(qc_ref, rows, low_q, _C_GROUPS)):
            k_loc = kce[j * QBLK:(j + 3) * QBLK, sw * LANES:(sw + 1) * LANES]
            s_c[j % 2][2 * sw * QBLK:2 * (sw + 1) * QBLK, :] = _dot_nt(q, k_loc) + window

    def c_attend(j):
        step = slice(4 * j * QBLK, 4 * (j + 1) * QBLK)
        values = [(vce[j * QBLK:(j + 3) * QBLK, sw * LANES:(sw + 2) * LANES], sw) for sw in range(2)]
        sink = jnp.concatenate([jnp.full((QBLK, 1), sink_ref[layer, 2 * t + half] * LOG2E, F32)
                                for group in _C_GROUPS for t, half in group], axis=0)
        r0, r1 = _local_pass(s_c[j % 2][...], values, m_ctx[step, :], acc_ctx[step, :], sink)
        yc_scr[j * QBLK:(j + 1) * QBLK, :] = jnp.concatenate(
            [jnp.where(low_q, r0[0:QBLK], r1[0:QBLK]), jnp.where(low_q, r1[QBLK:], r0[QBLK:])], axis=1)

    _skewed(blocks_per_tile, c_scores, c_attend)

    halo = 4 * GRID_W
    kde[0:halo, :] = kdp_ref[...]
    kde[halo:halo + TILE, :] = kd_ref[...]
    kde[halo + TILE:, :] = kdn_ref[...]
    for c_src, c_dst in ((0, 0), (LANES, 2 * LANES)):
        vde[0:halo, c_dst:c_dst + LANES] = vdp_ref[:, c_src:c_src + LANES]
        vde[halo:halo + TILE, c_dst:c_dst + LANES] = vd_ref[:, c_src:c_src + LANES]
        vde[halo + TILE:, c_dst:c_dst + LANES] = vdn_ref[:, c_src:c_src + LANES]
    vde[:, LANES:2 * LANES] = jnp.ones((TILE + 2 * halo, LANES), BF16)
    kd_ctx = cdk_ref[...].astype(BF16)
    cdv = cdv_ref[...].astype(BF16)
    vd_ctx = jnp.concatenate([cdv[:, 0:LANES], ones_ctx, cdv[:, LANES:2 * LANES]], axis=1)
    low_r = _low_half(GRID_W)
    n_rows = DEC_SEQ // GRID_W

    for t, q in enumerate(_stacked_queries(qd_ref, slice(None), low_t, _D_GROUPS)):
        m, acc = _context_pass(q, kd_ctx[:, t * LANES:(t + 1) * LANES], vd_ctx[:, t * LANES:(t + 2) * LANES])
        file_context(m, acc, t, GRID_W, ROWS_PER_TILE)

    def window_start(rl):
        r = i * ROWS_PER_TILE + rl
        rs = jnp.clip(r - WIN_R // 2, 0, n_rows - WIN_R)
        return rs - r + (WIN_R - 1), pl.multiple_of((rs - i * ROWS_PER_TILE + 4) * GRID_W, GRID_W)

    def d_scores(rl):
        didx, off = window_start(rl)
        rows = slice(rl * GRID_W, (rl + 1) * GRID_W)
        for t, q in enumerate(_stacked_queries(qd_ref, rows, low_r, _D_GROUPS)):
            k_loc = kde[pl.ds(off, WIN_R * GRID_W), t * LANES:(t + 1) * LANES]
            bias = jnp.concatenate([tab_ref[2 * t, didx], tab_ref[2 * t + 1, didx]], axis=0)
            s_d[rl % 2][2 * t * GRID_W:2 * (t + 1) * GRID_W, :] = _dot_nt(q, k_loc) + bias

    def d_attend(rl):
        _, off = window_start(rl)
        step = slice(4 * rl * GRID_W, 4 * (rl + 1) * GRID_W)
        values = [(vde[pl.ds(off, WIN_R * GRID_W), t * LANES:(t + 2) * LANES], t) for t in range(2)]
        r0, r1 = _local_pass(s_d[rl % 2][...], values, m_ctx[step, :], acc_ctx[step, :])
        yd_scr[rl * GRID_W:(rl + 1) * GRID_W, :] = jnp.concatenate(
            [jnp.where(low_r, r0[0:GRID_W], r0[GRID_W:]), jnp.where(low_r, r1[0:GRID_W], r1[GRID_W:])], axis=1)

    _skewed(ROWS_PER_TILE, d_scores, d_attend)

    yc = yc_scr[...] * _silu(z_ref[:, 0:256])
    yd = yd_scr[...] * _silu(z_ref[:, 256:512])
    x_new = _out_proj(x_ref[...], mod_ref[2, pl.ds(1 + b, 1), :], ya, yb, yc, yd, w_out_ref)
    if layer == DEPTH - 1:
        x_new = _rms(x_new) * fg_ref[...]
    xo_ref[...] = x_new


def _lat_mix(layer, x, mod, final_g, ab, z, qc, kc, vc, qd, kd, vd, cck, ccv, cdk, cdv, tab,
             w_out, ws_cat, bias_a, w_conv, sink):
    tile = lambda w: pl.BlockSpec((None, TILE, w), lambda b, i: (b, i, 0))
    per_layer = lambda *shape: pl.BlockSpec((None,) + shape, lambda b, i: (layer,) + (0,) * len(shape))

    def halo(rows, w, col=0):
        per = TILE // rows
        last = DEC_SEQ // rows - 1
        prev = pl.BlockSpec((None, rows, w), lambda b, i: (b, jnp.maximum(i * per - 1, 0), col))
        nxt = pl.BlockSpec((None, rows, w), lambda b, i: (b, jnp.minimum((i + 1) * per, last), col))
        return prev, nxt

    abp, abn = halo(8, 512, col=BC // 512)
    cprev, cnext = halo(QBLK, 256)
    dprev, dnext = halo(4 * GRID_W, 256)
    cache = lambda w: pl.BlockSpec((None, None, PAST_LEN, w), lambda b, i: (b, layer, 0, 0))
    return pl.pallas_call(
        functools.partial(_lat_mix_kernel, layer),
        grid=(DEC_BATCH, N_TILES),
        in_specs=[
            pl.BlockSpec(memory_space=pltpu.SMEM),
            tile(D_MODEL),
            per_layer(3, N_GROUPS, D_MODEL),
            pl.BlockSpec((1, D_MODEL), lambda b, i: (0, 0)),
            tile(N_AB), abp, abn, tile(512),
            tile(256), tile(256), cprev, cnext, tile(256), cprev, cnext,
            tile(256), tile(256), dprev, dnext, tile(256), dprev, dnext,
            cache(128), cache(128), cache(256), cache(256),
            pl.BlockSpec((3, QBLK, 3 * QBLK), lambda b, i: (0, 0, 0)),
            per_layer(4, WIN_R, GRID_W, WIN_R * GRID_W),
            per_layer(D_MODEL, D_MODEL),
            per_layer(CHUNK, A_HEADS * CHUNK),
            per_layer(CHUNK, GROUP_W),
            per_layer(3, GROUP_W),
        ],
        out_specs=tile(D_MODEL),
        out_shape=jax.ShapeDtypeStruct((DEC_BATCH, DEC_SEQ, D_MODEL), F32),
        scratch_shapes=[
            pltpu.VMEM((TILE + 2 * QBLK, 2 * LANES), BF16),
            pltpu.VMEM((TILE + 2 * QBLK, 3 * LANES), BF16),
            pltpu.VMEM((TILE + 8 * GRID_W, 2 * LANES), BF16),
            pltpu.VMEM((TILE + 8 * GRID_W, 3 * LANES), BF16),
            pltpu.VMEM((TILE, 256), F32),
            pltpu.VMEM((TILE, 256), F32),
            pltpu.VMEM((4 * TILE, 1), F32),
            pltpu.VMEM((4 * TILE, 2 * LANES), F32),
            pltpu.VMEM((4 * QBLK, 3 * QBLK), F32),
            pltpu.VMEM((4 * QBLK, 3 * QBLK), F32),
            pltpu.VMEM((4 * GRID_W, WIN_R * GRID_W), F32),
            pltpu.VMEM((4 * GRID_W, WIN_R * GRID_W), F32),
        ],
        compiler_params=pltpu.CompilerParams(dimension_semantics=("arbitrary", "arbitrary"),
                                             vmem_limit_bytes=VMEM_LIMIT),
        name="lat_mix",
    )(sink, x, mod, final_g, ab, ab, ab, z, qc, kc, kc, kc, vc, vc, vc,
      qd, kd, kd, kd, vd, vd, vd, cck, ccv, cdk, cdv, _window_masks(), tab, w_out, ws_cat, bias_a, w_conv)


def _window_masks():
    p = np.arange(QBLK)[:, None]
    j = np.arange(3 * QBLK)[None, :]
    band = np.abs(j - QBLK - p) <= WINDOW
    masks = [band, band & (j >= QBLK), band & (j < 2 * QBLK)]
    return jnp.asarray(np.where(np.stack(masks), 0.0, NEG), F32)


def _rope_tables():
    t = np.arange(DEC_SEQ)
    freqs = (np.float32(ROPE_BASE) ** (-np.arange(16, dtype=np.float32) / np.float32(16))).astype(np.float32)
    ang_r = (t // GRID_W).astype(np.float32)[:, None] * freqs
    ang_c = (t % GRID_W).astype(np.float32)[:, None] * freqs
    cos_h = np.concatenate([np.cos(ang_r), np.cos(ang_r), np.cos(ang_c), np.cos(ang_c)], axis=1)
    sin_h = np.concatenate([-np.sin(ang_r), np.sin(ang_r), -np.sin(ang_c), np.sin(ang_c)], axis=1)
    return (jnp.asarray(np.tile(cos_h, (1, 2)), F32), jnp.asarray(np.tile(sin_h, (1, 2)), F32))


def kernel(x_prompt, x_sample, cache_c_k, cache_c_v, cache_d_k, cache_d_v, c, c_ctx, norm_g, w_mod, b_mod,
           w_in, w_out, w_s, b_s, w_conv, sink, rpb, final_g):
    w_in_b = w_in.astype(BF16)
    w_out_b = w_out.astype(BF16)
    ws_cat = jnp.transpose(w_s, (0, 2, 1, 3)).reshape(DEPTH, CHUNK, A_HEADS * CHUNK).astype(BF16)
    bias_a = jnp.repeat(jnp.transpose(b_s, (0, 2, 1)), HEAD_DIM, axis=2)
    norm_g3 = norm_g.reshape(DEPTH, 1, D_MODEL)
    fg = final_g.reshape(1, D_MODEL)
    cos_t, sin_t = _rope_tables()
    cck = cache_c_k.reshape(DEC_BATCH, DEPTH, PAST_LEN, 128)
    ccv = cache_c_v.reshape(DEC_BATCH, DEPTH, PAST_LEN, 128)
    cdk = cache_d_k.reshape(DEC_BATCH, DEPTH, PAST_LEN, 256)
    cdv = cache_d_v.reshape(DEC_BATCH, DEPTH, PAST_LEN, 256)

    cond = jnp.concatenate([c_ctx[None, :], c, jnp.zeros((N_GROUPS - 1 - DEC_BATCH, D_MODEL), F32)], axis=0)
    mod = _modulation(cond, w_mod, b_mod)
    tab = _nbr_tables(rpb)

    y_prompt, s_ck, s_cv, s_dk, s_dv = _ctx_layers(x_prompt, mod, norm_g3, fg, w_in_b, w_out_b, ws_cat,
                                                   bias_a, w_conv, sink)
    xs = x_sample
    for l in range(DEPTH):
        ab, z, qc, kc, vc, qd, kd, vd = _lat_proj(l, xs, mod, norm_g3, w_in_b, cos_t, sin_t)
        xs = _lat_mix(l, xs, mod, fg, ab, z, qc, kc, vc, qd, kd, vd, cck, ccv, cdk, cdv, tab,
                      w_out_b, ws_cat, bias_a, w_conv, sink)

    shape_c = (BATCH, DEPTH, SEQ, 2, HEAD_DIM)
    shape_d = (BATCH, DEPTH, SEQ, 4, HEAD_DIM)
    return (y_prompt, xs, s_ck.reshape(shape_c), s_cv.reshape(shape_c), s_dk.reshape(shape_d),
            s_dv.reshape(shape_d))
```

```python
import functools

import numpy as np
import jax
import jax.numpy as jnp
from jax import lax
from jax.experimental import pallas as pl
from jax.experimental.pallas import tpu as pltpu

D_MODEL = 1024
BATCH = 16
SEQ = 256
DEPTH = 4
DEC_BATCH = 2
DEC_SEQ = 4096
PAST_LEN = 512
GRID_W = 64
GROUP_W = 256
HEAD_DIM = 64
A_HEADS = 4
CHUNK = 128
WINDOW = 128
QBLK = 128
WIN_R = 8
WIN_C = 16
ROPE_BASE = 10000.0
EPS = 1e-6
NEG = -1e30
D_IN = 3584
LOG2E = 1.4426950408889634
Q_SCALE = HEAD_DIM ** -0.5 * LOG2E

AU, AV, AZ, BB, BC, BH, BZ = 0, 256, 512, 768, 1024, 1280, 1536
CQ, CKV, CZ, DQ, DK, DV, DZ = 1792, 2048, 2304, 2560, 2816, 3072, 3328
N_AB = 1792

LANES = 128
TILE = 512
ROWS_PER_TILE = TILE // GRID_W
N_TILES = DEC_SEQ // TILE
D_ROWS = 1
N_GROUPS = 8
VMEM_LIMIT = 56 * 1024 * 1024

F32 = jnp.float32
BF16 = jnp.bfloat16


def _silu(z):
    return z * (1.0 / (1.0 + jnp.exp(-z)))


def _dot(a, b):
    return jnp.dot(a, b, preferred_element_type=F32)


def _dot_nt(a, b):
    return lax.dot_general(a, b, (((1,), (1,)), ((), ())), preferred_element_type=F32)


def _rms(x):
    return x * lax.rsqrt(jnp.mean(x * x, axis=-1, keepdims=True) + EPS)


def _norm_mod(x, g, shift, scale):
    return _rms(x) * g * (1.0 + scale) + shift


def _swap64(x):
    return pltpu.roll(x, HEAD_DIM, axis=1)


def _group_mean_matrix():
    r = lax.broadcasted_iota(jnp.int32, (GROUP_W, GROUP_W), 0) // HEAD_DIM
    c = lax.broadcasted_iota(jnp.int32, (GROUP_W, GROUP_W), 1) // HEAD_DIM
    return jnp.where(r == c, 1.0 / HEAD_DIM, 0.0).astype(BF16)


def _branch_a(au, av, az, ws, bias):
    t = av.shape[0]
    sq = av * av
    hi = sq.astype(BF16)
    lo = (sq - hi.astype(F32)).astype(BF16)
    gm = _group_mean_matrix()
    ms = _dot(hi, gm) + _dot(lo, gm)
    vh = (av * lax.rsqrt(ms + EPS)).astype(BF16)
    head = lax.broadcasted_iota(jnp.int32, (CHUNK, GROUP_W), 1) // HEAD_DIM
    outs = []
    for n in range(t // CHUNK):
        v = vh[n * CHUNK:(n + 1) * CHUNK]
        rhs = jnp.concatenate([jnp.where(head == h, v, jnp.zeros_like(v)) for h in range(A_HEADS)], axis=0)
        outs.append(_dot(ws, rhs) + bias)
    mixed = jnp.concatenate(outs, axis=0)
    return au * mixed * _silu(az)


def _branch_b(bb, bc, bh, bz, wc, prev_row, next_row):
    t = bb.shape[0]
    xc = bc * bh
    row = lax.broadcasted_iota(jnp.int32, xc.shape, 0)
    xm = jnp.where(row == 0, prev_row, pltpu.roll(xc, 1, axis=0))
    xp = jnp.where(row == t - 1, next_row, pltpu.roll(xc, t - 1, axis=0))
    y = wc[0:1, :] * xm + wc[1:2, :] * xc + wc[2:3, :] * xp
    return bb * y * _silu(bz)


def _attend(score, value, pos, sink=None):
    m = jnp.max(score, axis=-1, keepdims=True)
    if sink is not None:
        m = jnp.maximum(m, sink * LOG2E)
    acc = _dot(jnp.exp2(score - m).astype(BF16), value)
    o = acc[:, pos * LANES:(pos + 1) * LANES]
    l = acc[:, (1 - pos) * LANES:(2 - pos) * LANES]
    if sink is not None:
        l = l + jnp.exp2(sink * LOG2E - m)
    return o * (1.0 / l)


def _context_pass(q_stacked, k, value):
    s = _dot_nt(q_stacked, k)
    m = jnp.max(s, axis=-1, keepdims=True)
    return m, _dot(jnp.exp2(s - m).astype(BF16), value)


def _local_softmax(s_loc, m_ctx, sink=None):
    m = jnp.maximum(jnp.max(s_loc, axis=-1, keepdims=True), m_ctx)
    if sink is not None:
        m = jnp.maximum(m, sink)
    e = jnp.exp2(s_loc - m).astype(BF16)
    return e, jnp.exp2(m_ctx - m), (None if sink is None else jnp.exp2(sink - m))


def _local_output(e, values, acc_ctx, ctx_scale, sink_term=None):
    rows = e.shape[0] // len(values)
    outs = []
    for p, (v, pos) in enumerate(values):
        r = slice(p * rows, (p + 1) * rows)
        acc = _dot(e[r], v) + acc_ctx[r] * ctx_scale[r]
        o = acc[:, pos * LANES:(pos + 1) * LANES]
        l = acc[:, (1 - pos) * LANES:(2 - pos) * LANES]
        if sink_term is not None:
            l = l + sink_term[r]
        outs.append(o * (1.0 / l))
    return outs


def _skewed(n_steps, stages):
    for tick in range(n_steps + len(stages) - 1):
        for lag, stage in enumerate(stages):
            if 0 <= tick - lag < n_steps:
                stage(tick - lag)


def _low_half(m):
    return lax.broadcasted_iota(jnp.int32, (m, LANES), 1) < HEAD_DIM


def _mask_half(q, low, half):
    z = jnp.zeros_like(q)
    return jnp.where(low, q, z) if half == 0 else jnp.where(low, z, q)


def _out_proj(x, gate, ya, yb, yc, yd, w_out_ref):
    y = _dot(ya.astype(BF16), w_out_ref[0:256, :])
    y += _dot(yb.astype(BF16), w_out_ref[256:512, :])
    y += _dot(yc.astype(BF16), w_out_ref[512:768, :])
    y += _dot(yd.astype(BF16), w_out_ref[768:1024, :])
    return x + gate * y


def _mod_kernel(cond_ref, w_ref, b_ref, o_ref):
    o_ref[...] = jnp.dot(_silu(cond_ref[...]), w_ref[...], preferred_element_type=F32,
                         precision=lax.Precision.HIGHEST) + b_ref[...]


def _modulation(cond, w_mod, b_mod):
    return pl.pallas_call(
        _mod_kernel,
        grid=(DEPTH, 3),
        in_specs=[
            pl.BlockSpec((N_GROUPS, D_MODEL), lambda l, j: (0, 0)),
            pl.BlockSpec((None, D_MODEL, D_MODEL), lambda l, j: (l, 0, j)),
            pl.BlockSpec((None, 1, D_MODEL), lambda l, j: (l, 0, j)),
        ],
        out_specs=pl.BlockSpec((None, None, N_GROUPS, D_MODEL), lambda l, j: (l, j, 0, 0)),
        out_shape=jax.ShapeDtypeStruct((DEPTH, 3, N_GROUPS, D_MODEL), F32),
        compiler_params=pltpu.CompilerParams(dimension_semantics=("arbitrary", "arbitrary")),
        name="modulation",
    )(cond, w_mod, b_mod.reshape(DEPTH, 1, 3 * D_MODEL))


def _nbr_table_kernel(rpb_ref, o_ref):
    c = lax.broadcasted_iota(jnp.int32, (GRID_W, LANES), 0)
    lane = lax.broadcasted_iota(jnp.int32, (GRID_W, LANES), 1)
    cc = lane % GRID_W
    cstart = jnp.clip(c - WIN_C // 2, 0, GRID_W - WIN_C)
    low = lane < GRID_W
    t_low = []
    t_high = []
    for ro in range(2 * WIN_R - 1):
        x = jnp.broadcast_to(rpb_ref[ro:ro + 1, :] * LOG2E, (GRID_W, LANES))
        t_low.append(pltpu.roll(x, LANES - (WIN_C - 1), 1, stride=1, stride_axis=0))
        t_high.append(pltpu.roll(x, GRID_W - (WIN_C - 1), 1, stride=1, stride_axis=0))
    for d in range(WIN_R):
        for p in range(WIN_R // 2):
            v = jnp.where(low, t_low[d + 2 * p], t_high[d + 2 * p + 1])
            v = jnp.where(cc >= cstart, jnp.where(cc < cstart + WIN_C, v, NEG), NEG)
            o_ref[d, :, p * LANES:(p + 1) * LANES] = v


def _nbr_tables(rpb):
    rpb_p = jnp.pad(rpb.reshape(DEPTH * 4, 2 * WIN_R - 1, 2 * WIN_C - 1), ((0, 0), (0, 1), (0, LANES - 31)))
    out = pl.pallas_call(
        _nbr_table_kernel,
        grid=(DEPTH * 4,),
        in_specs=[pl.BlockSpec((None, 2 * WIN_R, LANES), lambda i: (i, 0, 0))],
        out_specs=pl.BlockSpec((None, WIN_R, GRID_W, WIN_R * GRID_W), lambda i: (i, 0, 0, 0)),
        out_shape=jax.ShapeDtypeStruct((DEPTH * 4, WIN_R, GRID_W, WIN_R * GRID_W), F32),
        compiler_params=pltpu.CompilerParams(dimension_semantics=("arbitrary",)),
        name="nbr_tables",
    )(rpb_p)
    return out.reshape(DEPTH, 4, WIN_R, GRID_W, WIN_R * GRID_W)


def _ctx_kernel(sink_ref, x_ref, mod_ref, g_ref, fg_ref, w_in_ref, w_out_ref, ws_ref, ba_ref, wc_ref,
                y_ref, ck_ref, cv_ref, dk_ref, dv_ref, xs):
    l = pl.program_id(0)
    b = pl.program_id(1)

    @pl.when(l == 0)
    def _():
        xs[b] = x_ref[...]

    x = xs[b]
    h = _norm_mod(x, g_ref[...], mod_ref[0, 0:1, :], mod_ref[1, 0:1, :]).astype(BF16)
    p = _dot(h, w_in_ref[...])
    kcf = p[:, CKV:CKV + 128]
    vcf = p[:, CKV + 128:CKV + 256]
    kdf = p[:, DK:DK + 256]
    vdf = p[:, DV:DV + 256]
    ck_ref[...] = kcf
    cv_ref[...] = vcf
    dk_ref[...] = kdf
    dv_ref[...] = vdf

    ya = _branch_a(p[:, AU:AU + 256], p[:, AV:AV + 256], p[:, AZ:AZ + 256], ws_ref[...], ba_ref[...])
    zero_row = jnp.zeros((1, GROUP_W), F32)
    yb = _branch_b(p[:, BB:BB + 256], p[:, BC:BC + 256], p[:, BH:BH + 256], p[:, BZ:BZ + 256],
                   wc_ref[...], zero_row, zero_row)

    low = _low_half(SEQ)

    ones = jnp.ones((SEQ, LANES), BF16)
    kc = (kcf.astype(BF16), _swap64(kcf).astype(BF16))
    vc = (jnp.concatenate([vcf.astype(BF16), ones], axis=1),
          jnp.concatenate([ones, _swap64(vcf).astype(BF16)], axis=1))
    o = []
    for t in range(2):
        q = (p[:, CQ + t * LANES:CQ + (t + 1) * LANES] * Q_SCALE).astype(BF16)
        for half in range(2):
            sw = (t + half) % 2
            o.append(_attend(_dot_nt(_mask_half(q, low, half), kc[sw]), vc[sw], sw, sink_ref[l, 2 * t + half]))
    yc = jnp.concatenate([jnp.where(low, o[0], o[1]), jnp.where(low, o[2], o[3])], axis=1)
    yc = yc * _silu(p[:, CZ:CZ + 256])

    o = []
    for t in range(2):
        q = (p[:, DQ + t * LANES:DQ + (t + 1) * LANES] * Q_SCALE).astype(BF16)
        k = kdf[:, t * LANES:(t + 1) * LANES].astype(BF16)
        v = jnp.concatenate([vdf[:, t * LANES:(t + 1) * LANES].astype(BF16), ones], axis=1)
        for half in range(2):
            o.append(_attend(_dot_nt(_mask_half(q, low, half), k), v, 0))
    yd = jnp.concatenate([jnp.where(low, o[0], o[1]), jnp.where(low, o[2], o[3])], axis=1)
    yd = yd * _silu(p[:, DZ:DZ + 256])

    x_new = _out_proj(x, mod_ref[2, 0:1, :], ya, yb, yc, yd, w_out_ref)
    xs[b] = x_new

    @pl.when(l == DEPTH - 1)
    def _():
        y_ref[...] = _rms(x_new) * fg_ref[...]


def _ctx_layers(x, mod, norm_g, final_g, w_in, w_out, ws_cat, bias_a, w_conv, sink):
    per_layer = lambda *shape: pl.BlockSpec((None,) + shape, lambda l, b: (l,) + (0,) * len(shape))
    state = lambda w: pl.BlockSpec((None, None, SEQ, w), lambda l, b: (b, l, 0, 0))
    return pl.pallas_call(
        _ctx_kernel,
        grid=(DEPTH, BATCH),
        in_specs=[
            pl.BlockSpec(memory_space=pltpu.SMEM),
            pl.BlockSpec((None, SEQ, D_MODEL), lambda l, b: (jnp.where(l == 0, b, BATCH - 1), 0, 0)),
            per_layer(3, N_GROUPS, D_MODEL),
            per_layer(1, D_MODEL),
            pl.BlockSpec((1, D_MODEL), lambda l, b: (0, 0)),
            pl.BlockSpec((None, D_MODEL, D_IN), lambda l, b: (l, 0, 0), pipeline_mode=pl.Buffered(1)),
            pl.BlockSpec((None, D_MODEL, D_MODEL), lambda l, b: (l, 0, 0), pipeline_mode=pl.Buffered(1)),
            per_layer(CHUNK, A_HEADS * CHUNK),
            per_layer(CHUNK, GROUP_W),
            per_layer(3, GROUP_W),
        ],
        out_specs=[
            pl.BlockSpec((None, SEQ, D_MODEL), lambda l, b: (jnp.where(l == DEPTH - 1, b, 0), 0, 0)),
            state(128), state(128), state(256), state(256),
        ],
        out_shape=[
            jax.ShapeDtypeStruct((BATCH, SEQ, D_MODEL), F32),
            jax.ShapeDtypeStruct((BATCH, DEPTH, SEQ, 128), F32),
            jax.ShapeDtypeStruct((BATCH, DEPTH, SEQ, 128), F32),
            jax.ShapeDtypeStruct((BATCH, DEPTH, SEQ, 256), F32),
            jax.ShapeDtypeStruct((BATCH, DEPTH, SEQ, 256), F32),
        ],
        scratch_shapes=[pltpu.VMEM((BATCH, SEQ, D_MODEL), F32)],
        compiler_params=pltpu.CompilerParams(dimension_semantics=("arbitrary", "arbitrary"),
                                             vmem_limit_bytes=VMEM_LIMIT),
        name="ctx_layers",
    )(sink, x, mod, norm_g, final_g, w_in, w_out, ws_cat, bias_a, w_conv)


def _rope(x, cos, sin_signed, first_half):
    swapped = jnp.where(first_half, pltpu.roll(x, LANES - 16, axis=1), pltpu.roll(x, 16, axis=1))
    return x * cos + swapped * sin_signed


def _lat_proj_kernel(x_ref, xp_ref, xn_ref, mod_ref, g_ref, w_ref, cos_ref, sin_ref, ws_ref, ba_ref, wc_ref,
                     yab_ref, sz_ref, qc_ref, kc_ref, vc_ref, qd_ref, kd_ref, vd_ref):
    i = pl.program_id(1)
    grp = pl.ds(1 + pl.program_id(0), 1)
    g = g_ref[...]
    shift = mod_ref[0, grp, :]
    scale = mod_ref[1, grp, :]
    h = _norm_mod(x_ref[...], g, shift, scale).astype(BF16)

    pa = _dot(h, w_ref[:, AU:AU + 3 * GROUP_W])
    ya = _branch_a(pa[:, 0:256], pa[:, 256:512], pa[:, 512:768], ws_ref[...], ba_ref[...])
    yab_ref[:, 0:GROUP_W] = ya.astype(BF16)

    pb = _dot(h, w_ref[:, BB:BB + 4 * GROUP_W])
    x_edge = jnp.concatenate([xp_ref[...], xn_ref[...]], axis=0)
    e = _dot(_norm_mod(x_edge, g, shift, scale).astype(BF16), w_ref[:, BC:BC + 2 * GROUP_W])
    prev_row = jnp.where(i > 0, e[7:8, 0:256] * e[7:8, 256:512], 0.0)
    next_row = jnp.where(i < N_TILES - 1, e[8:9, 0:256] * e[8:9, 256:512], 0.0)
    yb = _branch_b(pb[:, 0:256], pb[:, 256:512], pb[:, 512:768], pb[:, 768:1024], wc_ref[...],
                   prev_row, next_row)
    yab_ref[:, GROUP_W:2 * GROUP_W] = yb.astype(BF16)

    sz_ref[:, 0:256] = _silu(_dot(h, w_ref[:, CZ:CZ + 256]))
    sz_ref[:, 256:512] = _silu(_dot(h, w_ref[:, DZ:DZ + 256]))
    cos = cos_ref[...]
    sin = sin_ref[...]
    first_half = (lax.broadcasted_iota(jnp.int32, (TILE, LANES), 1) % 32) < 16
    qc = _dot(h, w_ref[:, CQ:CQ + 256])
    for t in range(2):
        qt = _rope(qc[:, t * LANES:(t + 1) * LANES], cos, sin, first_half) * Q_SCALE
        qc_ref[:, t * LANES:(t + 1) * LANES] = qt.astype(BF16)
    kv = _dot(h, w_ref[:, CKV:CKV + 256])
    k = _rope(kv[:, 0:LANES], cos, sin, first_half)
    v = kv[:, LANES:2 * LANES]
    kc_ref[:, 0:LANES] = k.astype(BF16)
    kc_ref[:, LANES:2 * LANES] = _swap64(k).astype(BF16)
    vc_ref[:, 0:LANES] = v.astype(BF16)
    vc_ref[:, LANES:2 * LANES] = _swap64(v).astype(BF16)
    qd_ref[...] = (_dot(h, w_ref[:, DQ:DQ + 256]) * Q_SCALE).astype(BF16)
    kd_ref[...] = _dot(h, w_ref[:, DK:DK + 256]).astype(BF16)
    vd_ref[...] = _dot(h, w_ref[:, DV:DV + 256]).astype(BF16)


def _lat_proj(layer, x, mod, norm_g, w_in, cos_t, sin_t, ws_cat, bias_a, w_conv):
    tile = lambda w: pl.BlockSpec((None, TILE, w), lambda b, i: (b, i, 0))
    per_layer = lambda *shape: pl.BlockSpec((None,) + shape, lambda b, i: (layer,) + (0,) * len(shape))
    sds = lambda w, dt: jax.ShapeDtypeStruct((DEC_BATCH, DEC_SEQ, w), dt)
    edge = TILE // 8
    last = DEC_SEQ // 8 - 1
    return pl.pallas_call(
        _lat_proj_kernel,
        grid=(DEC_BATCH, N_TILES),
        in_specs=[
            tile(D_MODEL),
            pl.BlockSpec((None, 8, D_MODEL), lambda b, i: (b, jnp.maximum(i * edge - 1, 0), 0)),
            pl.BlockSpec((None, 8, D_MODEL), lambda b, i: (b, jnp.minimum((i + 1) * edge, last), 0)),
            per_layer(3, N_GROUPS, D_MODEL),
            per_layer(1, D_MODEL),
            per_layer(D_MODEL, D_IN),
            pl.BlockSpec((TILE, LANES), lambda b, i: (i, 0)),
            pl.BlockSpec((TILE, LANES), lambda b, i: (i, 0)),
            per_layer(CHUNK, A_HEADS * CHUNK),
            per_layer(CHUNK, GROUP_W),
            per_layer(3, GROUP_W),
        ],
        out_specs=[tile(512), tile(512)] + [tile(256)] * 6,
        out_shape=[sds(512, BF16), sds(512, F32)] + [sds(256, BF16)] * 6,
        compiler_params=pltpu.CompilerParams(dimension_semantics=("arbitrary", "arbitrary"),
                                             vmem_limit_bytes=VMEM_LIMIT),
        name="lat_proj",
    )(x, x, x, mod, norm_g, w_in, cos_t, sin_t, ws_cat, bias_a, w_conv)


_C_GROUPS = (((0, 0), (1, 1)), ((0, 1), (1, 0)))
_D_GROUPS = (((0, 0), (0, 1)), ((1, 0), (1, 1)))


def _stacked_queries(q_ref, rows, low, groups):
    return [jnp.concatenate([_mask_half(q_ref[rows, t * LANES:(t + 1) * LANES], low, half)
                             for t, half in group], axis=0) for group in groups]


def _lat_mix_kernel(layer, sink_ref, x_ref, mod_ref, fg_ref, yab_ref, sz_ref,
                    qc_ref, kc_ref, kcp_ref, kcn_ref, vc_ref, vcp_ref, vcn_ref,
                    qd_ref, kd_ref, kdp_ref, kdn_ref, vd_ref, vdp_ref, vdn_ref,
                    cck_ref, ccv_ref, cdk_ref, cdv_ref, cmask_ref, tab_ref, w_out_ref,
                    xo_ref,
                    kce, vce, kde, vde, yc_scr, yd_scr, m_ctx, acc_ctx, *stage_bufs):
    b = pl.program_id(0)
    i = pl.program_id(1)
    s_c, s_d = stage_bufs[0:2], stage_bufs[2:4]

    def file_context(m, acc, group, step_rows, n_steps):
        for which in range(2):
            for k in range(n_steps):
                src = slice(which * TILE + k * step_rows, which * TILE + (k + 1) * step_rows)
                dst = slice((4 * k + 2 * group + which) * step_rows, (4 * k + 2 * group + which + 1) * step_rows)
                m_ctx[dst, :] = m[src]
                acc_ctx[dst, :] = acc[src]

    kce[0:QBLK, :] = kcp_ref[...]
    kce[QBLK:QBLK + TILE, :] = kc_ref[...]
    kce[QBLK + TILE:, :] = kcn_ref[...]
    for c_src, c_dst in ((0, 0), (LANES, 2 * LANES)):
        vce[0:QBLK, c_dst:c_dst + LANES] = vcp_ref[:, c_src:c_src + LANES]
        vce[QBLK:QBLK + TILE, c_dst:c_dst + LANES] = vc_ref[:, c_src:c_src + LANES]
        vce[QBLK + TILE:, c_dst:c_dst + LANES] = vcn_ref[:, c_src:c_src + LANES]
    vce[:, LANES:2 * LANES] = jnp.ones((TILE + 2 * QBLK, LANES), BF16)
    cck = cck_ref[...]
    ccv = ccv_ref[...]
    ones_ctx = jnp.ones((PAST_LEN, LANES), BF16)
    k_ctx = (cck.astype(BF16), _swap64(cck).astype(BF16))
    v_ctx = jnp.concatenate([ccv.astype(BF16), ones_ctx, _swap64(ccv).astype(BF16)], axis=1)
    low_q = _low_half(QBLK)
    low_t = _low_half(TILE)
    n_blocks = DEC_SEQ // QBLK
    blocks_per_tile = TILE // QBLK

    for sw, q in enumerate(_stacked_queries(qc_ref, slice(None), low_t, _C_GROUPS)):
        m, acc = _context_pass(q, k_ctx[sw], v_ctx[:, sw * LANES:(sw + 2) * LANES])
        file_context(m, acc, sw, QBLK, blocks_per_tile)

    def c_scores(j):
        n = i * blocks_per_tile + j
        window = cmask_ref[jnp.where(n == 0, 1, jnp.where(n == n_blocks - 1, 2, 0))]
        window = jnp.concatenate([window, window], axis=0)
        rows = slice(j * QBLK, (j + 1) * QBLK)
        for sw, q in enumerate(_stacked_queries(qc_ref, rows, low_q, _C_GROUPS)):
            k_loc = kce[j * QBLK:(j + 3) * QBLK, sw * LANES:(sw + 1) * LANES]
            s_c[j % 2][2 * sw * QBLK:2 * (sw + 1) * QBLK, :] = _dot_nt(q, k_loc) + window

    def c_softmax(j):
        step = slice(4 * j * QBLK, 4 * (j + 1) * QBLK)
        sink = jnp.concatenate([jnp.full((QBLK, 1), sink_ref[layer, 2 * t + half] * LOG2E, F32)
                                for group in _C_GROUPS for t, half in group], axis=0)
        e, ctx_scale, sink_term = _local_softmax(s_c[j % 2][...], m_ctx[step, :], sink)
        values = [(vce[j * QBLK:(j + 3) * QBLK, sw * LANES:(sw + 2) * LANES], sw) for sw in range(2)]
        r0, r1 = _local_output(e, values, acc_ctx[step, :], ctx_scale, sink_term)
        yc_scr[j * QBLK:(j + 1) * QBLK, :] = jnp.concatenate(
            [jnp.where(low_q, r0[0:QBLK], r1[0:QBLK]), jnp.where(low_q, r1[QBLK:], r0[QBLK:])], axis=1)

    _skewed(blocks_per_tile, (c_scores, c_softmax))

    halo = 4 * GRID_W
    kde[0:halo, :] = kdp_ref[...]
    kde[halo:halo + TILE, :] = kd_ref[...]
    kde[halo + TILE:, :] = kdn_ref[...]
    for c_src, c_dst in ((0, 0), (LANES, 2 * LANES)):
        vde[0:halo, c_dst:c_dst + LANES] = vdp_ref[:, c_src:c_src + LANES]
        vde[halo:halo + TILE, c_dst:c_dst + LANES] = vd_ref[:, c_src:c_src + LANES]
        vde[halo + TILE:, c_dst:c_dst + LANES] = vdn_ref[:, c_src:c_src + LANES]
    vde[:, LANES:2 * LANES] = jnp.ones((TILE + 2 * halo, LANES), BF16)
    kd_ctx = cdk_ref[...].astype(BF16)
    cdv = cdv_ref[...].astype(BF16)
    vd_ctx = jnp.concatenate([cdv[:, 0:LANES], ones_ctx, cdv[:, LANES:2 * LANES]], axis=1)
    low_r = _low_half(GRID_W)
    n_rows = DEC_SEQ // GRID_W

    for t, q in enumerate(_stacked_queries(qd_ref, slice(None), low_t, _D_GROUPS)):
        m, acc = _context_pass(q, kd_ctx[:, t * LANES:(t + 1) * LANES], vd_ctx[:, t * LANES:(t + 2) * LANES])
        file_context(m, acc, t, GRID_W, ROWS_PER_TILE)

    def window_start(rl):
        r = i * ROWS_PER_TILE + rl
        rs = jnp.clip(r - WIN_R // 2, 0, n_rows - WIN_R)
        return rs - r + (WIN_R - 1), pl.multiple_of((rs - i * ROWS_PER_TILE + 4) * GRID_W, GRID_W)

    def d_scores(k):
        for rr in range(D_ROWS):
            rl = D_ROWS * k + rr
            didx, off = window_start(rl)
            rows = slice(rl * GRID_W, (rl + 1) * GRID_W)
            for t, q in enumerate(_stacked_queries(qd_ref, rows, low_r, _D_GROUPS)):
                k_loc = kde[pl.ds(off, WIN_R * GRID_W), t * LANES:(t + 1) * LANES]
                bias = jnp.concatenate([tab_ref[2 * t, didx], tab_ref[2 * t + 1, didx]], axis=0)
                dst = (4 * rr + 2 * t) * GRID_W
                s_d[k % 2][dst:dst + 2 * GRID_W, :] = _dot_nt(q, k_loc) + bias

    def d_softmax(k):
        step = slice(4 * D_ROWS * k * GRID_W, 4 * D_ROWS * (k + 1) * GRID_W)
        e, ctx_scale, _ = _local_softmax(s_d[k % 2][...], m_ctx[step, :])
        offs = [window_start(D_ROWS * k + rr)[1] for rr in range(D_ROWS)]
        values = [(vde[pl.ds(off, WIN_R * GRID_W), t * LANES:(t + 2) * LANES], t) for off in offs for t in range(2)]
        r = _local_output(e, values, acc_ctx[step, :], ctx_scale)
        for rr in range(D_ROWS):
            rl = D_ROWS * k + rr
            yd_scr[rl * GRID_W:(rl + 1) * GRID_W, :] = jnp.concatenate(
                [jnp.where(low_r, r[2 * rr + t][0:GRID_W], r[2 * rr + t][GRID_W:]) for t in range(2)], axis=1)

    _skewed(ROWS_PER_TILE // D_ROWS, (d_scores, d_softmax))

    yc = (yc_scr[...] * sz_ref[:, 0:256]).astype(BF16)
    yd = (yd_scr[...] * sz_ref[:, 256:512]).astype(BF16)
    y = (_dot(yab_ref[...], w_out_ref[0:2 * GROUP_W, :]) + _dot(yc, w_out_ref[2 * GROUP_W:3 * GROUP_W, :])
         + _dot(yd, w_out_ref[3 * GROUP_W:4 * GROUP_W, :]))
    x_new = x_ref[...] + mod_ref[2, pl.ds(1 + b, 1), :] * y
    if layer == DEPTH - 1:
        x_new = _rms(x_new) * fg_ref[...]
    xo_ref[...] = x_new


def _lat_mix(layer, x, mod, final_g, yab, sz, qc, kc, vc, qd, kd, vd, cck, ccv, cdk, cdv, tab, w_out, sink):
    tile = lambda w: pl.BlockSpec((None, TILE, w), lambda b, i: (b, i, 0))
    per_layer = lambda *shape: pl.BlockSpec((None,) + shape, lambda b, i: (layer,) + (0,) * len(shape))

    def halo(rows, w):
        per = TILE // rows
        last = DEC_SEQ // rows - 1
        prev = pl.BlockSpec((None, rows, w), lambda b, i: (b, jnp.maximum(i * per - 1, 0), 0))
        nxt = pl.BlockSpec((None, rows, w), lambda b, i: (b, jnp.minimum((i + 1) * per, last), 0))
        return prev, nxt

    cprev, cnext = halo(QBLK, 256)
    dprev, dnext = halo(4 * GRID_W, 256)
    cache = lambda w: pl.BlockSpec((None, None, PAST_LEN, w), lambda b, i: (b, layer, 0, 0))
    return pl.pallas_call(
        functools.partial(_lat_mix_kernel, layer),
        grid=(DEC_BATCH, N_TILES),
        in_specs=[
            pl.BlockSpec(memory_space=pltpu.SMEM),
            tile(D_MODEL),
            per_layer(3, N_GROUPS, D_MODEL),
            pl.BlockSpec((1, D_MODEL), lambda b, i: (0, 0)),
            tile(512), tile(512),
            tile(256), tile(256), cprev, cnext, tile(256), cprev, cnext,
            tile(256), tile(256), dprev, dnext, tile(256), dprev, dnext,
            cache(128), cache(128), cache(256), cache(256),
            pl.BlockSpec((3, QBLK, 3 * QBLK), lambda b, i: (0, 0, 0)),
            per_layer(4, WIN_R, GRID_W, WIN_R * GRID_W),
            per_layer(D_MODEL, D_MODEL),
        ],
        out_specs=tile(D_MODEL),
        out_shape=jax.ShapeDtypeStruct((DEC_BATCH, DEC_SEQ, D_MODEL), F32),
        scratch_shapes=[
            pltpu.VMEM((TILE + 2 * QBLK, 2 * LANES), BF16),
            pltpu.VMEM((TILE + 2 * QBLK, 3 * LANES), BF16),
            pltpu.VMEM((TILE + 8 * GRID_W, 2 * LANES), BF16),
            pltpu.VMEM((TILE + 8 * GRID_W, 3 * LANES), BF16),
            pltpu.VMEM((TILE, 256), F32),
            pltpu.VMEM((TILE, 256), F32),
            pltpu.VMEM((4 * TILE, 1), F32),
            pltpu.VMEM((4 * TILE, 2 * LANES), F32),
        ] + 2 * [pltpu.VMEM((4 * QBLK, 3 * QBLK), F32)] + 2 * [pltpu.VMEM((4 * D_ROWS * GRID_W, WIN_R * GRID_W), F32)],
        compiler_params=pltpu.CompilerParams(dimension_semantics=("arbitrary", "arbitrary"),
                                             vmem_limit_bytes=VMEM_LIMIT),
        name="lat_mix",
    )(sink, x, mod, final_g, yab, sz, qc, kc, kc, kc, vc, vc, vc,
      qd, kd, kd, kd, vd, vd, vd, cck, ccv, cdk, cdv, _window_masks(), tab, w_out)


def _window_masks():
    p = np.arange(QBLK)[:, None]
    j = np.arange(3 * QBLK)[None, :]
    band = np.abs(j - QBLK - p) <= WINDOW
    masks = [band, band & (j >= QBLK), band & (j < 2 * QBLK)]
    return jnp.asarray(np.where(np.stack(masks), 0.0, NEG), F32)


def _rope_tables():
    t = np.arange(DEC_SEQ)
    freqs = (np.float32(ROPE_BASE) ** (-np.arange(16, dtype=np.float32) / np.float32(16))).astype(np.float32)
    ang_r = (t // GRID_W).astype(np.float32)[:, None] * freqs
    ang_c = (t % GRID_W).astype(np.float32)[:, None] * freqs
    cos_h = np.concatenate([np.cos(ang_r), np.cos(ang_r), np.cos(ang_c), np.cos(ang_c)], axis=1)
    sin_h = np.concatenate([-np.sin(ang_r), np.sin(ang_r), -np.sin(ang_c), np.sin(ang_c)], axis=1)
    return (jnp.asarray(np.tile(cos_h, (1, 2)), F32), jnp.asarray(np.tile(sin_h, (1, 2)), F32))


def kernel(x_prompt, x_sample, cache_c_k, cache_c_v, cache_d_k, cache_d_v, c, c_ctx, norm_g, w_mod, b_mod,
           w_in, w_out, w_s, b_s, w_conv, sink, rpb, final_g):
    w_in_b = w_in.astype(BF16)
    w_out_b = w_out.astype(BF16)
    ws_cat = jnp.transpose(w_s, (0, 2, 1, 3)).reshape(DEPTH, CHUNK, A_HEADS * CHUNK).astype(BF16)
    bias_a = jnp.repeat(jnp.transpose(b_s, (0, 2, 1)), HEAD_DIM, axis=2)
    norm_g3 = norm_g.reshape(DEPTH, 1, D_MODEL)
    fg = final_g.reshape(1, D_MODEL)
    cos_t, sin_t = _rope_tables()
    cck = cache_c_k.reshape(DEC_BATCH, DEPTH, PAST_LEN, 128)
    ccv = cache_c_v.reshape(DEC_BATCH, DEPTH, PAST_LEN, 128)
    cdk = cache_d_k.reshape(DEC_BATCH, DEPTH, PAST_LEN, 256)
    cdv = cache_d_v.reshape(DEC_BATCH, DEPTH, PAST_LEN, 256)

    cond = jnp.concatenate([c_ctx[None, :], c, jnp.zeros((N_GROUPS - 1 - DEC_BATCH, D_MODEL), F32)], axis=0)
    mod = _modulation(cond, w_mod, b_mod)
    tab = _nbr_tables(rpb)

    y_prompt, s_ck, s_cv, s_dk, s_dv = _ctx_layers(x_prompt, mod, norm_g3, fg, w_in_b, w_out_b, ws_cat,
                                                   bias_a, w_conv, sink)
    xs = x_sample
    for l in range(DEPTH):
        yab, sz, qc, kc, vc, qd, kd, vd = _lat_proj(l, xs, mod, norm_g3, w_in_b, cos_t, sin_t,
                                                    ws_cat, bias_a, w_conv)
        xs = _lat_mix(l, xs, mod, fg, yab, sz, qc, kc, vc, qd, kd, vd, cck, ccv, cdk, cdv, tab, w_out_b, sink)

    shape_c = (BATCH, DEPTH, SEQ, 2, HEAD_DIM)
    shape_d = (BATCH, DEPTH, SEQ, 4, HEAD_DIM)
    return (y_prompt, xs, s_ck.reshape(shape_c), s_cv.reshape(shape_c), s_dk.reshape(shape_d),
            s_dv.reshape(shape_d))
```

```python
import functools

import numpy as np
import jax
import jax.numpy as jnp
from jax import lax
from jax.experimental import pallas as pl
from jax.experimental.pallas import tpu as pltpu

D_MODEL = 1024
BATCH = 16
SEQ = 256
DEPTH = 4
DEC_BATCH = 2
DEC_SEQ = 4096
PAST_LEN = 512
GRID_W = 64
GROUP_W = 256
HEAD_DIM = 64
A_HEADS = 4
CHUNK = 128
WINDOW = 128
QBLK = 128
WIN_R = 8
WIN_C = 16
ROPE_BASE = 10000.0
EPS = 1e-6
NEG = -1e30
D_IN = 3584
LOG2E = 1.4426950408889634
Q_SCALE = HEAD_DIM ** -0.5 * LOG2E

AU, AV, AZ, BB, BC, BH, BZ = 0, 256, 512, 768, 1024, 1280, 1536
CQ, CKV, CZ, DQ, DK, DV, DZ = 1792, 2048, 2304, 2560, 2816, 3072, 3328
N_AB = 1792

LANES = 128
TILE = 512
ROWS_PER_TILE = TILE // GRID_W
N_TILES = DEC_SEQ // TILE
N_GROUPS = 8
VMEM_LIMIT = 56 * 1024 * 1024

F32 = jnp.float32
BF16 = jnp.bfloat16


def _silu(z):
    return z * (1.0 / (1.0 + jnp.exp(-z)))


def _dot(a, b):
    return jnp.dot(a, b, preferred_element_type=F32)


def _dot_nt(a, b):
    return lax.dot_general(a, b, (((1,), (1,)), ((), ())), preferred_element_type=F32)


def _rms(x):
    return x * lax.rsqrt(jnp.mean(x * x, axis=-1, keepdims=True) + EPS)


def _norm_mod(x, g, shift, scale):
    return _rms(x) * (g * (1.0 + scale)) + shift


def _swap64(x):
    return pltpu.roll(x, HEAD_DIM, axis=1)


def _group_mean_matrix():
    r = lax.broadcasted_iota(jnp.int32, (GROUP_W, GROUP_W), 0) // HEAD_DIM
    c = lax.broadcasted_iota(jnp.int32, (GROUP_W, GROUP_W), 1) // HEAD_DIM
    return jnp.where(r == c, 1.0 / HEAD_DIM, 0.0).astype(BF16)


def _branch_a(au, av, az, ws, bias):
    t = av.shape[0]
    sq = av * av
    hi = sq.astype(BF16)
    lo = (sq - hi.astype(F32)).astype(BF16)
    gm = _group_mean_matrix()
    ms = _dot(hi, gm) + _dot(lo, gm)
    vh = (av * lax.rsqrt(ms + EPS)).astype(BF16)
    head = lax.broadcasted_iota(jnp.int32, (CHUNK, GROUP_W), 1) // HEAD_DIM
    outs = []
    for n in range(t // CHUNK):
        v = vh[n * CHUNK:(n + 1) * CHUNK]
        rhs = jnp.concatenate([jnp.where(head == h, v, jnp.zeros_like(v)) for h in range(A_HEADS)], axis=0)
        outs.append(_dot(ws, rhs) + bias)
    mixed = jnp.concatenate(outs, axis=0)
    return au * mixed * _silu(az)


def _branch_b(bb, bc, bh, bz, wc, prev_row, next_row):
    t = bb.shape[0]
    xc = bc * bh
    row = lax.broadcasted_iota(jnp.int32, xc.shape, 0)
    xm = jnp.where(row == 0, prev_row, pltpu.roll(xc, 1, axis=0))
    xp = jnp.where(row == t - 1, next_row, pltpu.roll(xc, t - 1, axis=0))
    y = wc[0:1, :] * xm + wc[1:2, :] * xc + wc[2:3, :] * xp
    return bb * y * _silu(bz)


def _attend(score, value, pos, sink=None):
    m = jnp.max(score, axis=-1, keepdims=True)
    if sink is not None:
        m = jnp.maximum(m, sink * LOG2E)
    acc = _dot(jnp.exp2(score - m).astype(BF16), value)
    o = acc[:, pos * LANES:(pos + 1) * LANES]
    l = acc[:, (1 - pos) * LANES:(2 - pos) * LANES]
    if sink is not None:
        l = l + jnp.exp2(sink * LOG2E - m)
    return o * (1.0 / l)


def _context_pass(q_stacked, k, value):
    s = _dot_nt(q_stacked, k)
    m = jnp.max(s, axis=-1, keepdims=True)
    return m, _dot(jnp.exp2(s - m).astype(BF16), value)


def _local_softmax(s_loc, m_ctx, sink=None):
    m = jnp.maximum(jnp.max(s_loc, axis=-1, keepdims=True), m_ctx)
    if sink is not None:
        m = jnp.maximum(m, sink)
    e = jnp.exp2(s_loc - m).astype(BF16)
    return e, jnp.exp2(m_ctx - m), (None if sink is None else jnp.exp2(sink - m))


def _local_output(e, values, acc_ctx, ctx_scale, sink_term=None):
    rows = e.shape[0] // len(values)
    outs = []
    for p, (v, pos) in enumerate(values):
        r = slice(p * rows, (p + 1) * rows)
        acc = _dot(e[r], v) + acc_ctx[r] * ctx_scale[r]
        o = acc[:, pos * LANES:(pos + 1) * LANES]
        l = acc[:, (1 - pos) * LANES:(2 - pos) * LANES]
        if sink_term is not None:
            l = l + sink_term[r]
        outs.append(o * (1.0 / l))
    return outs


def _skewed(n_steps, stages):
    for tick in range(n_steps + len(stages) - 1):
        for lag, stage in enumerate(stages):
            if 0 <= tick - lag < n_steps:
                stage(tick - lag)


def _low_half(m):
    return lax.broadcasted_iota(jnp.int32, (m, LANES), 1) < HEAD_DIM


def _mask_half(q, low, half):
    z = jnp.zeros_like(q)
    return jnp.where(low, q, z) if half == 0 else jnp.where(low, z, q)


def _out_proj(x, gate, ya, yb, yc, yd, w_out_ref):
    y = _dot(ya.astype(BF16), w_out_ref[0:256, :])
    y += _dot(yb.astype(BF16), w_out_ref[256:512, :])
    y += _dot(yc.astype(BF16), w_out_ref[512:768, :])
    y += _dot(yd.astype(BF16), w_out_ref[768:1024, :])
    return x + gate * y


def _mod_kernel(cond_ref, w_ref, b_ref, o_ref):
    mod = _dot(_silu(cond_ref[...]).astype(BF16), w_ref[...].astype(BF16)) + b_ref[...]
    for j in range(3):
        o_ref[j] = mod[:, j * D_MODEL:(j + 1) * D_MODEL]


def _modulation(cond, w_mod, b_mod):
    return pl.pallas_call(
        _mod_kernel,
        grid=(DEPTH,),
        in_specs=[
            pl.BlockSpec((N_GROUPS, D_MODEL), lambda l: (0, 0)),
            pl.BlockSpec((None, D_MODEL, 3 * D_MODEL), lambda l: (l, 0, 0)),
            pl.BlockSpec((None, 1, 3 * D_MODEL), lambda l: (l, 0, 0)),
        ],
        out_specs=pl.BlockSpec((None, 3, N_GROUPS, D_MODEL), lambda l: (l, 0, 0, 0)),
        out_shape=jax.ShapeDtypeStruct((DEPTH, 3, N_GROUPS, D_MODEL), F32),
        compiler_params=pltpu.CompilerParams(dimension_semantics=("arbitrary",),
                                             vmem_limit_bytes=VMEM_LIMIT),
        name="modulation",
    )(cond, w_mod, b_mod.reshape(DEPTH, 1, 3 * D_MODEL))


def _nbr_table_kernel(rpb_ref, o_ref):
    c = lax.broadcasted_iota(jnp.int32, (GRID_W, LANES), 0)
    lane = lax.broadcasted_iota(jnp.int32, (GRID_W, LANES), 1)
    cc = lane % GRID_W
    cstart = jnp.clip(c - WIN_C // 2, 0, GRID_W - WIN_C)
    low = lane < GRID_W
    for h in range(4):
        t_low = []
        t_high = []
        for ro in range(2 * WIN_R - 1):
            x = jnp.broadcast_to(rpb_ref[h, ro:ro + 1, :] * LOG2E, (GRID_W, LANES))
            t_low.append(pltpu.roll(x, LANES - (WIN_C - 1), 1, stride=1, stride_axis=0))
            t_high.append(pltpu.roll(x, GRID_W - (WIN_C - 1), 1, stride=1, stride_axis=0))
        for d in range(WIN_R):
            for p in range(WIN_R // 2):
                v = jnp.where(low, t_low[d + 2 * p], t_high[d + 2 * p + 1])
                v = jnp.where(cc >= cstart, jnp.where(cc < cstart + WIN_C, v, NEG), NEG)
                o_ref[h, d, :, p * LANES:(p + 1) * LANES] = v


def _nbr_tables(rpb):
    rpb_p = jnp.pad(rpb, ((0, 0), (0, 0), (0, 1), (0, LANES - (2 * WIN_C - 1))))
    return pl.pallas_call(
        _nbr_table_kernel,
        grid=(DEPTH,),
        in_specs=[pl.BlockSpec((None, 4, 2 * WIN_R, LANES), lambda l: (l, 0, 0, 0))],
        out_specs=pl.BlockSpec((None, 4, WIN_R, GRID_W, WIN_R * GRID_W), lambda l: (l, 0, 0, 0, 0)),
        out_shape=jax.ShapeDtypeStruct((DEPTH, 4, WIN_R, GRID_W, WIN_R * GRID_W), F32),
        compiler_params=pltpu.CompilerParams(dimension_semantics=("arbitrary",)),
        name="nbr_tables",
    )(rpb_p)


def _ctx_kernel(sink_ref, x_ref, mod_ref, g_ref, fg_ref, w_in_ref, w_out_ref, ws_ref, ba_ref, wc_ref,
                y_ref, ck_ref, cv_ref, dk_ref, dv_ref, xs):
    l = pl.program_id(0)
    b = pl.program_id(1)

    @pl.when(l == 0)
    def _():
        xs[b] = x_ref[...]

    x = xs[b]
    h = _norm_mod(x, g_ref[...], mod_ref[0, 0:1, :], mod_ref[1, 0:1, :]).astype(BF16)
    p = _dot(h, w_in_ref[...])
    kcf = p[:, CKV:CKV + 128]
    vcf = p[:, CKV + 128:CKV + 256]
    kdf = p[:, DK:DK + 256]
    vdf = p[:, DV:DV + 256]
    ck_ref[...] = kcf
    cv_ref[...] = vcf
    dk_ref[...] = kdf
    dv_ref[...] = vdf

    ya = _branch_a(p[:, AU:AU + 256], p[:, AV:AV + 256], p[:, AZ:AZ + 256], ws_ref[...], ba_ref[...])
    zero_row = jnp.zeros((1, GROUP_W), F32)
    yb = _branch_b(p[:, BB:BB + 256], p[:, BC:BC + 256], p[:, BH:BH + 256], p[:, BZ:BZ + 256],
                   wc_ref[...], zero_row, zero_row)

    low = _low_half(SEQ)

    ones = jnp.ones((SEQ, LANES), BF16)
    kc = (kcf.astype(BF16), _swap64(kcf).astype(BF16))
    vc = (jnp.concatenate([vcf.astype(BF16), ones], axis=1),
          jnp.concatenate([ones, _swap64(vcf).astype(BF16)], axis=1))
    o = []
    for t in range(2):
        q = (p[:, CQ + t * LANES:CQ + (t + 1) * LANES] * Q_SCALE).astype(BF16)
        for half in range(2):
            sw = (t + half) % 2
            o.append(_attend(_dot_nt(_mask_half(q, low, half), kc[sw]), vc[sw], sw, sink_ref[l, 2 * t + half]))
    yc = jnp.concatenate([jnp.where(low, o[0], o[1]), jnp.where(low, o[2], o[3])], axis=1)
    yc = yc * _silu(p[:, CZ:CZ + 256])

    o = []
    for t in range(2):
        q = (p[:, DQ + t * LANES:DQ + (t + 1) * LANES] * Q_SCALE).astype(BF16)
        k = kdf[:, t * LANES:(t + 1) * LANES].astype(BF16)
        v = jnp.concatenate([vdf[:, t * LANES:(t + 1) * LANES].astype(BF16), ones], axis=1)
        for half in range(2):
            o.append(_attend(_dot_nt(_mask_half(q, low, half), k), v, 0))
    yd = jnp.concatenate([jnp.where(low, o[0], o[1]), jnp.where(low, o[2], o[3])], axis=1)
    yd = yd * _silu(p[:, DZ:DZ + 256])

    x_new = _out_proj(x, mod_ref[2, 0:1, :], ya, yb, yc, yd, w_out_ref)
    xs[b] = x_new

    @pl.when(l == DEPTH - 1)
    def _():
        y_ref[...] = _rms(x_new) * fg_ref[...]


def _ctx_layers(x, mod, norm_g, final_g, w_in, w_out, ws_cat, bias_a, w_conv, sink):
    per_layer = lambda *shape: pl.BlockSpec((None,) + shape, lambda l, b: (l,) + (0,) * len(shape))
    state = lambda w: pl.BlockSpec((None, None, SEQ, w), lambda l, b: (b, l, 0, 0))
    return pl.pallas_call(
        _ctx_kernel,
        grid=(DEPTH, BATCH),
        in_specs=[
            pl.BlockSpec(memory_space=pltpu.SMEM),
            pl.BlockSpec((None, SEQ, D_MODEL), lambda l, b: (jnp.where(l == 0, b, BATCH - 1), 0, 0)),
            per_layer(3, N_GROUPS, D_MODEL),
            per_layer(1, D_MODEL),
            pl.BlockSpec((1, D_MODEL), lambda l, b: (0, 0)),
            pl.BlockSpec((None, D_MODEL, D_IN), lambda l, b: (l, 0, 0), pipeline_mode=pl.Buffered(1)),
            pl.BlockSpec((None, D_MODEL, D_MODEL), lambda l, b: (l, 0, 0), pipeline_mode=pl.Buffered(1)),
            per_layer(CHUNK, A_HEADS * CHUNK),
            per_layer(CHUNK, GROUP_W),
            per_layer(3, GROUP_W),
        ],
        out_specs=[
            pl.BlockSpec((None, SEQ, D_MODEL), lambda l, b: (jnp.where(l == DEPTH - 1, b, 0), 0, 0)),
            state(128), state(128), state(256), state(256),
        ],
        out_shape=[
            jax.ShapeDtypeStruct((BATCH, SEQ, D_MODEL), F32),
            jax.ShapeDtypeStruct((BATCH, DEPTH, SEQ, 128), F32),
            jax.ShapeDtypeStruct((BATCH, DEPTH, SEQ, 128), F32),
            jax.ShapeDtypeStruct((BATCH, DEPTH, SEQ, 256), F32),
            jax.ShapeDtypeStruct((BATCH, DEPTH, SEQ, 256), F32),
        ],
        scratch_shapes=[pltpu.VMEM((BATCH, SEQ, D_MODEL), F32)],
        compiler_params=pltpu.CompilerParams(dimension_semantics=("arbitrary", "arbitrary"),
                                             vmem_limit_bytes=VMEM_LIMIT),
        name="ctx_layers",
    )(sink, x, mod, norm_g, final_g, w_in, w_out, ws_cat, bias_a, w_conv)


def _rope(x, cos, sin_signed, first_half):
    swapped = jnp.where(first_half, pltpu.roll(x, LANES - 16, axis=1), pltpu.roll(x, 16, axis=1))
    return x * cos + swapped * sin_signed


def _lat_proj_kernel(x_ref, xp_ref, xn_ref, mod_ref, g_ref, w_ref, cos_ref, sin_ref, ws_ref, ba_ref, wc_ref,
                     yab_ref, sz_ref, qc_ref, kc_ref, vc_ref, qd_ref, kd_ref, vd_ref):
    i = pl.program_id(1)
    grp = pl.ds(1 + pl.program_id(0), 1)
    g = g_ref[...]
    shift = mod_ref[0, grp, :]
    scale = mod_ref[1, grp, :]
    h = _norm_mod(x_ref[...], g, shift, scale).astype(BF16)

    pa = _dot(h, w_ref[:, AU:AU + 3 * GROUP_W])
    ya = _branch_a(pa[:, 0:256], pa[:, 256:512], pa[:, 512:768], ws_ref[...], ba_ref[...])
    yab_ref[:, 0:GROUP_W] = ya.astype(BF16)

    pb = _dot(h, w_ref[:, BB:BB + 4 * GROUP_W])
    x_edge = jnp.concatenate([xp_ref[...], xn_ref[...]], axis=0)
    e = _dot(_norm_mod(x_edge, g, shift, scale).astype(BF16), w_ref[:, BC:BC + 2 * GROUP_W])
    prev_row = jnp.where(i > 0, e[7:8, 0:256] * e[7:8, 256:512], 0.0)
    next_row = jnp.where(i < N_TILES - 1, e[8:9, 0:256] * e[8:9, 256:512], 0.0)
    yb = _branch_b(pb[:, 0:256], pb[:, 256:512], pb[:, 512:768], pb[:, 768:1024], wc_ref[...],
                   prev_row, next_row)
    yab_ref[:, GROUP_W:2 * GROUP_W] = yb.astype(BF16)

    sz_ref[:, 0:256] = _silu(_dot(h, w_ref[:, CZ:CZ + 256]))
    sz_ref[:, 256:512] = _silu(_dot(h, w_ref[:, DZ:DZ + 256]))
    cos = cos_ref[...]
    sin = sin_ref[...]
    first_half = (lax.broadcasted_iota(jnp.int32, (TILE, LANES), 1) % 32) < 16
    qc = _dot(h, w_ref[:, CQ:CQ + 256])
    for t in range(2):
        qt = _rope(qc[:, t * LANES:(t + 1) * LANES], cos, sin, first_half) * Q_SCALE
        qc_ref[:, t * LANES:(t + 1) * LANES] = qt.astype(BF16)
    kv = _dot(h, w_ref[:, CKV:CKV + 256])
    k = _rope(kv[:, 0:LANES], cos, sin, first_half)
    v = kv[:, LANES:2 * LANES]
    kc_ref[:, 0:LANES] = k.astype(BF16)
    kc_ref[:, LANES:2 * LANES] = _swap64(k).astype(BF16)
    vc_ref[:, 0:LANES] = v.astype(BF16)
    vc_ref[:, LANES:2 * LANES] = _swap64(v).astype(BF16)
    qd_ref[...] = (_dot(h, w_ref[:, DQ:DQ + 256]) * Q_SCALE).astype(BF16)
    kd_ref[...] = _dot(h, w_ref[:, DK:DK + 256]).astype(BF16)
    vd_ref[...] = _dot(h, w_ref[:, DV:DV + 256]).astype(BF16)


def _lat_proj(layer, x, mod, norm_g, w_in, cos_t, sin_t, ws_cat, bias_a, w_conv):
    tile = lambda w: pl.BlockSpec((None, TILE, w), lambda b, i: (b, i, 0))
    per_layer = lambda *shape: pl.BlockSpec((None,) + shape, lambda b, i: (layer,) + (0,) * len(shape))
    sds = lambda w, dt: jax.ShapeDtypeStruct((DEC_BATCH, DEC_SEQ, w), dt)
    edge = TILE // 8
    last = DEC_SEQ // 8 - 1
    return pl.pallas_call(
        _lat_proj_kernel,
        grid=(DEC_BATCH, N_TILES),
        in_specs=[
            tile(D_MODEL),
            pl.BlockSpec((None, 8, D_MODEL), lambda b, i: (b, jnp.maximum(i * edge - 1, 0), 0)),
            pl.BlockSpec((None, 8, D_MODEL), lambda b, i: (b, jnp.minimum((i + 1) * edge, last), 0)),
            per_layer(3, N_GROUPS, D_MODEL),
            per_layer(1, D_MODEL),
            per_layer(D_MODEL, D_IN),
            pl.BlockSpec((TILE, LANES), lambda b, i: (i, 0)),
            pl.BlockSpec((TILE, LANES), lambda b, i: (i, 0)),
            per_layer(CHUNK, A_HEADS * CHUNK),
            per_layer(CHUNK, GROUP_W),
            per_layer(3, GROUP_W),
        ],
        out_specs=[tile(512), tile(512)] + [tile(256)] * 6,
        out_shape=[sds(512, BF16), sds(512, F32)] + [sds(256, BF16)] * 6,
        compiler_params=pltpu.CompilerParams(dimension_semantics=("arbitrary", "arbitrary"),
                                             vmem_limit_bytes=VMEM_LIMIT),
        name="lat_proj",
    )(x, x, x, mod, norm_g, w_in, cos_t, sin_t, ws_cat, bias_a, w_conv)


_C_GROUPS = (((0, 0), (1, 1)), ((0, 1), (1, 0)))
_D_GROUPS = (((0, 0), (0, 1)), ((1, 0), (1, 1)))


def _stacked_queries(q_ref, rows, low, groups):
    return [jnp.concatenate([_mask_half(q_ref[rows, t * LANES:(t + 1) * LANES], low, half)
                             for t, half in group], axis=0) for group in groups]


def _lat_mix_kernel(layer, sink_ref, x_ref, mod_ref, fg_ref, yab_ref, sz_ref,
                    qc_ref, kc_ref, kcp_ref, kcn_ref, vc_ref, vcp_ref, vcn_ref,
                    qd_ref, kd_ref, kdp_ref, kdn_ref, vd_ref, vdp_ref, vdn_ref,
                    cck_ref, ccv_ref, cdk_ref, cdv_ref, cmask_ref, tab_ref, w_out_ref,
                    xo_ref,
                    kce, vce, kde, vde, yc_scr, yd_scr, m_ctx, acc_ctx, *stage_bufs):
    b = pl.program_id(0)
    i = pl.program_id(1)
    s_c, s_d = stage_bufs[0:2], stage_bufs[2:4]

    def file_context(m, acc, group, step_rows, n_steps):
        for which in range(2):
            for k in range(n_steps):
                src = slice(which * TILE + k * step_rows, which * TILE + (k + 1) * step_rows)
                dst = slice((4 * k + 2 * group + which) * step_rows, (4 * k + 2 * group + which + 1) * step_rows)
                m_ctx[dst, :] = m[src]
                acc_ctx[dst, :] = acc[src]

    kce[0:QBLK, :] = kcp_ref[...]
    kce[QBLK:QBLK + TILE, :] = kc_ref[...]
    kce[QBLK + TILE:, :] = kcn_ref[...]
    for c_src, c_dst in ((0, 0), (LANES, 2 * LANES)):
        vce[0:QBLK, c_dst:c_dst + LANES] = vcp_ref[:, c_src:c_src + LANES]
        vce[QBLK:QBLK + TILE, c_dst:c_dst + LANES] = vc_ref[:, c_src:c_src + LANES]
        vce[QBLK + TILE:, c_dst:c_dst + LANES] = vcn_ref[:, c_src:c_src + LANES]
    vce[:, LANES:2 * LANES] = jnp.ones((TILE + 2 * QBLK, LANES), BF16)
    cck = cck_ref[...]
    ccv = ccv_ref[...]
    ones_ctx = jnp.ones((PAST_LEN, LANES), BF16)
    k_ctx = (cck.astype(BF16), _swap64(cck).astype(BF16))
    v_ctx = jnp.concatenate([ccv.astype(BF16), ones_ctx, _swap64(ccv).astype(BF16)], axis=1)
    low_q = _low_half(QBLK)
    low_t = _low_half(TILE)
    n_blocks = DEC_SEQ // QBLK
    blocks_per_tile = TILE // QBLK

    for sw, q in enumerate(_stacked_queries(qc_ref, slice(None), low_t, _C_GROUPS)):
        m, acc = _context_pass(q, k_ctx[sw], v_ctx[:, sw * LANES:(sw + 2) * LANES])
        file_context(m, acc, sw, QBLK, blocks_per_tile)

    def c_scores(j):
        n = i * blocks_per_tile + j
        window = cmask_ref[jnp.where(n == 0, 1, jnp.where(n == n_blocks - 1, 2, 0))]
        window = jnp.concatenate([window, window], axis=0)
        rows = slice(j * QBLK, (j + 1) * QBLK)
        for sw, q in enumerate(_stacked_queries(qc_ref, rows, low_q, _C_GROUPS)):
            k_loc = kce[j * QBLK:(j + 3) * QBLK, sw * LANES:(sw + 1) * LANES]
            s_c[j % 2][2 * sw * QBLK:2 * (sw + 1) * QBLK, :] = _dot_nt(q, k_loc) + window

    def c_softmax(j):
        step = slice(4 * j * QBLK, 4 * (j + 1) * QBLK)
        sink = jnp.concatenate([jnp.full((QBLK, 1), sink_ref[layer, 2 * t + half] * LOG2E, F32)
                                for group in _C_GROUPS for t, half in group], axis=0)
        e, ctx_scale, sink_term = _local_softmax(s_c[j % 2][...], m_ctx[step, :], sink)
        values = [(vce[j * QBLK:(j + 3) * QBLK, sw * LANES:(sw + 2) * LANES], sw) for sw in range(2)]
        r0, r1 = _local_output(e, values, acc_ctx[step, :], ctx_scale, sink_term)
        yc_scr[j * QBLK:(j + 1) * QBLK, :] = jnp.concatenate(
            [jnp.where(low_q, r0[0:QBLK], r1[0:QBLK]), jnp.where(low_q, r1[QBLK:], r0[QBLK:])], axis=1)

    _skewed(blocks_per_tile, (c_scores, c_softmax))

    halo = 4 * GRID_W
    kde[0:halo, :] = kdp_ref[...]
    kde[halo:halo + TILE, :] = kd_ref[...]
    kde[halo + TILE:, :] = kdn_ref[...]
    for c_src, c_dst in ((0, 0), (LANES, 2 * LANES)):
        vde[0:halo, c_dst:c_dst + LANES] = vdp_ref[:, c_src:c_src + LANES]
        vde[halo:halo + TILE, c_dst:c_dst + LANES] = vd_ref[:, c_src:c_src + LANES]
        vde[halo + TILE:, c_dst:c_dst + LANES] = vdn_ref[:, c_src:c_src + LANES]
    vde[:, LANES:2 * LANES] = jnp.ones((TILE + 2 * halo, LANES), BF16)
    kd_ctx = cdk_ref[...].astype(BF16)
    cdv = cdv_ref[...].astype(BF16)
    vd_ctx = jnp.concatenate([cdv[:, 0:LANES], ones_ctx, cdv[:, LANES:2 * LANES]], axis=1)
    low_r = _low_half(GRID_W)
    n_rows = DEC_SEQ // GRID_W

    for t, q in enumerate(_stacked_queries(qd_ref, slice(None), low_t, _D_GROUPS)):
        m, acc = _context_pass(q, kd_ctx[:, t * LANES:(t + 1) * LANES], vd_ctx[:, t * LANES:(t + 2) * LANES])
        file_context(m, acc, t, GRID_W, ROWS_PER_TILE)

    def window_start(rl):
        r = i * ROWS_PER_TILE + rl
        rs = jnp.clip(r - WIN_R // 2, 0, n_rows - WIN_R)
        return rs - r + (WIN_R - 1), pl.multiple_of((rs - i * ROWS_PER_TILE + 4) * GRID_W, GRID_W)

    def d_scores(rl):
        didx, off = window_start(rl)
        rows = slice(rl * GRID_W, (rl + 1) * GRID_W)
        for t, q in enumerate(_stacked_queries(qd_ref, rows, low_r, _D_GROUPS)):
            k_loc = kde[pl.ds(off, WIN_R * GRID_W), t * LANES:(t + 1) * LANES]
            bias = jnp.concatenate([tab_ref[2 * t, didx], tab_ref[2 * t + 1, didx]], axis=0)
            s_d[rl % 2][2 * t * GRID_W:2 * (t + 1) * GRID_W, :] = _dot_nt(q, k_loc) + bias

    def d_softmax(rl):
        _, off = window_start(rl)
        step = slice(4 * rl * GRID_W, 4 * (rl + 1) * GRID_W)
        e, ctx_scale, _ = _local_softmax(s_d[rl % 2][...], m_ctx[step, :])
        values = [(vde[pl.ds(off, WIN_R * GRID_W), t * LANES:(t + 2) * LANES], t) for t in range(2)]
        r = _local_output(e, values, acc_ctx[step, :], ctx_scale)
        yd_scr[rl * GRID_W:(rl + 1) * GRID_W, :] = jnp.concatenate(
            [jnp.where(low_r, r[t][0:GRID_W], r[t][GRID_W:]) for t in range(2)], axis=1)

    _skewed(ROWS_PER_TILE, (d_scores, d_softmax))

    yc = (yc_scr[...] * sz_ref[:, 0:256]).astype(BF16)
    yd = (yd_scr[...] * sz_ref[:, 256:512]).astype(BF16)
    y = (_dot(yab_ref[...], w_out_ref[0:2 * GROUP_W, :]) + _dot(yc, w_out_ref[2 * GROUP_W:3 * GROUP_W, :])
         + _dot(yd, w_out_ref[3 * GROUP_W:4 * GROUP_W, :]))
    x_new = x_ref[...] + mod_ref[2, pl.ds(1 + b, 1), :] * y
    if layer == DEPTH - 1:
        x_new = _rms(x_new) * fg_ref[...]
    xo_ref[...] = x_new


def _lat_mix(layer, x, mod, final_g, yab, sz, qc, kc, vc, qd, kd, vd, cck, ccv, cdk, cdv, tab, w_out, sink):
    tile = lambda w: pl.BlockSpec((None, TILE, w), lambda b, i: (b, i, 0))
    per_layer = lambda *shape: pl.BlockSpec((None,) + shape, lambda b, i: (layer,) + (0,) * len(shape))

    def halo(rows, w):
        per = TILE // rows
        last = DEC_SEQ // rows - 1
        prev = pl.BlockSpec((None, rows, w), lambda b, i: (b, jnp.maximum(i * per - 1, 0), 0))
        nxt = pl.BlockSpec((None, rows, w), lambda b, i: (b, jnp.minimum((i + 1) * per, last), 0))
        return prev, nxt

    cprev, cnext = halo(QBLK, 256)
    dprev, dnext = halo(4 * GRID_W, 256)
    cache = lambda w: pl.BlockSpec((None, None, PAST_LEN, w), lambda b, i: (b, layer, 0, 0))
    return pl.pallas_call(
        functools.partial(_lat_mix_kernel, layer),
        grid=(DEC_BATCH, N_TILES),
        in_specs=[
            pl.BlockSpec(memory_space=pltpu.SMEM),
            tile(D_MODEL),
            per_layer(3, N_GROUPS, D_MODEL),
            pl.BlockSpec((1, D_MODEL), lambda b, i: (0, 0)),
            tile(512), tile(512),
            tile(256), tile(256), cprev, cnext, tile(256), cprev, cnext,
            tile(256), tile(256), dprev, dnext, tile(256), dprev, dnext,
            cache(128), cache(128), cache(256), cache(256),
            pl.BlockSpec((3, QBLK, 3 * QBLK), lambda b, i: (0, 0, 0)),
            per_layer(4, WIN_R, GRID_W, WIN_R * GRID_W),
            per_layer(D_MODEL, D_MODEL),
        ],
        out_specs=tile(D_MODEL),
        out_shape=jax.ShapeDtypeStruct((DEC_BATCH, DEC_SEQ, D_MODEL), F32),
        scratch_shapes=[
            pltpu.VMEM((TILE + 2 * QBLK, 2 * LANES), BF16),
            pltpu.VMEM((TILE + 2 * QBLK, 3 * LANES), BF16),
            pltpu.VMEM((TILE + 8 * GRID_W, 2 * LANES), BF16),
            pltpu.VMEM((TILE + 8 * GRID_W, 3 * LANES), BF16),
            pltpu.VMEM((TILE, 256), F32),
            pltpu.VMEM((TILE, 256), F32),
            pltpu.VMEM((4 * TILE, 1), F32),
            pltpu.VMEM((4 * TILE, 2 * LANES), F32),
        ] + 2 * [pltpu.VMEM((4 * QBLK, 3 * QBLK), F32)] + 2 * [pltpu.VMEM((4 * GRID_W, WIN_R * GRID_W), F32)],
        compiler_params=pltpu.CompilerParams(dimension_semantics=("arbitrary", "arbitrary"),
                                             vmem_limit_bytes=VMEM_LIMIT),
        name="lat_mix",
    )(sink, x, mod, final_g, yab, sz, qc, kc, kc, kc, vc, vc, vc,
      qd, kd, kd, kd, vd, vd, vd, cck, ccv, cdk, cdv, _window_masks(), tab, w_out)


def _window_masks():
    p = np.arange(QBLK)[:, None]
    j = np.arange(3 * QBLK)[None, :]
    band = np.abs(j - QBLK - p) <= WINDOW
    masks = [band, band & (j >= QBLK), band & (j < 2 * QBLK)]
    return jnp.asarray(np.where(np.stack(masks), 0.0, NEG), F32)


def _rope_tables():
    t = np.arange(DEC_SEQ)
    freqs = (np.float32(ROPE_BASE) ** (-np.arange(16, dtype=np.float32) / np.float32(16))).astype(np.float32)
    ang_r = (t // GRID_W).astype(np.float32)[:, None] * freqs
    ang_c = (t % GRID_W).astype(np.float32)[:, None] * freqs
    cos_h = np.concatenate([np.cos(ang_r), np.cos(ang_r), np.cos(ang_c), np.cos(ang_c)], axis=1)
    sin_h = np.concatenate([-np.sin(ang_r), np.sin(ang_r), -np.sin(ang_c), np.sin(ang_c)], axis=1)
    return (jnp.asarray(np.tile(cos_h, (1, 2)), F32), jnp.asarray(np.tile(sin_h, (1, 2)), F32))


def kernel(x_prompt, x_sample, cache_c_k, cache_c_v, cache_d_k, cache_d_v, c, c_ctx, norm_g, w_mod, b_mod,
           w_in, w_out, w_s, b_s, w_conv, sink, rpb, final_g):
    w_in_b = w_in.astype(BF16)
    w_out_b = w_out.astype(BF16)
    ws_cat = jnp.transpose(w_s, (0, 2, 1, 3)).reshape(DEPTH, CHUNK, A_HEADS * CHUNK).astype(BF16)
    bias_a = jnp.repeat(jnp.transpose(b_s, (0, 2, 1)), HEAD_DIM, axis=2)
    norm_g3 = norm_g.reshape(DEPTH, 1, D_MODEL)
    fg = final_g.reshape(1, D_MODEL)
    cos_t, sin_t = _rope_tables()
    cck = cache_c_k.reshape(DEC_BATCH, DEPTH, PAST_LEN, 128)
    ccv = cache_c_v.reshape(DEC_BATCH, DEPTH, PAST_LEN, 128)
    cdk = cache_d_k.reshape(DEC_BATCH, DEPTH, PAST_LEN, 256)
    cdv = cache_d_v.reshape(DEC_BATCH, DEPTH, PAST_LEN, 256)

    cond = jnp.concatenate([c_ctx[None, :], c, jnp.zeros((N_GROUPS - 1 - DEC_BATCH, D_MODEL), F32)], axis=0)
    mod = _modulation(cond, w_mod, b_mod)
    tab = _nbr_tables(rpb)

    y_prompt, s_ck, s_cv, s_dk, s_dv = _ctx_layers(x_prompt, mod, norm_g3, fg, w_in_b, w_out_b, ws_cat,
                                                   bias_a, w_conv, sink)
    xs = x_sample
    for l in range(DEPTH):
        yab, sz, qc, kc, vc, qd, kd, vd = _lat_proj(l, xs, mod, norm_g3, w_in_b, cos_t, sin_t,
                                                    ws_cat, bias_a, w_conv)
        xs = _lat_mix(l, xs, mod, fg, yab, sz, qc, kc, vc, qd, kd, vd, cck, ccv, cdk, cdv, tab, w_out_b, sink)

    shape_c = (BATCH, DEPTH, SEQ, 2, HEAD_DIM)
    shape_d = (BATCH, DEPTH, SEQ, 4, HEAD_DIM)
    return (y_prompt, xs, s_ck.reshape(shape_c), s_cv.reshape(shape_c), s_dk.reshape(shape_d),
            s_dv.reshape(shape_d))
```

```python
import functools

import numpy as np
import jax
import jax.numpy as jnp
from jax import lax
from jax.experimental import pallas as pl
from jax.experimental.pallas import tpu as pltpu

D_MODEL = 1024
BATCH = 16
SEQ = 256
DEPTH = 4
DEC_BATCH = 2
DEC_SEQ = 4096
PAST_LEN = 512
GRID_W = 64
GROUP_W = 256
HEAD_DIM = 64
A_HEADS = 4
CHUNK = 128
WINDOW = 128
QBLK = 128
WIN_R = 8
WIN_C = 16
ROPE_BASE = 10000.0
EPS = 1e-6
NEG = -1e30
D_IN = 3584
LOG2E = 1.4426950408889634
Q_SCALE = HEAD_DIM ** -0.5 * LOG2E

AU, AV, AZ, BB, BC, BH, BZ = 0, 256, 512, 768, 1024, 1280, 1536
CQ, CKV, CZ, DQ, DK, DV, DZ = 1792, 2048, 2304, 2560, 2816, 3072, 3328
N_AB = 1792

LANES = 128
TILE = 512
ROWS_PER_TILE = TILE // GRID_W
N_TILES = DEC_SEQ // TILE
N_GROUPS = 8
VMEM_LIMIT = 56 * 1024 * 1024

F32 = jnp.float32
BF16 = jnp.bfloat16


def _silu(z):
    return z * (1.0 / (1.0 + jnp.exp(-z)))


def _dot(a, b):
    return jnp.dot(a, b, preferred_element_type=F32)


def _dot_nt(a, b):
    return lax.dot_general(a, b, (((1,), (1,)), ((), ())), preferred_element_type=F32)


def _rms(x):
    return x * lax.rsqrt(jnp.mean(x * x, axis=-1, keepdims=True) + EPS)


def _norm_mod(x, g, shift, scale):
    return _rms(x) * (g * (1.0 + scale)) + shift


def _swap64(x):
    return pltpu.roll(x, HEAD_DIM, axis=1)


def _group_mean_matrix():
    r = lax.broadcasted_iota(jnp.int32, (GROUP_W, GROUP_W), 0) // HEAD_DIM
    c = lax.broadcasted_iota(jnp.int32, (GROUP_W, GROUP_W), 1) // HEAD_DIM
    return jnp.where(r == c, 1.0 / HEAD_DIM, 0.0).astype(BF16)


def _branch_a(au, av, az, ws, bias):
    t = av.shape[0]
    sq = av * av
    hi = sq.astype(BF16)
    lo = (sq - hi.astype(F32)).astype(BF16)
    gm = _group_mean_matrix()
    ms = _dot(hi, gm) + _dot(lo, gm)
    vh = (av * lax.rsqrt(ms + EPS)).astype(BF16)
    head = lax.broadcasted_iota(jnp.int32, (CHUNK, GROUP_W), 1) // HEAD_DIM
    outs = []
    for n in range(t // CHUNK):
        v = vh[n * CHUNK:(n + 1) * CHUNK]
        rhs = jnp.concatenate([jnp.where(head == h, v, jnp.zeros_like(v)) for h in range(A_HEADS)], axis=0)
        outs.append(_dot(ws, rhs) + bias)
    mixed = jnp.concatenate(outs, axis=0)
    return au * mixed * _silu(az)


def _branch_b(bb, bc, bh, bz, wc, prev_row, next_row):
    t = bb.shape[0]
    xc = bc * bh
    row = lax.broadcasted_iota(jnp.int32, xc.shape, 0)
    xm = jnp.where(row == 0, prev_row, pltpu.roll(xc, 1, axis=0))
    xp = jnp.where(row == t - 1, next_row, pltpu.roll(xc, t - 1, axis=0))
    y = wc[0:1, :] * xm + wc[1:2, :] * xc + wc[2:3, :] * xp
    return bb * y * _silu(bz)


def _attend(score, value, pos, sink=None):
    m = jnp.max(score, axis=-1, keepdims=True)
    if sink is not None:
        m = jnp.maximum(m, sink * LOG2E)
    acc = _dot(jnp.exp2(score - m).astype(BF16), value)
    o = acc[:, pos * LANES:(pos + 1) * LANES]
    l = acc[:, (1 - pos) * LANES:(2 - pos) * LANES]
    if sink is not None:
        l = l + jnp.exp2(sink * LOG2E - m)
    return o * (1.0 / l)


def _context_pass(q_stacked, k, value):
    s = _dot_nt(q_stacked, k)
    m = jnp.max(s, axis=-1, keepdims=True)
    return m, _dot(jnp.exp2(s - m).astype(BF16), value)


def _joint_max(s_loc, m_ctx, sink=None):
    m = jnp.maximum(jnp.max(s_loc, axis=-1, keepdims=True), m_ctx)
    return m if sink is None else jnp.maximum(m, sink)


def _local_softmax(s_loc, m, m_ctx, sink=None):
    e = jnp.exp2(s_loc - m).astype(BF16)
    return e, jnp.exp2(m_ctx - m), (None if sink is None else jnp.exp2(sink - m))


def _local_output(e, values, acc_ctx, ctx_scale, sink_term=None):
    rows = e.shape[0] // len(values)
    outs = []
    for p, (v, pos) in enumerate(values):
        r = slice(p * rows, (p + 1) * rows)
        acc = _dot(e[r], v) + acc_ctx[r] * ctx_scale[r]
        o = acc[:, pos * LANES:(pos + 1) * LANES]
        l = acc[:, (1 - pos) * LANES:(2 - pos) * LANES]
        if sink_term is not None:
            l = l + sink_term[r]
        outs.append(o * (1.0 / l))
    return outs


def _skewed(n_steps, stages):
    for tick in range(n_steps + len(stages) - 1):
        for lag, stage in enumerate(stages):
            if 0 <= tick - lag < n_steps:
                stage(tick - lag)


def _low_half(m):
    return lax.broadcasted_iota(jnp.int32, (m, LANES), 1) < HEAD_DIM


def _mask_half(q, low, half):
    z = jnp.zeros_like(q)
    return jnp.where(low, q, z) if half == 0 else jnp.where(low, z, q)


def _out_proj(x, gate, ya, yb, yc, yd, w_out_ref):
    y = _dot(ya.astype(BF16), w_out_ref[0:256, :])
    y += _dot(yb.astype(BF16), w_out_ref[256:512, :])
    y += _dot(yc.astype(BF16), w_out_ref[512:768, :])
    y += _dot(yd.astype(BF16), w_out_ref[768:1024, :])
    return x + gate * y


def _mod_kernel(cond_ref, w_ref, b_ref, o_ref):
    mod = _dot(_silu(cond_ref[...]).astype(BF16), w_ref[...].astype(BF16)) + b_ref[...]
    for j in range(3):
        o_ref[j] = mod[:, j * D_MODEL:(j + 1) * D_MODEL]


def _modulation(cond, w_mod, b_mod):
    return pl.pallas_call(
        _mod_kernel,
        grid=(DEPTH,),
        in_specs=[
            pl.BlockSpec((N_GROUPS, D_MODEL), lambda l: (0, 0)),
            pl.BlockSpec((None, D_MODEL, 3 * D_MODEL), lambda l: (l, 0, 0)),
            pl.BlockSpec((None, 1, 3 * D_MODEL), lambda l: (l, 0, 0)),
        ],
        out_specs=pl.BlockSpec((None, 3, N_GROUPS, D_MODEL), lambda l: (l, 0, 0, 0)),
        out_shape=jax.ShapeDtypeStruct((DEPTH, 3, N_GROUPS, D_MODEL), F32),
        compiler_params=pltpu.CompilerParams(dimension_semantics=("arbitrary",),
                                             vmem_limit_bytes=VMEM_LIMIT),
        name="modulation",
    )(cond, w_mod, b_mod.reshape(DEPTH, 1, 3 * D_MODEL))


def _nbr_table_kernel(rpb_ref, o_ref):
    c = lax.broadcasted_iota(jnp.int32, (GRID_W, LANES), 0)
    lane = lax.broadcasted_iota(jnp.int32, (GRID_W, LANES), 1)
    cc = lane % GRID_W
    cstart = jnp.clip(c - WIN_C // 2, 0, GRID_W - WIN_C)
    for h in range(4):
        tiles = [pltpu.roll(jnp.broadcast_to(rpb_ref[h, ro:ro + 1, :] * LOG2E, (GRID_W, LANES)),
                            LANES - (WIN_C - 1), 1, stride=1, stride_axis=0)
                 for ro in range(2 * WIN_R - 2)]
        for d in range(WIN_R):
            for p in range(WIN_R // 2):
                v = tiles[d + 2 * p]
                v = jnp.where(cc >= cstart, jnp.where(cc < cstart + WIN_C, v, NEG), NEG)
                o_ref[h, d, :, p * LANES:(p + 1) * LANES] = v


def _nbr_tables(rpb):
    half = LANES // 2 - (2 * WIN_C - 1)
    rows = jnp.pad(rpb, ((0, 0), (0, 0), (0, 1), (0, half)))
    rpb_p = jnp.concatenate([rows[:, :, :-1], rows[:, :, 1:]], axis=-1)
    rpb_p = jnp.pad(rpb_p, ((0, 0), (0, 0), (0, 1), (0, 0)))
    return pl.pallas_call(
        _nbr_table_kernel,
        grid=(DEPTH,),
        in_specs=[pl.BlockSpec((None, 4, 2 * WIN_R, LANES), lambda l: (l, 0, 0, 0))],
        out_specs=pl.BlockSpec((None, 4, WIN_R, GRID_W, WIN_R * GRID_W), lambda l: (l, 0, 0, 0, 0)),
        out_shape=jax.ShapeDtypeStruct((DEPTH, 4, WIN_R, GRID_W, WIN_R * GRID_W), F32),
        compiler_params=pltpu.CompilerParams(dimension_semantics=("arbitrary",)),
        name="nbr_tables",
    )(rpb_p)


def _ctx_kernel(sink_ref, x_ref, mod_ref, g_ref, fg_ref, w_in_ref, w_out_ref, ws_ref, ba_ref, wc_ref,
                y_ref, ck_ref, cv_ref, dk_ref, dv_ref, xs):
    l = pl.program_id(0)
    b = pl.program_id(1)

    @pl.when(l == 0)
    def _():
        xs[b] = x_ref[...]

    x = xs[b]
    h = _norm_mod(x, g_ref[...], mod_ref[0, 0:1, :], mod_ref[1, 0:1, :]).astype(BF16)
    p = _dot(h, w_in_ref[...])
    kcf = p[:, CKV:CKV + 128]
    vcf = p[:, CKV + 128:CKV + 256]
    kdf = p[:, DK:DK + 256]
    vdf = p[:, DV:DV + 256]
    ck_ref[...] = kcf
    cv_ref[...] = vcf
    dk_ref[...] = kdf
    dv_ref[...] = vdf

    ya = _branch_a(p[:, AU:AU + 256], p[:, AV:AV + 256], p[:, AZ:AZ + 256], ws_ref[...], ba_ref[...])
    zero_row = jnp.zeros((1, GROUP_W), F32)
    yb = _branch_b(p[:, BB:BB + 256], p[:, BC:BC + 256], p[:, BH:BH + 256], p[:, BZ:BZ + 256],
                   wc_ref[...], zero_row, zero_row)

    low = _low_half(SEQ)

    ones = jnp.ones((SEQ, LANES), BF16)
    kc = (kcf.astype(BF16), _swap64(kcf).astype(BF16))
    vc = (jnp.concatenate([vcf.astype(BF16), ones], axis=1),
          jnp.concatenate([ones, _swap64(vcf).astype(BF16)], axis=1))
    o = []
    for t in range(2):
        q = (p[:, CQ + t * LANES:CQ + (t + 1) * LANES] * Q_SCALE).astype(BF16)
        for half in range(2):
            sw = (t + half) % 2
            o.append(_attend(_dot_nt(_mask_half(q, low, half), kc[sw]), vc[sw], sw, sink_ref[l, 2 * t + half]))
    yc = jnp.concatenate([jnp.where(low, o[0], o[1]), jnp.where(low, o[2], o[3])], axis=1)
    yc = yc * _silu(p[:, CZ:CZ + 256])

    o = []
    for t in range(2):
        q = (p[:, DQ + t * LANES:DQ + (t + 1) * LANES] * Q_SCALE).astype(BF16)
        k = kdf[:, t * LANES:(t + 1) * LANES].astype(BF16)
        v = jnp.concatenate([vdf[:, t * LANES:(t + 1) * LANES].astype(BF16), ones], axis=1)
        for half in range(2):
            o.append(_attend(_dot_nt(_mask_half(q, low, half), k), v, 0))
    yd = jnp.concatenate([jnp.where(low, o[0], o[1]), jnp.where(low, o[2], o[3])], axis=1)
    yd = yd * _silu(p[:, DZ:DZ + 256])

    x_new = _out_proj(x, mod_ref[2, 0:1, :], ya, yb, yc, yd, w_out_ref)
    xs[b] = x_new

    @pl.when(l == DEPTH - 1)
    def _():
        y_ref[...] = _rms(x_new) * fg_ref[...]


def _ctx_layers(x, mod, norm_g, final_g, w_in, w_out, ws_cat, bias_a, w_conv, sink):
    per_layer = lambda *shape: pl.BlockSpec((None,) + shape, lambda l, b: (l,) + (0,) * len(shape))
    state = lambda w: pl.BlockSpec((None, None, SEQ, w), lambda l, b: (b, l, 0, 0))
    return pl.pallas_call(
        _ctx_kernel,
        grid=(DEPTH, BATCH),
        in_specs=[
            pl.BlockSpec(memory_space=pltpu.SMEM),
            pl.BlockSpec((None, SEQ, D_MODEL), lambda l, b: (jnp.where(l == 0, b, BATCH - 1), 0, 0)),
            per_layer(3, N_GROUPS, D_MODEL),
            per_layer(1, D_MODEL),
            pl.BlockSpec((1, D_MODEL), lambda l, b: (0, 0)),
            pl.BlockSpec((None, D_MODEL, D_IN), lambda l, b: (l, 0, 0), pipeline_mode=pl.Buffered(1)),
            pl.BlockSpec((None, D_MODEL, D_MODEL), lambda l, b: (l, 0, 0), pipeline_mode=pl.Buffered(1)),
            per_layer(CHUNK, A_HEADS * CHUNK),
            per_layer(CHUNK, GROUP_W),
            per_layer(3, GROUP_W),
        ],
        out_specs=[
            pl.BlockSpec((None, SEQ, D_MODEL), lambda l, b: (jnp.where(l == DEPTH - 1, b, 0), 0, 0)),
            state(128), state(128), state(256), state(256),
        ],
        out_shape=[
            jax.ShapeDtypeStruct((BATCH, SEQ, D_MODEL), F32),
            jax.ShapeDtypeStruct((BATCH, DEPTH, SEQ, 128), F32),
            jax.ShapeDtypeStruct((BATCH, DEPTH, SEQ, 128), F32),
            jax.ShapeDtypeStruct((BATCH, DEPTH, SEQ, 256), F32),
            jax.ShapeDtypeStruct((BATCH, DEPTH, SEQ, 256), F32),
        ],
        scratch_shapes=[pltpu.VMEM((BATCH, SEQ, D_MODEL), F32)],
        compiler_params=pltpu.CompilerParams(dimension_semantics=("arbitrary", "arbitrary"),
                                             vmem_limit_bytes=VMEM_LIMIT),
        name="ctx_layers",
    )(sink, x, mod, norm_g, final_g, w_in, w_out, ws_cat, bias_a, w_conv)


def _rope(x, cos, sin_signed, first_half):
    swapped = jnp.where(first_half, pltpu.roll(x, LANES - 16, axis=1), pltpu.roll(x, 16, axis=1))
    return x * cos + swapped * sin_signed


def _lat_proj_kernel(x_ref, xp_ref, xn_ref, mod_ref, g_ref, w_ref, cos_ref, sin_ref, ws_ref, ba_ref, wc_ref,
                     yab_ref, sz_ref, qc_ref, kc_ref, vc_ref, qd_ref, kd_ref, vd_ref):
    i = pl.program_id(1)
    grp = pl.ds(1 + pl.program_id(0), 1)
    g = g_ref[...]
    shift = mod_ref[0, grp, :]
    scale = mod_ref[1, grp, :]
    h = _norm_mod(x_ref[...], g, shift, scale).astype(BF16)

    pa = _dot(h, w_ref[:, AU:AU + 3 * GROUP_W])
    ya = _branch_a(pa[:, 0:256], pa[:, 256:512], pa[:, 512:768], ws_ref[...], ba_ref[...])
    yab_ref[:, 0:GROUP_W] = ya.astype(BF16)

    pb = _dot(h, w_ref[:, BB:BB + 4 * GROUP_W])
    x_edge = jnp.concatenate([xp_ref[...], xn_ref[...]], axis=0)
    e = _dot(_norm_mod(x_edge, g, shift, scale).astype(BF16), w_ref[:, BC:BC + 2 * GROUP_W])
    prev_row = jnp.where(i > 0, e[7:8, 0:256] * e[7:8, 256:512], 0.0)
    next_row = jnp.where(i < N_TILES - 1, e[8:9, 0:256] * e[8:9, 256:512], 0.0)
    yb = _branch_b(pb[:, 0:256], pb[:, 256:512], pb[:, 512:768], pb[:, 768:1024], wc_ref[...],
                   prev_row, next_row)
    yab_ref[:, GROUP_W:2 * GROUP_W] = yb.astype(BF16)

    sz_ref[:, 0:256] = _silu(_dot(h, w_ref[:, CZ:CZ + 256]))
    sz_ref[:, 256:512] = _silu(_dot(h, w_ref[:, DZ:DZ + 256]))
    cos = cos_ref[...]
    sin = sin_ref[...]
    first_half = (lax.broadcasted_iota(jnp.int32, (TILE, LANES), 1) % 32) < 16
    qc = _dot(h, w_ref[:, CQ:CQ + 256])
    for t in range(2):
        qt = _rope(qc[:, t * LANES:(t + 1) * LANES], cos, sin, first_half) * Q_SCALE
        qc_ref[:, t * LANES:(t + 1) * LANES] = qt.astype(BF16)
    kv = _dot(h, w_ref[:, CKV:CKV + 256])
    k = _rope(kv[:, 0:LANES], cos, sin, first_half)
    v = kv[:, LANES:2 * LANES]
    kc_ref[:, 0:LANES] = k.astype(BF16)
    kc_ref[:, LANES:2 * LANES] = _swap64(k).astype(BF16)
    vc_ref[:, 0:LANES] = v.astype(BF16)
    vc_ref[:, LANES:2 * LANES] = _swap64(v).astype(BF16)
    qd_ref[...] = (_dot(h, w_ref[:, DQ:DQ + 256]) * Q_SCALE).astype(BF16)
    kd_ref[...] = _dot(h, w_ref[:, DK:DK + 256]).astype(BF16)
    vd_ref[...] = _dot(h, w_ref[:, DV:DV + 256]).astype(BF16)


def _lat_proj(layer, x, mod, norm_g, w_in, cos_t, sin_t, ws_cat, bias_a, w_conv):
    tile = lambda w: pl.BlockSpec((None, TILE, w), lambda b, i: (b, i, 0))
    per_layer = lambda *shape: pl.BlockSpec((None,) + shape, lambda b, i: (layer,) + (0,) * len(shape))
    sds = lambda w, dt: jax.ShapeDtypeStruct((DEC_BATCH, DEC_SEQ, w), dt)
    edge = TILE // 8
    last = DEC_SEQ // 8 - 1
    return pl.pallas_call(
        _lat_proj_kernel,
        grid=(DEC_BATCH, N_TILES),
        in_specs=[
            tile(D_MODEL),
            pl.BlockSpec((None, 8, D_MODEL), lambda b, i: (b, jnp.maximum(i * edge - 1, 0), 0)),
            pl.BlockSpec((None, 8, D_MODEL), lambda b, i: (b, jnp.minimum((i + 1) * edge, last), 0)),
            per_layer(3, N_GROUPS, D_MODEL),
            per_layer(1, D_MODEL),
            per_layer(D_MODEL, D_IN),
            pl.BlockSpec((TILE, LANES), lambda b, i: (i, 0)),
            pl.BlockSpec((TILE, LANES), lambda b, i: (i, 0)),
            per_layer(CHUNK, A_HEADS * CHUNK),
            per_layer(CHUNK, GROUP_W),
            per_layer(3, GROUP_W),
        ],
        out_specs=[tile(512), tile(512)] + [tile(256)] * 6,
        out_shape=[sds(512, BF16), sds(512, F32)] + [sds(256, BF16)] * 6,
        compiler_params=pltpu.CompilerParams(dimension_semantics=("arbitrary", "arbitrary"),
                                             vmem_limit_bytes=VMEM_LIMIT),
        name="lat_proj",
    )(x, x, x, mod, norm_g, w_in, cos_t, sin_t, ws_cat, bias_a, w_conv)


_C_GROUPS = (((0, 0), (1, 1)), ((0, 1), (1, 0)))
_D_GROUPS = (((0, 0), (0, 1)), ((1, 0), (1, 1)))


def _stacked_queries(q_ref, rows, low, groups):
    return [jnp.concatenate([_mask_half(q_ref[rows, t * LANES:(t + 1) * LANES], low, half)
                             for t, half in group], axis=0) for group in groups]


def _lat_mix_kernel(layer, sink_ref, x_ref, mod_ref, fg_ref, yab_ref, sz_ref,
                    qc_ref, kc_ref, kcp_ref, kcn_ref, vc_ref, vcp_ref, vcn_ref,
                    qd_ref, kd_ref, kdp_ref, kdn_ref, vd_ref, vdp_ref, vdn_ref,
                    cck_ref, ccv_ref, cdk_ref, cdv_ref, cmask_ref, tab_ref, w_out_ref,
                    xo_ref,
                    kce, vce, kde, vde, yc_scr, yd_scr, m_ctx_c, acc_ctx_c, m_ctx_d, acc_ctx_d, *stage_bufs):
    b = pl.program_id(0)
    i = pl.program_id(1)
    s_c, s_d, m_c, m_d = stage_bufs[0:2], stage_bufs[2:4], stage_bufs[4:6], stage_bufs[6:8]

    def file_context(m_ctx, acc_ctx, m, acc, group, step_rows, n_steps):
        for which in range(2):
            for k in range(n_steps):
                src = slice(which * TILE + k * step_rows, which * TILE + (k + 1) * step_rows)
                dst = slice((4 * k + 2 * group + which) * step_rows, (4 * k + 2 * group + which + 1) * step_rows)
                m_ctx[dst, :] = m[src]
                acc_ctx[dst, :] = acc[src]

    kce[0:QBLK, :] = kcp_ref[...]
    kce[QBLK:QBLK + TILE, :] = kc_ref[...]
    kce[QBLK + TILE:, :] = kcn_ref[...]
    for c_src, c_dst in ((0, 0), (LANES, 2 * LANES)):
        vce[0:QBLK, c_dst:c_dst + LANES] = vcp_ref[:, c_src:c_src + LANES]
        vce[QBLK:QBLK + TILE, c_dst:c_dst + LANES] = vc_ref[:, c_src:c_src + LANES]
        vce[QBLK + TILE:, c_dst:c_dst + LANES] = vcn_ref[:, c_src:c_src + LANES]
    vce[:, LANES:2 * LANES] = jnp.ones((TILE + 2 * QBLK, LANES), BF16)
    cck = cck_ref[...]
    ccv = ccv_ref[...]
    ones_ctx = jnp.ones((PAST_LEN, LANES), BF16)
    k_ctx = (cck.astype(BF16), _swap64(cck).astype(BF16))
    v_ctx = jnp.concatenate([ccv.astype(BF16), ones_ctx, _swap64(ccv).astype(BF16)], axis=1)
    low_q = _low_half(QBLK)
    low_t = _low_half(TILE)
    n_blocks = DEC_SEQ // QBLK
    blocks_per_tile = TILE // QBLK

    for sw, q in enumerate(_stacked_queries(qc_ref, slice(None), low_t, _C_GROUPS)):
        m, acc = _context_pass(q, k_ctx[sw], v_ctx[:, sw * LANES:(sw + 2) * LANES])
        file_context(m_ctx_c, acc_ctx_c, m, acc, sw, QBLK, blocks_per_tile)

    c_sink = jnp.concatenate([jnp.full((QBLK, 1), sink_ref[layer, 2 * t + half] * LOG2E, F32)
                              for group in _C_GROUPS for t, half in group], axis=0)

    def c_scores(j):
        n = i * blocks_per_tile + j
        window = cmask_ref[jnp.where(n == 0, 1, jnp.where(n == n_blocks - 1, 2, 0))]
        window = jnp.concatenate([window, window], axis=0)
        rows = slice(j * QBLK, (j + 1) * QBLK)
        for sw, q in enumerate(_stacked_queries(qc_ref, rows, low_q, _C_GROUPS)):
            k_loc = kce[j * QBLK:(j + 3) * QBLK, sw * LANES:(sw + 1) * LANES]
            s = _dot_nt(q, k_loc) + window
            dst = slice(2 * sw * QBLK, 2 * (sw + 1) * QBLK)
            ctx_rows = slice((4 * j + 2 * sw) * QBLK, (4 * j + 2 * sw + 2) * QBLK)
            s_c[j % 2][dst, :] = s
            m_c[j % 2][dst, :] = _joint_max(s, m_ctx_c[ctx_rows, :], c_sink[dst])

    def c_softmax(j):
        step = slice(4 * j * QBLK, 4 * (j + 1) * QBLK)
        e, ctx_scale, sink_term = _local_softmax(s_c[j % 2][...], m_c[j % 2][...], m_ctx_c[step, :], c_sink)
        values = [(vce[j * QBLK:(j + 3) * QBLK, sw * LANES:(sw + 2) * LANES], sw) for sw in range(2)]
        r0, r1 = _local_output(e, values, acc_ctx_c[step, :], ctx_scale, sink_term)
        yc_scr[j * QBLK:(j + 1) * QBLK, :] = jnp.concatenate(
            [jnp.where(low_q, r0[0:QBLK], r1[0:QBLK]), jnp.where(low_q, r1[QBLK:], r0[QBLK:])], axis=1)

    _skewed(blocks_per_tile, (c_scores, c_softmax))

    halo = 4 * GRID_W
    kde[0:halo, :] = kdp_ref[...]
    kde[halo:halo + TILE, :] = kd_ref[...]
    kde[halo + TILE:, :] = kdn_ref[...]
    for c_src, c_dst in ((0, 0), (LANES, 2 * LANES)):
        vde[0:halo, c_dst:c_dst + LANES] = vdp_ref[:, c_src:c_src + LANES]
        vde[halo:halo + TILE, c_dst:c_dst + LANES] = vd_ref[:, c_src:c_src + LANES]
        vde[halo + TILE:, c_dst:c_dst + LANES] = vdn_ref[:, c_src:c_src + LANES]
    vde[:, LANES:2 * LANES] = jnp.ones((TILE + 2 * halo, LANES), BF16)
    kd_ctx = cdk_ref[...].astype(BF16)
    cdv = cdv_ref[...].astype(BF16)
    vd_ctx = jnp.concatenate([cdv[:, 0:LANES], ones_ctx, cdv[:, LANES:2 * LANES]], axis=1)
    low_r = _low_half(GRID_W)
    n_rows = DEC_SEQ // GRID_W

    for t, q in enumerate(_stacked_queries(qd_ref, slice(None), low_t, _D_GROUPS)):
        m, acc = _context_pass(q, kd_ctx[:, t * LANES:(t + 1) * LANES], vd_ctx[:, t * LANES:(t + 2) * LANES])
        file_context(m_ctx_d, acc_ctx_d, m, acc, t, GRID_W, ROWS_PER_TILE)

    def window_start(rl):
        r = i * ROWS_PER_TILE + rl
        rs = jnp.clip(r - WIN_R // 2, 0, n_rows - WIN_R)
        return rs - r + (WIN_R - 1), pl.multiple_of((rs - i * ROWS_PER_TILE + 4) * GRID_W, GRID_W)

    def d_scores(rl):
        didx, off = window_start(rl)
        rows = slice(rl * GRID_W, (rl + 1) * GRID_W)
        for t, q in enumerate(_stacked_queries(qd_ref, rows, low_r, _D_GROUPS)):
            k_loc = kde[pl.ds(off, WIN_R * GRID_W), t * LANES:(t + 1) * LANES]
            bias = jnp.concatenate([tab_ref[2 * t, didx], tab_ref[2 * t + 1, didx]], axis=0)
            s = _dot_nt(q, k_loc) + bias
            dst = slice(2 * t * GRID_W, 2 * (t + 1) * GRID_W)
            ctx_rows = slice((4 * rl + 2 * t) * GRID_W, (4 * rl + 2 * t + 2) * GRID_W)
            s_d[rl % 2][dst, :] = s
            m_d[rl % 2][dst, :] = _joint_max(s, m_ctx_d[ctx_rows, :])

    def d_softmax(rl):
        _, off = window_start(rl)
        step = slice(4 * rl * GRID_W, 4 * (rl + 1) * GRID_W)
        e, ctx_scale, _ = _local_softmax(s_d[rl % 2][...], m_d[rl % 2][...], m_ctx_d[step, :])
        values = [(vde[pl.ds(off, WIN_R * GRID_W), t * LANES:(t + 2) * LANES], t) for t in range(2)]
        r = _local_output(e, values, acc_ctx_d[step, :], ctx_scale)
        yd_scr[rl * GRID_W:(rl + 1) * GRID_W, :] = jnp.concatenate(
            [jnp.where(low_r, r[t][0:GRID_W], r[t][GRID_W:]) for t in range(2)], axis=1)

    _skewed(ROWS_PER_TILE, (d_scores, d_softmax))

    yc = (yc_scr[...] * sz_ref[:, 0:256]).astype(BF16)
    yd = (yd_scr[...] * sz_ref[:, 256:512]).astype(BF16)
    y = (_dot(yab_ref[...], w_out_ref[0:2 * GROUP_W, :]) + _dot(yc, w_out_ref[2 * GROUP_W:3 * GROUP_W, :])
         + _dot(yd, w_out_ref[3 * GROUP_W:4 * GROUP_W, :]))
    x_new = x_ref[...] + mod_ref[2, pl.ds(1 + b, 1), :] * y
    if layer == DEPTH - 1:
        x_new = _rms(x_new) * fg_ref[...]
    xo_ref[...] = x_new


def _lat_mix(layer, x, mod, final_g, yab, sz, qc, kc, vc, qd, kd, vd, cck, ccv, cdk, cdv, tab, w_out, sink):
    tile = lambda w: pl.BlockSpec((None, TILE, w), lambda b, i: (b, i, 0))
    per_layer = lambda *shape: pl.BlockSpec((None,) + shape, lambda b, i: (layer,) + (0,) * len(shape))

    def halo(rows, w):
        per = TILE // rows
        last = DEC_SEQ // rows - 1
        prev = pl.BlockSpec((None, rows, w), lambda b, i: (b, jnp.maximum(i * per - 1, 0), 0))
        nxt = pl.BlockSpec((None, rows, w), lambda b, i: (b, jnp.minimum((i + 1) * per, last), 0))
        return prev, nxt

    cprev, cnext = halo(QBLK, 256)
    dprev, dnext = halo(4 * GRID_W, 256)
    cache = lambda w: pl.BlockSpec((None, None, PAST_LEN, w), lambda b, i: (b, layer, 0, 0))
    return pl.pallas_call(
        functools.partial(_lat_mix_kernel, layer),
        grid=(DEC_BATCH, N_TILES),
        in_specs=[
            pl.BlockSpec(memory_space=pltpu.SMEM),
            tile(D_MODEL),
            per_layer(3, N_GROUPS, D_MODEL),
            pl.BlockSpec((1, D_MODEL), lambda b, i: (0, 0)),
            tile(512), tile(512),
            tile(256), tile(256), cprev, cnext, tile(256), cprev, cnext,
            tile(256), tile(256), dprev, dnext, tile(256), dprev, dnext,
            cache(128), cache(128), cache(256), cache(256),
            pl.BlockSpec((3, QBLK, 3 * QBLK), lambda b, i: (0, 0, 0)),
            per_layer(4, WIN_R, GRID_W, WIN_R * GRID_W),
            per_layer(D_MODEL, D_MODEL),
        ],
        out_specs=tile(D_MODEL),
        out_shape=jax.ShapeDtypeStruct((DEC_BATCH, DEC_SEQ, D_MODEL), F32),
        scratch_shapes=[
            pltpu.VMEM((TILE + 2 * QBLK, 2 * LANES), BF16),
            pltpu.VMEM((TILE + 2 * QBLK, 3 * LANES), BF16),
            pltpu.VMEM((TILE + 8 * GRID_W, 2 * LANES), BF16),
            pltpu.VMEM((TILE + 8 * GRID_W, 3 * LANES), BF16),
            pltpu.VMEM((TILE, 256), F32),
            pltpu.VMEM((TILE, 256), F32),
            pltpu.VMEM((4 * TILE, 1), F32),
            pltpu.VMEM((4 * TILE, 2 * LANES), F32),
            pltpu.VMEM((4 * TILE, 1), F32),
            pltpu.VMEM((4 * TILE, 2 * LANES), F32),
        ] + 2 * [pltpu.VMEM((4 * QBLK, 3 * QBLK), F32)] + 2 * [pltpu.VMEM((4 * GRID_W, WIN_R * GRID_W), F32)]
          + 2 * [pltpu.VMEM((4 * QBLK, 1), F32)] + 2 * [pltpu.VMEM((4 * GRID_W, 1), F32)],
        compiler_params=pltpu.CompilerParams(dimension_semantics=("arbitrary", "arbitrary"),
                                             vmem_limit_bytes=VMEM_LIMIT),
        name="lat_mix",
    )(sink, x, mod, final_g, yab, sz, qc, kc, kc, kc, vc, vc, vc,
      qd, kd, kd, kd, vd, vd, vd, cck, ccv, cdk, cdv, _window_masks(), tab, w_out)


def _window_masks():
    p = np.arange(QBLK)[:, None]
    j = np.arange(3 * QBLK)[None, :]
    band = np.abs(j - QBLK - p) <= WINDOW
    masks = [band, band & (j >= QBLK), band & (j < 2 * QBLK)]
    return jnp.asarray(np.where(np.stack(masks), 0.0, NEG), F32)


def _rope_tables():
    t = np.arange(DEC_SEQ)
    freqs = (np.float32(ROPE_BASE) ** (-np.arange(16, dtype=np.float32) / np.float32(16))).astype(np.float32)
    ang_r = (t // GRID_W).astype(np.float32)[:, None] * freqs
    ang_c = (t % GRID_W).astype(np.float32)[:, None] * freqs
    cos_h = np.concatenate([np.cos(ang_r), np.cos(ang_r), np.cos(ang_c), np.cos(ang_c)], axis=1)
    sin_h = np.concatenate([-np.sin(ang_r), np.sin(ang_r), -np.sin(ang_c), np.sin(ang_c)], axis=1)
    return (jnp.asarray(np.tile(cos_h, (1, 2)), F32), jnp.asarray(np.tile(sin_h, (1, 2)), F32))


def kernel(x_prompt, x_sample, cache_c_k, cache_c_v, cache_d_k, cache_d_v, c, c_ctx, norm_g, w_mod, b_mod,
           w_in, w_out, w_s, b_s, w_conv, sink, rpb, final_g):
    w_in_b = w_in.astype(BF16)
    w_out_b = w_out.astype(BF16)
    ws_cat = jnp.transpose(w_s, (0, 2, 1, 3)).reshape(DEPTH, CHUNK, A_HEADS * CHUNK).astype(BF16)
    bias_a = jnp.repeat(jnp.transpose(b_s, (0, 2, 1)), HEAD_DIM, axis=2)
    norm_g3 = norm_g.reshape(DEPTH, 1, D_MODEL)
    fg = final_g.reshape(1, D_MODEL)
    cos_t, sin_t = _rope_tables()
    cck = cache_c_k.reshape(DEC_BATCH, DEPTH, PAST_LEN, 128)
    ccv = cache_c_v.reshape(DEC_BATCH, DEPTH, PAST_LEN, 128)
    cdk = cache_d_k.reshape(DEC_BATCH, DEPTH, PAST_LEN, 256)
    cdv = cache_d_v.reshape(DEC_BATCH, DEPTH, PAST_LEN, 256)

    cond = jnp.concatenate([c_ctx[None, :], c, jnp.zeros((N_GROUPS - 1 - DEC_BATCH, D_MODEL), F32)], axis=0)
    mod = _modulation(cond, w_mod, b_mod)
    tab = _nbr_tables(rpb)

    y_prompt, s_ck, s_cv, s_dk, s_dv = _ctx_layers(x_prompt, mod, norm_g3, fg, w_in_b, w_out_b, ws_cat,
                                                   bias_a, w_conv, sink)
    xs = x_sample
    for l in range(DEPTH):
        yab, sz, qc, kc, vc, qd, kd, vd = _lat_proj(l, xs, mod, norm_g3, w_in_b, cos_t, sin_t,
                                                    ws_cat, bias_a, w_conv)
        xs = _lat_mix(l, xs, mod, fg, yab, sz, qc, kc, vc, qd, kd, vd, cck, ccv, cdk, cdv, tab, w_out_b, sink)

    shape_c = (BATCH, DEPTH, SEQ, 2, HEAD_DIM)
    shape_d = (BATCH, DEPTH, SEQ, 4, HEAD_DIM)
    return (y_prompt, xs, s_ck.reshape(shape_c), s_cv.reshape(shape_c), s_dk.reshape(shape_d),
            s_dv.reshape(shape_d))
```

```python
import functools

import numpy as np
import jax
import jax.numpy as jnp
from jax import lax
from jax.experimental import pallas as pl
from jax.experimental.pallas import tpu as pltpu

D_MODEL = 1024
BATCH = 16
SEQ = 256
DEPTH = 4
DEC_BATCH = 2
DEC_SEQ = 4096
PAST_LEN = 512
GRID_W = 64
GROUP_W = 256
HEAD_DIM = 64
A_HEADS = 4
CHUNK = 128
WINDOW = 128
QBLK = 128
WIN_R = 8
WIN_C = 16
ROPE_BASE = 10000.0
EPS = 1e-6
NEG = -1e30
D_IN = 3584
LOG2E = 1.4426950408889634
Q_SCALE = HEAD_DIM ** -0.5 * LOG2E

AU, AV, AZ, BB, BC, BH, BZ = 0, 256, 512, 768, 1024, 1280, 1536
CQ, CKV, CZ, DQ, DK, DV, DZ = 1792, 2048, 2304, 2560, 2816, 3072, 3328
N_AB = 1792

LANES = 128
TILE = 512
ROWS_PER_TILE = TILE // GRID_W
N_TILES = DEC_SEQ // TILE
CTX_SEQS = 2
PROJ_TILES = 2
N_GROUPS = 8
VMEM_LIMIT = 56 * 1024 * 1024

F32 = jnp.float32
BF16 = jnp.bfloat16


def _silu(z):
    return z * (1.0 / (1.0 + jnp.exp(-z)))


def _dot(a, b):
    return jnp.dot(a, b, preferred_element_type=F32)


def _dot_nt(a, b):
    return lax.dot_general(a, b, (((1,), (1,)), ((), ())), preferred_element_type=F32)


def _rms(x):
    return x * lax.rsqrt(jnp.mean(x * x, axis=-1, keepdims=True) + EPS)


def _norm_mod(x, g, shift, scale):
    return _rms(x) * (g * (1.0 + scale)) + shift


def _swap64(x):
    return pltpu.roll(x, HEAD_DIM, axis=1)


def _group_mean_matrix():
    r = lax.broadcasted_iota(jnp.int32, (GROUP_W, GROUP_W), 0) // HEAD_DIM
    c = lax.broadcasted_iota(jnp.int32, (GROUP_W, GROUP_W), 1) // HEAD_DIM
    return jnp.where(r == c, 1.0 / HEAD_DIM, 0.0).astype(BF16)


def _branch_a(au, av, az, ws, bias):
    t = av.shape[0]
    sq = av * av
    hi = sq.astype(BF16)
    lo = (sq - hi.astype(F32)).astype(BF16)
    gm = _group_mean_matrix()
    ms = _dot(hi, gm) + _dot(lo, gm)
    vh = (av * lax.rsqrt(ms + EPS)).astype(BF16)
    head = lax.broadcasted_iota(jnp.int32, (CHUNK, GROUP_W), 1) // HEAD_DIM
    outs = []
    for n in range(t // CHUNK):
        v = vh[n * CHUNK:(n + 1) * CHUNK]
        rhs = jnp.concatenate([jnp.where(head == h, v, jnp.zeros_like(v)) for h in range(A_HEADS)], axis=0)
        outs.append(_dot(ws, rhs) + bias)
    mixed = jnp.concatenate(outs, axis=0)
    return au * mixed * _silu(az)


def _branch_b(bb, bc, bh, bz, wc, prev_row, next_row):
    t = bb.shape[0]
    xc = bc * bh
    row = lax.broadcasted_iota(jnp.int32, xc.shape, 0)
    xm = jnp.where(row == 0, prev_row, pltpu.roll(xc, 1, axis=0))
    xp = jnp.where(row == t - 1, next_row, pltpu.roll(xc, t - 1, axis=0))
    y = wc[0:1, :] * xm + wc[1:2, :] * xc + wc[2:3, :] * xp
    return bb * y * _silu(bz)


def _attend(score, value, pos, sink=None):
    m = jnp.max(score, axis=-1, keepdims=True)
    if sink is not None:
        m = jnp.maximum(m, sink * LOG2E)
    acc = _dot(jnp.exp2(score - m).astype(BF16), value)
    o = acc[:, pos * LANES:(pos + 1) * LANES]
    l = acc[:, (1 - pos) * LANES:(2 - pos) * LANES]
    if sink is not None:
        l = l + jnp.exp2(sink * LOG2E - m)
    return o * (1.0 / l)


def _context_pass(q_stacked, k, value):
    s = _dot_nt(q_stacked, k)
    m = jnp.max(s, axis=-1, keepdims=True)
    return m, _dot(jnp.exp2(s - m).astype(BF16), value)


def _local_softmax(s_loc, m_ctx, sink=None):
    m = jnp.maximum(jnp.max(s_loc, axis=-1, keepdims=True), m_ctx)
    if sink is not None:
        m = jnp.maximum(m, sink)
    e = jnp.exp2(s_loc - m).astype(BF16)
    return e, jnp.exp2(m_ctx - m), (None if sink is None else jnp.exp2(sink - m))


def _local_output(e, values, acc_ctx, ctx_scale, sink_term=None):
    rows = e.shape[0] // len(values)
    outs = []
    for p, (v, pos) in enumerate(values):
        r = slice(p * rows, (p + 1) * rows)
        acc = _dot(e[r], v) + acc_ctx[r] * ctx_scale[r]
        o = acc[:, pos * LANES:(pos + 1) * LANES]
        l = acc[:, (1 - pos) * LANES:(2 - pos) * LANES]
        if sink_term is not None:
            l = l + sink_term[r]
        outs.append(o * (1.0 / l))
    return outs


def _skewed(n_steps, stages):
    for tick in range(n_steps + len(stages) - 1):
        for lag, stage in enumerate(stages):
            if 0 <= tick - lag < n_steps:
                stage(tick - lag)


def _low_half(m):
    return lax.broadcasted_iota(jnp.int32, (m, LANES), 1) < HEAD_DIM


def _mask_half(q, low, half):
    z = jnp.zeros_like(q)
    return jnp.where(low, q, z) if half == 0 else jnp.where(low, z, q)


def _out_proj(x, gate, ya, yb, yc, yd, w_out_ref):
    y = _dot(ya.astype(BF16), w_out_ref[0:256, :])
    y += _dot(yb.astype(BF16), w_out_ref[256:512, :])
    y += _dot(yc.astype(BF16), w_out_ref[512:768, :])
    y += _dot(yd.astype(BF16), w_out_ref[768:1024, :])
    return x + gate * y


def _mod_kernel(cond_ref, w_ref, b_ref, o_ref):
    mod = _dot(_silu(cond_ref[...]).astype(BF16), w_ref[...].astype(BF16)) + b_ref[...]
    for j in range(3):
        o_ref[j] = mod[:, j * D_MODEL:(j + 1) * D_MODEL]


def _modulation(cond, w_mod, b_mod):
    return pl.pallas_call(
        _mod_kernel,
        grid=(DEPTH,),
        in_specs=[
            pl.BlockSpec((N_GROUPS, D_MODEL), lambda l: (0, 0)),
            pl.BlockSpec((None, D_MODEL, 3 * D_MODEL), lambda l: (l, 0, 0)),
            pl.BlockSpec((None, 1, 3 * D_MODEL), lambda l: (l, 0, 0)),
        ],
        out_specs=pl.BlockSpec((None, 3, N_GROUPS, D_MODEL), lambda l: (l, 0, 0, 0)),
        out_shape=jax.ShapeDtypeStruct((DEPTH, 3, N_GROUPS, D_MODEL), F32),
        compiler_params=pltpu.CompilerParams(dimension_semantics=("arbitrary",),
                                             vmem_limit_bytes=VMEM_LIMIT),
        name="modulation",
    )(cond, w_mod, b_mod.reshape(DEPTH, 1, 3 * D_MODEL))


def _nbr_table_kernel(rpb_ref, o_ref):
    c = lax.broadcasted_iota(jnp.int32, (GRID_W, LANES), 0)
    lane = lax.broadcasted_iota(jnp.int32, (GRID_W, LANES), 1)
    cc = lane % GRID_W
    cstart = jnp.clip(c - WIN_C // 2, 0, GRID_W - WIN_C)
    for h in range(4):
        tiles = [pltpu.roll(jnp.broadcast_to(rpb_ref[h, ro:ro + 1, :] * LOG2E, (GRID_W, LANES)),
                            LANES - (WIN_C - 1), 1, stride=1, stride_axis=0)
                 for ro in range(2 * WIN_R - 2)]
        for d in range(WIN_R):
            for p in range(WIN_R // 2):
                v = tiles[d + 2 * p]
                v = jnp.where(cc >= cstart, jnp.where(cc < cstart + WIN_C, v, NEG), NEG)
                o_ref[h, d, :, p * LANES:(p + 1) * LANES] = v


def _nbr_tables(rpb):
    half = LANES // 2 - (2 * WIN_C - 1)
    rows = jnp.pad(rpb, ((0, 0), (0, 0), (0, 1), (0, half)))
    rpb_p = jnp.concatenate([rows[:, :, :-1], rows[:, :, 1:]], axis=-1)
    rpb_p = jnp.pad(rpb_p, ((0, 0), (0, 0), (0, 1), (0, 0)))
    return pl.pallas_call(
        _nbr_table_kernel,
        grid=(DEPTH,),
        in_specs=[pl.BlockSpec((None, 4, 2 * WIN_R, LANES), lambda l: (l, 0, 0, 0))],
        out_specs=pl.BlockSpec((None, 4, WIN_R, GRID_W, WIN_R * GRID_W), lambda l: (l, 0, 0, 0, 0)),
        out_shape=jax.ShapeDtypeStruct((DEPTH, 4, WIN_R, GRID_W, WIN_R * GRID_W), F32),
        compiler_params=pltpu.CompilerParams(dimension_semantics=("arbitrary",)),
        name="nbr_tables",
    )(rpb_p)


def _ctx_kernel(sink_ref, x_ref, mod_ref, g_ref, fg_ref, w_in_ref, w_out_ref, ws_ref, ba_ref, wc_ref,
                y_ref, ck_ref, cv_ref, dk_ref, dv_ref, xs):
    l = pl.program_id(0)
    pair = pl.program_id(1)

    @pl.when(l == 0)
    def _():
        for u in range(CTX_SEQS):
            xs[CTX_SEQS * pair + u] = x_ref[u]

    low = _low_half(SEQ)
    ones = jnp.ones((SEQ, LANES), BF16)

    def project(u):
        x = xs[CTX_SEQS * pair + u]
        h = _norm_mod(x, g_ref[...], mod_ref[0, 0:1, :], mod_ref[1, 0:1, :]).astype(BF16)
        return x, _dot(h, w_in_ref[...])

    def mix(u, p):
        kcf = p[:, CKV:CKV + 128]
        vcf = p[:, CKV + 128:CKV + 256]
        kdf = p[:, DK:DK + 256]
        vdf = p[:, DV:DV + 256]
        ck_ref[u] = kcf
        cv_ref[u] = vcf
        dk_ref[u] = kdf
        dv_ref[u] = vdf

        ya = _branch_a(p[:, AU:AU + 256], p[:, AV:AV + 256], p[:, AZ:AZ + 256], ws_ref[...], ba_ref[...])
        zero_row = jnp.zeros((1, GROUP_W), F32)
        yb = _branch_b(p[:, BB:BB + 256], p[:, BC:BC + 256], p[:, BH:BH + 256], p[:, BZ:BZ + 256],
                       wc_ref[...], zero_row, zero_row)

        kc = (kcf.astype(BF16), _swap64(kcf).astype(BF16))
        vc = (jnp.concatenate([vcf.astype(BF16), ones], axis=1),
              jnp.concatenate([ones, _swap64(vcf).astype(BF16)], axis=1))
        o = []
        for t in range(2):
            q = (p[:, CQ + t * LANES:CQ + (t + 1) * LANES] * Q_SCALE).astype(BF16)
            for half in range(2):
                sw = (t + half) % 2
                o.append(_attend(_dot_nt(_mask_half(q, low, half), kc[sw]), vc[sw], sw,
                                 sink_ref[l, 2 * t + half]))
        yc = jnp.concatenate([jnp.where(low, o[0], o[1]), jnp.where(low, o[2], o[3])], axis=1)
        yc = yc * _silu(p[:, CZ:CZ + 256])

        o = []
        for t in range(2):
            q = (p[:, DQ + t * LANES:DQ + (t + 1) * LANES] * Q_SCALE).astype(BF16)
            k = kdf[:, t * LANES:(t + 1) * LANES].astype(BF16)
            v = jnp.concatenate([vdf[:, t * LANES:(t + 1) * LANES].astype(BF16), ones], axis=1)
            for half in range(2):
                o.append(_attend(_dot_nt(_mask_half(q, low, half), k), v, 0))
        yd = jnp.concatenate([jnp.where(low, o[0], o[1]), jnp.where(low, o[2], o[3])], axis=1)
        yd = yd * _silu(p[:, DZ:DZ + 256])
        return ya, yb, yc, yd

    def finish(u, x, ys):
        x_new = _out_proj(x, mod_ref[2, 0:1, :], *ys, w_out_ref)
        xs[CTX_SEQS * pair + u] = x_new
        return x_new

    staged = [project(0)]
    x_new = []
    for u in range(CTX_SEQS):
        if u + 1 < CTX_SEQS:
            staged.append(project(u + 1))
        x, p = staged[u]
        x_new.append(finish(u, x, mix(u, p)))

    @pl.when(l == DEPTH - 1)
    def _():
        for u in range(CTX_SEQS):
            y_ref[u] = _rms(x_new[u]) * fg_ref[...]


def _ctx_layers(x, mod, norm_g, final_g, w_in, w_out, ws_cat, bias_a, w_conv, sink):
    per_layer = lambda *shape: pl.BlockSpec((None,) + shape, lambda l, b: (l,) + (0,) * len(shape))
    state = lambda w: pl.BlockSpec((CTX_SEQS, None, SEQ, w), lambda l, b: (b, l, 0, 0))
    n_steps = BATCH // CTX_SEQS
    return pl.pallas_call(
        _ctx_kernel,
        grid=(DEPTH, n_steps),
        in_specs=[
            pl.BlockSpec(memory_space=pltpu.SMEM),
            pl.BlockSpec((CTX_SEQS, SEQ, D_MODEL), lambda l, b: (jnp.where(l == 0, b, n_steps - 1), 0, 0)),
            per_layer(3, N_GROUPS, D_MODEL),
            per_layer(1, D_MODEL),
            pl.BlockSpec((1, D_MODEL), lambda l, b: (0, 0)),
            pl.BlockSpec((None, D_MODEL, D_IN), lambda l, b: (l, 0, 0), pipeline_mode=pl.Buffered(1)),
            pl.BlockSpec((None, D_MODEL, D_MODEL), lambda l, b: (l, 0, 0), pipeline_mode=pl.Buffered(1)),
            per_layer(CHUNK, A_HEADS * CHUNK),
            per_layer(CHUNK, GROUP_W),
            per_layer(3, GROUP_W),
        ],
        out_specs=[
            pl.BlockSpec((CTX_SEQS, SEQ, D_MODEL), lambda l, b: (jnp.where(l == DEPTH - 1, b, 0), 0, 0)),
            state(128), state(128), state(256), state(256),
        ],
        out_shape=[
            jax.ShapeDtypeStruct((BATCH, SEQ, D_MODEL), F32),
            jax.ShapeDtypeStruct((BATCH, DEPTH, SEQ, 128), F32),
            jax.ShapeDtypeStruct((BATCH, DEPTH, SEQ, 128), F32),
            jax.ShapeDtypeStruct((BATCH, DEPTH, SEQ, 256), F32),
            jax.ShapeDtypeStruct((BATCH, DEPTH, SEQ, 256), F32),
        ],
        scratch_shapes=[pltpu.VMEM((BATCH, SEQ, D_MODEL), F32)],
        compiler_params=pltpu.CompilerParams(dimension_semantics=("arbitrary", "arbitrary"),
                                             vmem_limit_bytes=VMEM_LIMIT),
        name="ctx_layers",
    )(sink, x, mod, norm_g, final_g, w_in, w_out, ws_cat, bias_a, w_conv)


def _rope(x, cos, sin_signed, first_half):
    swapped = jnp.where(first_half, pltpu.roll(x, LANES - 16, axis=1), pltpu.roll(x, 16, axis=1))
    return x * cos + swapped * sin_signed


def _lat_proj_kernel(x_ref, xp_ref, xn_ref, mod_ref, g_ref, w_ref, cos_ref, sin_ref, ws_ref, ba_ref, wc_ref,
                     yab_ref, sz_ref, qc_ref, kc_ref, vc_ref, qd_ref, kd_ref, vd_ref):
    i = pl.program_id(1)
    grp = pl.ds(1 + pl.program_id(0), 1)
    g = g_ref[...]
    shift = mod_ref[0, grp, :]
    scale = mod_ref[1, grp, :]
    first_half = (lax.broadcasted_iota(jnp.int32, (TILE, LANES), 1) % 32) < 16

    def normalised(x):
        return _norm_mod(x, g, shift, scale).astype(BF16)

    def project(u, h, after_first_dot):
        rows = slice(u * TILE, (u + 1) * TILE)
        tile_index = PROJ_TILES * i + u
        pa = _dot(h, w_ref[:, AU:AU + 3 * GROUP_W])
        after_first_dot()
        ya = _branch_a(pa[:, 0:256], pa[:, 256:512], pa[:, 512:768], ws_ref[...], ba_ref[...])
        yab_ref[rows, 0:GROUP_W] = ya.astype(BF16)

        pb = _dot(h, w_ref[:, BB:BB + 4 * GROUP_W])
        before = xp_ref[...] if u == 0 else x_ref[u * TILE - 8:u * TILE, :]
        after = xn_ref[...] if u == PROJ_TILES - 1 else x_ref[(u + 1) * TILE:(u + 1) * TILE + 8, :]
        e = _dot(normalised(jnp.concatenate([before, after], axis=0)), w_ref[:, BC:BC + 2 * GROUP_W])
        prev_row = jnp.where(tile_index > 0, e[7:8, 0:256] * e[7:8, 256:512], 0.0)
        next_row = jnp.where(tile_index < N_TILES - 1, e[8:9, 0:256] * e[8:9, 256:512], 0.0)
        yb = _branch_b(pb[:, 0:256], pb[:, 256:512], pb[:, 512:768], pb[:, 768:1024], wc_ref[...],
                       prev_row, next_row)
        yab_ref[rows, GROUP_W:2 * GROUP_W] = yb.astype(BF16)

        sz_ref[rows, 0:256] = _silu(_dot(h, w_ref[:, CZ:CZ + 256]))
        sz_ref[rows, 256:512] = _silu(_dot(h, w_ref[:, DZ:DZ + 256]))
        cos = cos_ref[rows, :]
        sin = sin_ref[rows, :]
        qc = _dot(h, w_ref[:, CQ:CQ + 256])
        for t in range(2):
            qt = _rope(qc[:, t * LANES:(t + 1) * LANES], cos, sin, first_half) * Q_SCALE
            qc_ref[rows, t * LANES:(t + 1) * LANES] = qt.astype(BF16)
        kv = _dot(h, w_ref[:, CKV:CKV + 256])
        k = _rope(kv[:, 0:LANES], cos, sin, first_half)
        v = kv[:, LANES:2 * LANES]
        kc_ref[rows, 0:LANES] = k.astype(BF16)
        kc_ref[rows, LANES:2 * LANES] = _swap64(k).astype(BF16)
        vc_ref[rows, 0:LANES] = v.astype(BF16)
        vc_ref[rows, LANES:2 * LANES] = _swap64(v).astype(BF16)
        qd_ref[rows, :] = (_dot(h, w_ref[:, DQ:DQ + 256]) * Q_SCALE).astype(BF16)
        kd_ref[rows, :] = _dot(h, w_ref[:, DK:DK + 256]).astype(BF16)
        vd_ref[rows, :] = _dot(h, w_ref[:, DV:DV + 256]).astype(BF16)

    hs = [normalised(x_ref[0:TILE, :])]

    def normalise_next(u):
        if u + 1 < PROJ_TILES:
            hs.append(normalised(x_ref[(u + 1) * TILE:(u + 2) * TILE, :]))

    for u in range(PROJ_TILES):
        project(u, hs[u], functools.partial(normalise_next, u))


def _lat_proj(layer, x, mod, norm_g, w_in, cos_t, sin_t, ws_cat, bias_a, w_conv):
    step_rows = PROJ_TILES * TILE
    tile = lambda w: pl.BlockSpec((None, step_rows, w), lambda b, i: (b, i, 0))
    per_layer = lambda *shape: pl.BlockSpec((None,) + shape, lambda b, i: (layer,) + (0,) * len(shape))
    sds = lambda w, dt: jax.ShapeDtypeStruct((DEC_BATCH, DEC_SEQ, w), dt)
    edge = step_rows // 8
    last = DEC_SEQ // 8 - 1
    return pl.pallas_call(
        _lat_proj_kernel,
        grid=(DEC_BATCH, N_TILES // PROJ_TILES),
        in_specs=[
            tile(D_MODEL),
            pl.BlockSpec((None, 8, D_MODEL), lambda b, i: (b, jnp.maximum(i * edge - 1, 0), 0)),
            pl.BlockSpec((None, 8, D_MODEL), lambda b, i: (b, jnp.minimum((i + 1) * edge, last), 0)),
            per_layer(3, N_GROUPS, D_MODEL),
            per_layer(1, D_MODEL),
            per_layer(D_MODEL, D_IN),
            pl.BlockSpec((step_rows, LANES), lambda b, i: (i, 0)),
            pl.BlockSpec((step_rows, LANES), lambda b, i: (i, 0)),
            per_layer(CHUNK, A_HEADS * CHUNK),
            per_layer(CHUNK, GROUP_W),
            per_layer(3, GROUP_W),
        ],
        out_specs=[tile(512), tile(512)] + [tile(256)] * 6,
        out_shape=[sds(512, BF16), sds(512, F32)] + [sds(256, BF16)] * 6,
        compiler_params=pltpu.CompilerParams(dimension_semantics=("arbitrary", "arbitrary"),
                                             vmem_limit_bytes=VMEM_LIMIT),
        name="lat_proj",
    )(x, x, x, mod, norm_g, w_in, cos_t, sin_t, ws_cat, bias_a, w_conv)


_C_GROUPS = (((0, 0), (1, 1)), ((0, 1), (1, 0)))
_D_GROUPS = (((0, 0), (0, 1)), ((1, 0), (1, 1)))


def _stacked_queries(q_ref, rows, low, groups):
    return [jnp.concatenate([_mask_half(q_ref[rows, t * LANES:(t + 1) * LANES], low, half)
                             for t, half in group], axis=0) for group in groups]


def _lat_mix_kernel(layer, sink_ref, x_ref, mod_ref, fg_ref, yab_ref, sz_ref,
                    qc_ref, kc_ref, kcp_ref, kcn_ref, vc_ref, vcp_ref, vcn_ref,
                    qd_ref, kd_ref, kdp_ref, kdn_ref, vd_ref, vdp_ref, vdn_ref,
                    cck_ref, ccv_ref, cdk_ref, cdv_ref, cmask_ref, tab_ref, w_out_ref,
                    xo_ref,
                    kce, vce, kde, vde, yc_scr, yd_scr, m_ctx, acc_ctx, *stage_bufs):
    b = pl.program_id(0)
    i = pl.program_id(1)
    s_c, s_d = stage_bufs[0:2], stage_bufs[2:4]

    def file_context(m, acc, group, step_rows, n_steps):
        for which in range(2):
            for k in range(n_steps):
                src = slice(which * TILE + k * step_rows, which * TILE + (k + 1) * step_rows)
                dst = slice((4 * k + 2 * group + which) * step_rows, (4 * k + 2 * group + which + 1) * step_rows)
                m_ctx[dst, :] = m[src]
                acc_ctx[dst, :] = acc[src]

    kce[0:QBLK, :] = kcp_ref[...]
    kce[QBLK:QBLK + TILE, :] = kc_ref[...]
    kce[QBLK + TILE:, :] = kcn_ref[...]
    for c_src, c_dst in ((0, 0), (LANES, 2 * LANES)):
        vce[0:QBLK, c_dst:c_dst + LANES] = vcp_ref[:, c_src:c_src + LANES]
        vce[QBLK:QBLK + TILE, c_dst:c_dst + LANES] = vc_ref[:, c_src:c_src + LANES]
        vce[QBLK + TILE:, c_dst:c_dst + LANES] = vcn_ref[:, c_src:c_src + LANES]
    vce[:, LANES:2 * LANES] = jnp.ones((TILE + 2 * QBLK, LANES), BF16)
    cck = cck_ref[...]
    ccv = ccv_ref[...]
    ones_ctx = jnp.ones((PAST_LEN, LANES), BF16)
    k_ctx = (cck.astype(BF16), _swap64(cck).astype(BF16))
    v_ctx = jnp.concatenate([ccv.astype(BF16), ones_ctx, _swap64(ccv).astype(BF16)], axis=1)
    low_q = _low_half(QBLK)
    low_t = _low_half(TILE)
    n_blocks = DEC_SEQ // QBLK
    blocks_per_tile = TILE // QBLK

    for sw, q in enumerate(_stacked_queries(qc_ref, slice(None), low_t, _C_GROUPS)):
        m, acc = _context_pass(q, k_ctx[sw], v_ctx[:, sw * LANES:(sw + 2) * LANES])
        file_context(m, acc, sw, QBLK, blocks_per_tile)

    def c_scores(j):
        n = i * blocks_per_tile + j
        window = cmask_ref[jnp.where(n == 0, 1, jnp.where(n == n_blocks - 1, 2, 0))]
        window = jnp.concatenate([window, window], axis=0)
        rows = slice(j * QBLK, (j + 1) * QBLK)
        for sw, q in enumerate(_stacked_queries(qc_ref, rows, low_q, _C_GROUPS)):
            k_loc = kce[j * QBLK:(j + 3) * QBLK, sw * LANES:(sw + 1) * LANES]
            s_c[j % 2][2 * sw * QBLK:2 * (sw + 1) * QBLK, :] = _dot_nt(q, k_loc) + window

    def c_softmax(j):
        step = slice(4 * j * QBLK, 4 * (j + 1) * QBLK)
        sink = jnp.concatenate([jnp.full((QBLK, 1), sink_ref[layer, 2 * t + half] * LOG2E, F32)
                                for group in _C_GROUPS for t, half in group], axis=0)
        e, ctx_scale, sink_term = _local_softmax(s_c[j % 2][...], m_ctx[step, :], sink)
        values = [(vce[j * QBLK:(j + 3) * QBLK, sw * LANES:(sw + 2) * LANES], sw) for sw in range(2)]
        r0, r1 = _local_output(e, values, acc_ctx[step, :], ctx_scale, sink_term)
        yc_scr[j * QBLK:(j + 1) * QBLK, :] = jnp.concatenate(
            [jnp.where(low_q, r0[0:QBLK], r1[0:QBLK]), jnp.where(low_q, r1[QBLK:], r0[QBLK:])], axis=1)

    _skewed(blocks_per_tile, (c_scores, c_softmax))

    halo = 4 * GRID_W
    kde[0:halo, :] = kdp_ref[...]
    kde[halo:halo + TILE, :] = kd_ref[...]
    kde[halo + TILE:, :] = kdn_ref[...]
    for c_src, c_dst in ((0, 0), (LANES, 2 * LANES)):
        vde[0:halo, c_dst:c_dst + LANES] = vdp_ref[:, c_src:c_src + LANES]
        vde[halo:halo + TILE, c_dst:c_dst + LANES] = vd_ref[:, c_src:c_src + LANES]
        vde[halo + TILE:, c_dst:c_dst + LANES] = vdn_ref[:, c_src:c_src + LANES]
    vde[:, LANES:2 * LANES] = jnp.ones((TILE + 2 * halo, LANES), BF16)
    kd_ctx = cdk_ref[...].astype(BF16)
    cdv = cdv_ref[...].astype(BF16)
    vd_ctx = jnp.concatenate([cdv[:, 0:LANES], ones_ctx, cdv[:, LANES:2 * LANES]], axis=1)
    low_r = _low_half(GRID_W)
    n_rows = DEC_SEQ // GRID_W

    for t, q in enumerate(_stacked_queries(qd_ref, slice(None), low_t, _D_GROUPS)):
        m, acc = _context_pass(q, kd_ctx[:, t * LANES:(t + 1) * LANES], vd_ctx[:, t * LANES:(t + 2) * LANES])
        file_context(m, acc, t, GRID_W, ROWS_PER_TILE)

    def window_start(rl):
        r = i * ROWS_PER_TILE + rl
        rs = jnp.clip(r - WIN_R // 2, 0, n_rows - WIN_R)
        return rs - r + (WIN_R - 1), pl.multiple_of((rs - i * ROWS_PER_TILE + 4) * GRID_W, GRID_W)

    def d_scores(rl):
        didx, off = window_start(rl)
        rows = slice(rl * GRID_W, (rl + 1) * GRID_W)
        for t, q in enumerate(_stacked_queries(qd_ref, rows, low_r, _D_GROUPS)):
            k_loc = kde[pl.ds(off, WIN_R * GRID_W), t * LANES:(t + 1) * LANES]
            bias = jnp.concatenate([tab_ref[2 * t, didx], tab_ref[2 * t + 1, didx]], axis=0)
            s_d[rl % 2][2 * t * GRID_W:2 * (t + 1) * GRID_W, :] = _dot_nt(q, k_loc) + bias

    def d_softmax(rl):
        _, off = window_start(rl)
        step = slice(4 * rl * GRID_W, 4 * (rl + 1) * GRID_W)
        e, ctx_scale, _ = _local_softmax(s_d[rl % 2][...], m_ctx[step, :])
        values = [(vde[pl.ds(off, WIN_R * GRID_W), t * LANES:(t + 2) * LANES], t) for t in range(2)]
        r = _local_output(e, values, acc_ctx[step, :], ctx_scale)
        yd_scr[rl * GRID_W:(rl + 1) * GRID_W, :] = jnp.concatenate(
            [jnp.where(low_r, r[t][0:GRID_W], r[t][GRID_W:]) for t in range(2)], axis=1)

    _skewed(ROWS_PER_TILE, (d_scores, d_softmax))

    yc = (yc_scr[...] * sz_ref[:, 0:256]).astype(BF16)
    yd = (yd_scr[...] * sz_ref[:, 256:512]).astype(BF16)
    y = (_dot(yab_ref[...], w_out_ref[0:2 * GROUP_W, :]) + _dot(yc, w_out_ref[2 * GROUP_W:3 * GROUP_W, :])
         + _dot(yd, w_out_ref[3 * GROUP_W:4 * GROUP_W, :]))
    x_new = x_ref[...] + mod_ref[2, pl.ds(1 + b, 1), :] * y
    if layer == DEPTH - 1:
        x_new = _rms(x_new) * fg_ref[...]
    xo_ref[...] = x_new


def _lat_mix(layer, x, mod, final_g, yab, sz, qc, kc, vc, qd, kd, vd, cck, ccv, cdk, cdv, tab, w_out, sink):
    tile = lambda w: pl.BlockSpec((None, TILE, w), lambda b, i: (b, i, 0))
    per_layer = lambda *shape: pl.BlockSpec((None,) + shape, lambda b, i: (layer,) + (0,) * len(shape))

    def halo(rows, w):
        per = TILE // rows
        last = DEC_SEQ // rows - 1
        prev = pl.BlockSpec((None, rows, w), lambda b, i: (b, jnp.maximum(i * per - 1, 0), 0))
        nxt = pl.BlockSpec((None, rows, w), lambda b, i: (b, jnp.minimum((i + 1) * per, last), 0))
        return prev, nxt

    cprev, cnext = halo(QBLK, 256)
    dprev, dnext = halo(4 * GRID_W, 256)
    cache = lambda w: pl.BlockSpec((None, None, PAST_LEN, w), lambda b, i: (b, layer, 0, 0))
    return pl.pallas_call(
        functools.partial(_lat_mix_kernel, layer),
        grid=(DEC_BATCH, N_TILES),
        in_specs=[
            pl.BlockSpec(memory_space=pltpu.SMEM),
            tile(D_MODEL),
            per_layer(3, N_GROUPS, D_MODEL),
            pl.BlockSpec((1, D_MODEL), lambda b, i: (0, 0)),
            tile(512), tile(512),
            tile(256), tile(256), cprev, cnext, tile(256), cprev, cnext,
            tile(256), tile(256), dprev, dnext, tile(256), dprev, dnext,
            cache(128), cache(128), cache(256), cache(256),
            pl.BlockSpec((3, QBLK, 3 * QBLK), lambda b, i: (0, 0, 0)),
            per_layer(4, WIN_R, GRID_W, WIN_R * GRID_W),
            per_layer(D_MODEL, D_MODEL),
        ],
        out_specs=tile(D_MODEL),
        out_shape=jax.ShapeDtypeStruct((DEC_BATCH, DEC_SEQ, D_MODEL), F32),
        scratch_shapes=[
            pltpu.VMEM((TILE + 2 * QBLK, 2 * LANES), BF16),
            pltpu.VMEM((TILE + 2 * QBLK, 3 * LANES), BF16),
            pltpu.VMEM((TILE + 8 * GRID_W, 2 * LANES), BF16),
            pltpu.VMEM((TILE + 8 * GRID_W, 3 * LANES), BF16),
            pltpu.VMEM((TILE, 256), F32),
            pltpu.VMEM((TILE, 256), F32),
            pltpu.VMEM((4 * TILE, 1), F32),
            pltpu.VMEM((4 * TILE, 2 * LANES), F32),
        ] + 2 * [pltpu.VMEM((4 * QBLK, 3 * QBLK), F32)] + 2 * [pltpu.VMEM((4 * GRID_W, WIN_R * GRID_W), F32)],
        compiler_params=pltpu.CompilerParams(dimension_semantics=("arbitrary", "arbitrary"),
                                             vmem_limit_bytes=VMEM_LIMIT),
        name="lat_mix",
    )(sink, x, mod, final_g, yab, sz, qc, kc, kc, kc, vc, vc, vc,
      qd, kd, kd, kd, vd, vd, vd, cck, ccv, cdk, cdv, _window_masks(), tab, w_out)


def _window_masks():
    p = np.arange(QBLK)[:, None]
    j = np.arange(3 * QBLK)[None, :]
    band = np.abs(j - QBLK - p) <= WINDOW
    masks = [band, band & (j >= QBLK), band & (j < 2 * QBLK)]
    return jnp.asarray(np.where(np.stack(masks), 0.0, NEG), F32)


def _rope_tables():
    t = np.arange(DEC_SEQ)
    freqs = (np.float32(ROPE_BASE) ** (-np.arange(16, dtype=np.float32) / np.float32(16))).astype(np.float32)
    ang_r = (t // GRID_W).astype(np.float32)[:, None] * freqs
    ang_c = (t % GRID_W).astype(np.float32)[:, None] * freqs
    cos_h = np.concatenate([np.cos(ang_r), np.cos(ang_r), np.cos(ang_c), np.cos(ang_c)], axis=1)
    sin_h = np.concatenate([-np.sin(ang_r), np.sin(ang_r), -np.sin(ang_c), np.sin(ang_c)], axis=1)
    return (jnp.asarray(np.tile(cos_h, (1, 2)), F32), jnp.asarray(np.tile(sin_h, (1, 2)), F32))


def kernel(x_prompt, x_sample, cache_c_k, cache_c_v, cache_d_k, cache_d_v, c, c_ctx, norm_g, w_mod, b_mod,
           w_in, w_out, w_s, b_s, w_conv, sink, rpb, final_g):
    w_in_b = w_in.astype(BF16)
    w_out_b = w_out.astype(BF16)
    ws_cat = jnp.transpose(w_s, (0, 2, 1, 3)).reshape(DEPTH, CHUNK, A_HEADS * CHUNK).astype(BF16)
    bias_a = jnp.repeat(jnp.transpose(b_s, (0, 2, 1)), HEAD_DIM, axis=2)
    norm_g3 = norm_g.reshape(DEPTH, 1, D_MODEL)
    fg = final_g.reshape(1, D_MODEL)
    cos_t, sin_t = _rope_tables()
    cck = cache_c_k.reshape(DEC_BATCH, DEPTH, PAST_LEN, 128)
    ccv = cache_c_v.reshape(DEC_BATCH, DEPTH, PAST_LEN, 128)
    cdk = cache_d_k.reshape(DEC_BATCH, DEPTH, PAST_LEN, 256)
    cdv = cache_d_v.reshape(DEC_BATCH, DEPTH, PAST_LEN, 256)

    cond = jnp.concatenate([c_ctx[None, :], c, jnp.zeros((N_GROUPS - 1 - DEC_BATCH, D_MODEL), F32)], axis=0)
    mod = _modulation(cond, w_mod, b_mod)
    tab = _nbr_tables(rpb)

    y_prompt, s_ck, s_cv, s_dk, s_dv = _ctx_layers(x_prompt, mod, norm_g3, fg, w_in_b, w_out_b, ws_cat,
                                                   bias_a, w_conv, sink)
    xs = x_sample
    for l in range(DEPTH):
        yab, sz, qc, kc, vc, qd, kd, vd = _lat_proj(l, xs, mod, norm_g3, w_in_b, cos_t, sin_t,
                                                    ws_cat, bias_a, w_conv)
        xs = _lat_mix(l, xs, mod, fg, yab, sz, qc, kc, vc, qd, kd, vd, cck, ccv, cdk, cdv, tab, w_out_b, sink)

    shape_c = (BATCH, DEPTH, SEQ, 2, HEAD_DIM)
    shape_d = (BATCH, DEPTH, SEQ, 4, HEAD_DIM)
    return (y_prompt, xs, s_ck.reshape(shape_c), s_cv.reshape(shape_c), s_dk.reshape(shape_d),
            s_dv.reshape(shape_d))
```

```python
import functools

import numpy as np
import jax
import jax.numpy as jnp
from jax import lax
from jax.experimental import pallas as pl
from jax.experimental.pallas import tpu as pltpu

D_MODEL = 1024
BATCH = 16
SEQ = 256
DEPTH = 4
DEC_BATCH = 2
DEC_SEQ = 4096
PAST_LEN = 512
GRID_W = 64
GROUP_W = 256
HEAD_DIM = 64
A_HEADS = 4
CHUNK = 128
WINDOW = 128
QBLK = 128
WIN_R = 8
WIN_C = 16
ROPE_BASE = 10000.0
EPS = 1e-6
NEG = -1e30
D_IN = 3584
LOG2E = 1.4426950408889634
Q_SCALE = HEAD_DIM ** -0.5 * LOG2E

AU, AV, AZ, BB, BC, BH, BZ = 0, 256, 512, 768, 1024, 1280, 1536
CQ, CKV, CZ, DQ, DK, DV, DZ = 1792, 2048, 2304, 2560, 2816, 3072, 3328

LANES = 128
TILE = 512
ROWS_PER_TILE = TILE // GRID_W
N_TILES = DEC_SEQ // TILE
CTX_SEQS = 4
PROJ_TILES = 2
N_GROUPS = 8
VMEM_LIMIT = 56 * 1024 * 1024

F32 = jnp.float32
BF16 = jnp.bfloat16


def _silu(z):
    return z * (1.0 / (1.0 + jnp.exp(-z)))


def _dot(a, b):
    return jnp.dot(a, b, preferred_element_type=F32)


def _dot_nt(a, b):
    return lax.dot_general(a, b, (((1,), (1,)), ((), ())), preferred_element_type=F32)


def _rms(x):
    return x * lax.rsqrt(jnp.mean(x * x, axis=-1, keepdims=True) + EPS)


def _norm_mod(x, g, shift, scale):
    return _rms(x) * (g * (1.0 + scale)) + shift


def _swap64(x):
    return pltpu.roll(x, HEAD_DIM, axis=1)


def _group_mean_matrix():
    r = lax.broadcasted_iota(jnp.int32, (GROUP_W, GROUP_W), 0) // HEAD_DIM
    c = lax.broadcasted_iota(jnp.int32, (GROUP_W, GROUP_W), 1) // HEAD_DIM
    return jnp.where(r == c, 1.0 / HEAD_DIM, 0.0).astype(BF16)


def _branch_a(au, av, az, ws, bias):
    t = av.shape[0]
    sq = av * av
    hi = sq.astype(BF16)
    lo = (sq - hi.astype(F32)).astype(BF16)
    gm = _group_mean_matrix()
    ms = _dot(hi, gm) + _dot(lo, gm)
    vh = (av * lax.rsqrt(ms + EPS)).astype(BF16)
    head = lax.broadcasted_iota(jnp.int32, (CHUNK, GROUP_W), 1) // HEAD_DIM
    outs = []
    for n in range(t // CHUNK):
        v = vh[n * CHUNK:(n + 1) * CHUNK]
        rhs = jnp.concatenate([jnp.where(head == h, v, jnp.zeros_like(v)) for h in range(A_HEADS)], axis=0)
        outs.append(_dot(ws, rhs) + bias)
    mixed = jnp.concatenate(outs, axis=0)
    return au * mixed * _silu(az)


def _branch_b(bb, bc, bh, bz, wc, prev_row, next_row):
    t = bb.shape[0]
    xc = bc * bh
    row = lax.broadcasted_iota(jnp.int32, xc.shape, 0)
    xm = jnp.where(row == 0, prev_row, pltpu.roll(xc, 1, axis=0))
    xp = jnp.where(row == t - 1, next_row, pltpu.roll(xc, t - 1, axis=0))
    y = wc[0:1, :] * xm + wc[1:2, :] * xc + wc[2:3, :] * xp
    return bb * y * _silu(bz)


def _attend(score, value, pos, sink=None):
    m = jnp.max(score, axis=-1, keepdims=True)
    if sink is not None:
        m = jnp.maximum(m, sink * LOG2E)
    acc = _dot(jnp.exp2(score - m).astype(BF16), value)
    o = acc[:, pos * LANES:(pos + 1) * LANES]
    l = acc[:, (1 - pos) * LANES:(2 - pos) * LANES]
    if sink is not None:
        l = l + jnp.exp2(sink * LOG2E - m)
    return o * (1.0 / l)


def _context_pass(q_stacked, k, value):
    s = _dot_nt(q_stacked, k)
    m = jnp.max(s, axis=-1, keepdims=True)
    return m, _dot(jnp.exp2(s - m).astype(BF16), value)


def _local_softmax(s_loc, m_ctx, sink=None):
    m = jnp.maximum(jnp.max(s_loc, axis=-1, keepdims=True), m_ctx)
    if sink is not None:
        m = jnp.maximum(m, sink)
    e = jnp.exp2(s_loc - m).astype(BF16)
    return e, jnp.exp2(m_ctx - m), (None if sink is None else jnp.exp2(sink - m))


def _local_output(e, values, acc_ctx, ctx_scale, sink_term=None):
    rows = e.shape[0] // len(values)
    outs = []
    for p, (v, pos) in enumerate(values):
        r = slice(p * rows, (p + 1) * rows)
        acc = _dot(e[r], v) + acc_ctx[r] * ctx_scale[r]
        o = acc[:, pos * LANES:(pos + 1) * LANES]
        l = acc[:, (1 - pos) * LANES:(2 - pos) * LANES]
        if sink_term is not None:
            l = l + sink_term[r]
        outs.append(o * (1.0 / l))
    return outs


def _skewed(n_steps, stages):
    for tick in range(n_steps + len(stages) - 1):
        for lag, stage in enumerate(stages):
            if 0 <= tick - lag < n_steps:
                stage(tick - lag)


def _low_half(m):
    return lax.broadcasted_iota(jnp.int32, (m, LANES), 1) < HEAD_DIM


def _mask_half(q, low, half):
    z = jnp.zeros_like(q)
    return jnp.where(low, q, z) if half == 0 else jnp.where(low, z, q)


def _out_proj(x, gate, ya, yb, yc, yd, w_out_ref):
    y = _dot(ya.astype(BF16), w_out_ref[0:256, :])
    y += _dot(yb.astype(BF16), w_out_ref[256:512, :])
    y += _dot(yc.astype(BF16), w_out_ref[512:768, :])
    y += _dot(yd.astype(BF16), w_out_ref[768:1024, :])
    return x + gate * y


def _mod_kernel(cond_ref, w_ref, b_ref, o_ref):
    mod = _dot(_silu(cond_ref[...]).astype(BF16), w_ref[...].astype(BF16)) + b_ref[...]
    for j in range(3):
        o_ref[j] = mod[:, j * D_MODEL:(j + 1) * D_MODEL]


def _modulation(cond, w_mod, b_mod):
    return pl.pallas_call(
        _mod_kernel,
        grid=(DEPTH,),
        in_specs=[
            pl.BlockSpec((N_GROUPS, D_MODEL), lambda l: (0, 0)),
            pl.BlockSpec((None, D_MODEL, 3 * D_MODEL), lambda l: (l, 0, 0)),
            pl.BlockSpec((None, 1, 3 * D_MODEL), lambda l: (l, 0, 0)),
        ],
        out_specs=pl.BlockSpec((None, 3, N_GROUPS, D_MODEL), lambda l: (l, 0, 0, 0)),
        out_shape=jax.ShapeDtypeStruct((DEPTH, 3, N_GROUPS, D_MODEL), F32),
        compiler_params=pltpu.CompilerParams(dimension_semantics=("arbitrary",),
                                             vmem_limit_bytes=VMEM_LIMIT),
        name="modulation",
    )(cond, w_mod, b_mod.reshape(DEPTH, 1, 3 * D_MODEL))


def _nbr_table_kernel(rpb_ref, o_ref):
    c = lax.broadcasted_iota(jnp.int32, (GRID_W, LANES), 0)
    lane = lax.broadcasted_iota(jnp.int32, (GRID_W, LANES), 1)
    cc = lane % GRID_W
    cstart = jnp.clip(c - WIN_C // 2, 0, GRID_W - WIN_C)
    for h in range(4):
        tiles = [pltpu.roll(jnp.broadcast_to(rpb_ref[h, ro:ro + 1, :] * LOG2E, (GRID_W, LANES)),
                            LANES - (WIN_C - 1), 1, stride=1, stride_axis=0)
                 for ro in range(2 * WIN_R - 2)]
        for d in range(WIN_R):
            for p in range(WIN_R // 2):
                v = tiles[d + 2 * p]
                v = jnp.where(cc >= cstart, jnp.where(cc < cstart + WIN_C, v, NEG), NEG)
                o_ref[h, d, :, p * LANES:(p + 1) * LANES] = v


def _nbr_tables(rpb):
    half = LANES // 2 - (2 * WIN_C - 1)
    rows = jnp.pad(rpb, ((0, 0), (0, 0), (0, 1), (0, half)))
    rpb_p = jnp.concatenate([rows[:, :, :-1], rows[:, :, 1:]], axis=-1)
    rpb_p = jnp.pad(rpb_p, ((0, 0), (0, 0), (0, 1), (0, 0)))
    return pl.pallas_call(
        _nbr_table_kernel,
        grid=(DEPTH,),
        in_specs=[pl.BlockSpec((None, 4, 2 * WIN_R, LANES), lambda l: (l, 0, 0, 0))],
        out_specs=pl.BlockSpec((None, 4, WIN_R, GRID_W, WIN_R * GRID_W), lambda l: (l, 0, 0, 0, 0)),
        out_shape=jax.ShapeDtypeStruct((DEPTH, 4, WIN_R, GRID_W, WIN_R * GRID_W), F32),
        compiler_params=pltpu.CompilerParams(dimension_semantics=("arbitrary",)),
        name="nbr_tables",
    )(rpb_p)


def _ctx_kernel(sink_ref, x_ref, mod_ref, g_ref, fg_ref, w_in_ref, w_out_ref, ws_ref, ba_ref, wc_ref,
                y_ref, ck_ref, cv_ref, dk_ref, dv_ref, xs):
    l = pl.program_id(0)
    pair = pl.program_id(1)

    @pl.when(l == 0)
    def _():
        for u in range(CTX_SEQS):
            xs[CTX_SEQS * pair + u] = x_ref[u]

    low = _low_half(SEQ)
    ones = jnp.ones((SEQ, LANES), BF16)

    def project(u):
        x = xs[CTX_SEQS * pair + u]
        h = _norm_mod(x, g_ref[...], mod_ref[0, 0:1, :], mod_ref[1, 0:1, :]).astype(BF16)
        return x, _dot(h, w_in_ref[...])

    def mix(u, p):
        kcf = p[:, CKV:CKV + 128]
        vcf = p[:, CKV + 128:CKV + 256]
        kdf = p[:, DK:DK + 256]
        vdf = p[:, DV:DV + 256]
        ck_ref[u] = kcf
        cv_ref[u] = vcf
        dk_ref[u] = kdf
        dv_ref[u] = vdf

        ya = _branch_a(p[:, AU:AU + 256], p[:, AV:AV + 256], p[:, AZ:AZ + 256], ws_ref[...], ba_ref[...])
        zero_row = jnp.zeros((1, GROUP_W), F32)
        yb = _branch_b(p[:, BB:BB + 256], p[:, BC:BC + 256], p[:, BH:BH + 256], p[:, BZ:BZ + 256],
                       wc_ref[...], zero_row, zero_row)

        kc = (kcf.astype(BF16), _swap64(kcf).astype(BF16))
        vc = (jnp.concatenate([vcf.astype(BF16), ones], axis=1),
              jnp.concatenate([ones, _swap64(vcf).astype(BF16)], axis=1))
        o = []
        for t in range(2):
            q = (p[:, CQ + t * LANES:CQ + (t + 1) * LANES] * Q_SCALE).astype(BF16)
            for half in range(2):
                sw = (t + half) % 2
                o.append(_attend(_dot_nt(_mask_half(q, low, half), kc[sw]), vc[sw], sw,
                                 sink_ref[l, 2 * t + half]))
        yc = jnp.concatenate([jnp.where(low, o[0], o[1]), jnp.where(low, o[2], o[3])], axis=1)
        yc = yc * _silu(p[:, CZ:CZ + 256])

        o = []
        for t in range(2):
            q = (p[:, DQ + t * LANES:DQ + (t + 1) * LANES] * Q_SCALE).astype(BF16)
            k = kdf[:, t * LANES:(t + 1) * LANES].astype(BF16)
            v = jnp.concatenate([vdf[:, t * LANES:(t + 1) * LANES].astype(BF16), ones], axis=1)
            for half in range(2):
                o.append(_attend(_dot_nt(_mask_half(q, low, half), k), v, 0))
        yd = jnp.concatenate([jnp.where(low, o[0], o[1]), jnp.where(low, o[2], o[3])], axis=1)
        yd = yd * _silu(p[:, DZ:DZ + 256])
        return ya, yb, yc, yd

    def finish(u, x, ys):
        x_new = _out_proj(x, mod_ref[2, 0:1, :], *ys, w_out_ref)
        xs[CTX_SEQS * pair + u] = x_new
        return x_new

    staged = [project(0)]
    x_new = []
    for u in range(CTX_SEQS):
        if u + 1 < CTX_SEQS:
            staged.append(project(u + 1))
        x, p = staged[u]
        x_new.append(finish(u, x, mix(u, p)))

    @pl.when(l == DEPTH - 1)
    def _():
        for u in range(CTX_SEQS):
            y_ref[u] = _rms(x_new[u]) * fg_ref[...]


def _ctx_layers(x, mod, norm_g, final_g, w_in, w_out, ws_cat, bias_a, w_conv, sink):
    per_layer = lambda *shape: pl.BlockSpec((None,) + shape, lambda l, b: (l,) + (0,) * len(shape))
    state = lambda w: pl.BlockSpec((CTX_SEQS, None, SEQ, w), lambda l, b: (b, l, 0, 0))
    n_steps = BATCH // CTX_SEQS
    return pl.pallas_call(
        _ctx_kernel,
        grid=(DEPTH, n_steps),
        in_specs=[
            pl.BlockSpec(memory_space=pltpu.SMEM),
            pl.BlockSpec((CTX_SEQS, SEQ, D_MODEL), lambda l, b: (jnp.where(l == 0, b, n_steps - 1), 0, 0)),
            per_layer(3, N_GROUPS, D_MODEL),
            per_layer(1, D_MODEL),
            pl.BlockSpec((1, D_MODEL), lambda l, b: (0, 0)),
            pl.BlockSpec((None, D_MODEL, D_IN), lambda l, b: (l, 0, 0), pipeline_mode=pl.Buffered(1)),
            pl.BlockSpec((None, D_MODEL, D_MODEL), lambda l, b: (l, 0, 0), pipeline_mode=pl.Buffered(1)),
            per_layer(CHUNK, A_HEADS * CHUNK),
            per_layer(CHUNK, GROUP_W),
            per_layer(3, GROUP_W),
        ],
        out_specs=[
            pl.BlockSpec((CTX_SEQS, SEQ, D_MODEL), lambda l, b: (jnp.where(l == DEPTH - 1, b, 0), 0, 0)),
            state(128), state(128), state(256), state(256),
        ],
        out_shape=[
            jax.ShapeDtypeStruct((BATCH, SEQ, D_MODEL), F32),
            jax.ShapeDtypeStruct((BATCH, DEPTH, SEQ, 128), F32),
            jax.ShapeDtypeStruct((BATCH, DEPTH, SEQ, 128), F32),
            jax.ShapeDtypeStruct((BATCH, DEPTH, SEQ, 256), F32),
            jax.ShapeDtypeStruct((BATCH, DEPTH, SEQ, 256), F32),
        ],
        scratch_shapes=[pltpu.VMEM((BATCH, SEQ, D_MODEL), F32)],
        compiler_params=pltpu.CompilerParams(dimension_semantics=("arbitrary", "arbitrary"),
                                             vmem_limit_bytes=VMEM_LIMIT),
        name="ctx_layers",
    )(sink, x, mod, norm_g, final_g, w_in, w_out, ws_cat, bias_a, w_conv)


def _rope(x, cos, sin_signed, first_half):
    swapped = jnp.where(first_half, pltpu.roll(x, LANES - 16, axis=1), pltpu.roll(x, 16, axis=1))
    return x * cos + swapped * sin_signed


def _lat_proj_kernel(x_ref, xp_ref, xn_ref, mod_ref, g_ref, w_ref, cos_ref, sin_ref, ws_ref, ba_ref, wc_ref,
                     yab_ref, sz_ref, qc_ref, kc_ref, vc_ref, qd_ref, kd_ref, vd_ref):
    i = pl.program_id(1)
    grp = pl.ds(1 + pl.program_id(0), 1)
    g = g_ref[...]
    shift = mod_ref[0, grp, :]
    scale = mod_ref[1, grp, :]
    first_half = (lax.broadcasted_iota(jnp.int32, (TILE, LANES), 1) % 32) < 16

    def normalised(x):
        return _norm_mod(x, g, shift, scale).astype(BF16)

    def project(u, h, after_first_dot):
        rows = slice(u * TILE, (u + 1) * TILE)
        tile_index = PROJ_TILES * i + u
        pa = _dot(h, w_ref[:, AU:AU + 3 * GROUP_W])
        after_first_dot()
        ya = _branch_a(pa[:, 0:256], pa[:, 256:512], pa[:, 512:768], ws_ref[...], ba_ref[...])
        yab_ref[rows, 0:GROUP_W] = ya.astype(BF16)

        pb = _dot(h, w_ref[:, BB:BB + 4 * GROUP_W])
        before = xp_ref[...] if u == 0 else x_ref[u * TILE - 8:u * TILE, :]
        after = xn_ref[...] if u == PROJ_TILES - 1 else x_ref[(u + 1) * TILE:(u + 1) * TILE + 8, :]
        e = _dot(normalised(jnp.concatenate([before, after], axis=0)), w_ref[:, BC:BC + 2 * GROUP_W])
        prev_row = jnp.where(tile_index > 0, e[7:8, 0:256] * e[7:8, 256:512], 0.0)
        next_row = jnp.where(tile_index < N_TILES - 1, e[8:9, 0:256] * e[8:9, 256:512], 0.0)
        yb = _branch_b(pb[:, 0:256], pb[:, 256:512], pb[:, 512:768], pb[:, 768:1024], wc_ref[...],
                       prev_row, next_row)
        yab_ref[rows, GROUP_W:2 * GROUP_W] = yb.astype(BF16)

        sz_ref[rows, 0:256] = _silu(_dot(h, w_ref[:, CZ:CZ + 256]))
        sz_ref[rows, 256:512] = _silu(_dot(h, w_ref[:, DZ:DZ + 256]))
        cos = cos_ref[rows, :]
        sin = sin_ref[rows, :]
        qc = _dot(h, w_ref[:, CQ:CQ + 256])
        for t in range(2):
            qt = _rope(qc[:, t * LANES:(t + 1) * LANES], cos, sin, first_half) * Q_SCALE
            qc_ref[rows, t * LANES:(t + 1) * LANES] = qt.astype(BF16)
        kv = _dot(h, w_ref[:, CKV:CKV + 256])
        k = _rope(kv[:, 0:LANES], cos, sin, first_half)
        v = kv[:, LANES:2 * LANES]
        kc_ref[rows, 0:LANES] = k.astype(BF16)
        kc_ref[rows, LANES:2 * LANES] = _swap64(k).astype(BF16)
        vc_ref[rows, 0:LANES] = v.astype(BF16)
        vc_ref[rows, LANES:2 * LANES] = _swap64(v).astype(BF16)
        qd_ref[rows, :] = (_dot(h, w_ref[:, DQ:DQ + 256]) * Q_SCALE).astype(BF16)
        kd_ref[rows, :] = _dot(h, w_ref[:, DK:DK + 256]).astype(BF16)
        vd_ref[rows, :] = _dot(h, w_ref[:, DV:DV + 256]).astype(BF16)

    hs = [normalised(x_ref[0:TILE, :])]

    def normalise_next(u):
        if u + 1 < PROJ_TILES:
            hs.append(normalised(x_ref[(u + 1) * TILE:(u + 2) * TILE, :]))

    for u in range(PROJ_TILES):
        project(u, hs[u], functools.partial(normalise_next, u))


def _lat_proj(layer, x, mod, norm_g, w_in, cos_t, sin_t, ws_cat, bias_a, w_conv):
    step_rows = PROJ_TILES * TILE
    tile = lambda w: pl.BlockSpec((None, step_rows, w), lambda b, i: (b, i, 0))
    per_layer = lambda *shape: pl.BlockSpec((None,) + shape, lambda b, i: (layer,) + (0,) * len(shape))
    sds = lambda w, dt: jax.ShapeDtypeStruct((DEC_BATCH, DEC_SEQ, w), dt)
    edge = step_rows // 8
    last = DEC_SEQ // 8 - 1
    return pl.pallas_call(
        _lat_proj_kernel,
        grid=(DEC_BATCH, N_TILES // PROJ_TILES),
        in_specs=[
            tile(D_MODEL),
            pl.BlockSpec((None, 8, D_MODEL), lambda b, i: (b, jnp.maximum(i * edge - 1, 0), 0)),
            pl.BlockSpec((None, 8, D_MODEL), lambda b, i: (b, jnp.minimum((i + 1) * edge, last), 0)),
            per_layer(3, N_GROUPS, D_MODEL),
            per_layer(1, D_MODEL),
            per_layer(D_MODEL, D_IN),
            pl.BlockSpec((step_rows, LANES), lambda b, i: (i, 0)),
            pl.BlockSpec((step_rows, LANES), lambda b, i: (i, 0)),
            per_layer(CHUNK, A_HEADS * CHUNK),
            per_layer(CHUNK, GROUP_W),
            per_layer(3, GROUP_W),
        ],
        out_specs=[tile(512), tile(512)] + [tile(256)] * 6,
        out_shape=[sds(512, BF16), sds(512, F32)] + [sds(256, BF16)] * 6,
        compiler_params=pltpu.CompilerParams(dimension_semantics=("arbitrary", "arbitrary"),
                                             vmem_limit_bytes=VMEM_LIMIT),
        name="lat_proj",
    )(x, x, x, mod, norm_g, w_in, cos_t, sin_t, ws_cat, bias_a, w_conv)


_C_GROUPS = (((0, 0), (1, 1)), ((0, 1), (1, 0)))
_D_GROUPS = (((0, 0), (0, 1)), ((1, 0), (1, 1)))


def _stacked_queries(q_ref, rows, low, groups):
    return [jnp.concatenate([_mask_half(q_ref[rows, t * LANES:(t + 1) * LANES], low, half)
                             for t, half in group], axis=0) for group in groups]


def _lat_mix_kernel(layer, sink_ref, x_ref, mod_ref, fg_ref, yab_ref, sz_ref,
                    qc_ref, kc_ref, kcp_ref, kcn_ref, vc_ref, vcp_ref, vcn_ref,
                    qd_ref, kd_ref, kdp_ref, kdn_ref, vd_ref, vdp_ref, vdn_ref,
                    cck_ref, ccv_ref, cdk_ref, cdv_ref, cmask_ref, tab_ref, w_out_ref,
                    xo_ref,
                    kce, vce, kde, vde, yc_scr, yd_scr, m_ctx, acc_ctx, *stage_bufs):
    b = pl.program_id(0)
    i = pl.program_id(1)
    s_c, s_d = stage_bufs[0:2], stage_bufs[2:4]

    def file_context(m, acc, group, step_rows, n_steps):
        for which in range(2):
            for k in range(n_steps):
                src = slice(which * TILE + k * step_rows, which * TILE + (k + 1) * step_rows)
                dst = slice((4 * k + 2 * group + which) * step_rows, (4 * k + 2 * group + which + 1) * step_rows)
                m_ctx[dst, :] = m[src]
                acc_ctx[dst, :] = acc[src]

    kce[0:QBLK, :] = kcp_ref[...]
    kce[QBLK:QBLK + TILE, :] = kc_ref[...]
    kce[QBLK + TILE:, :] = kcn_ref[...]
    for c_src, c_dst in ((0, 0), (LANES, 2 * LANES)):
        vce[0:QBLK, c_dst:c_dst + LANES] = vcp_ref[:, c_src:c_src + LANES]
        vce[QBLK:QBLK + TILE, c_dst:c_dst + LANES] = vc_ref[:, c_src:c_src + LANES]
        vce[QBLK + TILE:, c_dst:c_dst + LANES] = vcn_ref[:, c_src:c_src + LANES]
    vce[:, LANES:2 * LANES] = jnp.ones((TILE + 2 * QBLK, LANES), BF16)
    cck = cck_ref[...]
    ccv = ccv_ref[...]
    ones_ctx = jnp.ones((PAST_LEN, LANES), BF16)
    k_ctx = (cck.astype(BF16), _swap64(cck).astype(BF16))
    v_ctx = jnp.concatenate([ccv.astype(BF16), ones_ctx, _swap64(ccv).astype(BF16)], axis=1)
    low_q = _low_half(QBLK)
    low_t = _low_half(TILE)
    n_blocks = DEC_SEQ // QBLK
    blocks_per_tile = TILE // QBLK

    for sw, q in enumerate(_stacked_queries(qc_ref, slice(None), low_t, _C_GROUPS)):
        m, acc = _context_pass(q, k_ctx[sw], v_ctx[:, sw * LANES:(sw + 2) * LANES])
        file_context(m, acc, sw, QBLK, blocks_per_tile)

    def c_scores(j):
        n = i * blocks_per_tile + j
        window = cmask_ref[jnp.where(n == 0, 1, jnp.where(n == n_blocks - 1, 2, 0))]
        window = jnp.concatenate([window, window], axis=0)
        rows = slice(j * QBLK, (j + 1) * QBLK)
        for sw, q in enumerate(_stacked_queries(qc_ref, rows, low_q, _C_GROUPS)):
            k_loc = kce[j * QBLK:(j + 3) * QBLK, sw * LANES:(sw + 1) * LANES]
            s_c[j % 2][2 * sw * QBLK:2 * (sw + 1) * QBLK, :] = _dot_nt(q, k_loc) + window

    def c_softmax(j):
        step = slice(4 * j * QBLK, 4 * (j + 1) * QBLK)
        sink = jnp.concatenate([jnp.full((QBLK, 1), sink_ref[layer, 2 * t + half] * LOG2E, F32)
                                for group in _C_GROUPS for t, half in group], axis=0)
        e, ctx_scale, sink_term = _local_softmax(s_c[j % 2][...], m_ctx[step, :], sink)
        values = [(vce[j * QBLK:(j + 3) * QBLK, sw * LANES:(sw + 2) * LANES], sw) for sw in range(2)]
        r0, r1 = _local_output(e, values, acc_ctx[step, :], ctx_scale, sink_term)
        yc_scr[j * QBLK:(j + 1) * QBLK, :] = jnp.concatenate(
            [jnp.where(low_q, r0[0:QBLK], r1[0:QBLK]), jnp.where(low_q, r1[QBLK:], r0[QBLK:])], axis=1)

    _skewed(blocks_per_tile, (c_scores, c_softmax))

    halo = 4 * GRID_W
    kde[0:halo, :] = kdp_ref[...]
    kde[halo:halo + TILE, :] = kd_ref[...]
    kde[halo + TILE:, :] = kdn_ref[...]
    for c_src, c_dst in ((0, 0), (LANES, 2 * LANES)):
        vde[0:halo, c_dst:c_dst + LANES] = vdp_ref[:, c_src:c_src + LANES]
        vde[halo:halo + TILE, c_dst:c_dst + LANES] = vd_ref[:, c_src:c_src + LANES]
        vde[halo + TILE:, c_dst:c_dst + LANES] = vdn_ref[:, c_src:c_src + LANES]
    vde[:, LANES:2 * LANES] = jnp.ones((TILE + 2 * halo, LANES), BF16)
    kd_ctx = cdk_ref[...].astype(BF16)
    cdv = cdv_ref[...].astype(BF16)
    vd_ctx = jnp.concatenate([cdv[:, 0:LANES], ones_ctx, cdv[:, LANES:2 * LANES]], axis=1)
    low_r = _low_half(GRID_W)
    n_rows = DEC_SEQ // GRID_W

    for t, q in enumerate(_stacked_queries(qd_ref, slice(None), low_t, _D_GROUPS)):
        m, acc = _context_pass(q, kd_ctx[:, t * LANES:(t + 1) * LANES], vd_ctx[:, t * LANES:(t + 2) * LANES])
        file_context(m, acc, t, GRID_W, ROWS_PER_TILE)

    def window_start(rl):
        r = i * ROWS_PER_TILE + rl
        rs = jnp.clip(r - WIN_R // 2, 0, n_rows - WIN_R)
        return rs - r + (WIN_R - 1), pl.multiple_of((rs - i * ROWS_PER_TILE + 4) * GRID_W, GRID_W)

    def d_scores(rl):
        didx, off = window_start(rl)
        rows = slice(rl * GRID_W, (rl + 1) * GRID_W)
        for t, q in enumerate(_stacked_queries(qd_ref, rows, low_r, _D_GROUPS)):
            k_loc = kde[pl.ds(off, WIN_R * GRID_W), t * LANES:(t + 1) * LANES]
            bias = jnp.concatenate([tab_ref[2 * t, didx], tab_ref[2 * t + 1, didx]], axis=0)
            s_d[rl % 2][2 * t * GRID_W:2 * (t + 1) * GRID_W, :] = _dot_nt(q, k_loc) + bias

    def d_softmax(rl):
        _, off = window_start(rl)
        step = slice(4 * rl * GRID_W, 4 * (rl + 1) * GRID_W)
        e, ctx_scale, _ = _local_softmax(s_d[rl % 2][...], m_ctx[step, :])
        values = [(vde[pl.ds(off, WIN_R * GRID_W), t * LANES:(t + 2) * LANES], t) for t in range(2)]
        r = _local_output(e, values, acc_ctx[step, :], ctx_scale)
        yd_scr[rl * GRID_W:(rl + 1) * GRID_W, :] = jnp.concatenate(
            [jnp.where(low_r, r[t][0:GRID_W], r[t][GRID_W:]) for t in range(2)], axis=1)

    _skewed(ROWS_PER_TILE, (d_scores, d_softmax))

    yc = (yc_scr[...] * sz_ref[:, 0:256]).astype(BF16)
    yd = (yd_scr[...] * sz_ref[:, 256:512]).astype(BF16)
    y = (_dot(yab_ref[...], w_out_ref[0:2 * GROUP_W, :]) + _dot(yc, w_out_ref[2 * GROUP_W:3 * GROUP_W, :])
         + _dot(yd, w_out_ref[3 * GROUP_W:4 * GROUP_W, :]))
    x_new = x_ref[...] + mod_ref[2, pl.ds(1 + b, 1), :] * y
    if layer == DEPTH - 1:
        x_new = _rms(x_new) * fg_ref[...]
    xo_ref[...] = x_new


def _lat_mix(layer, x, mod, final_g, yab, sz, qc, kc, vc, qd, kd, vd, cck, ccv, cdk, cdv, tab, w_out, sink):
    tile = lambda w: pl.BlockSpec((None, TILE, w), lambda b, i: (b, i, 0))
    per_layer = lambda *shape: pl.BlockSpec((None,) + shape, lambda b, i: (layer,) + (0,) * len(shape))

    def halo(rows, w):
        per = TILE // rows
        last = DEC_SEQ // rows - 1
        prev = pl.BlockSpec((None, rows, w), lambda b, i: (b, jnp.maximum(i * per - 1, 0), 0))
        nxt = pl.BlockSpec((None, rows, w), lambda b, i: (b, jnp.minimum((i + 1) * per, last), 0))
        return prev, nxt

    cprev, cnext = halo(QBLK, 256)
    dprev, dnext = halo(4 * GRID_W, 256)
    cache = lambda w: pl.BlockSpec((None, None, PAST_LEN, w), lambda b, i: (b, layer, 0, 0))
    return pl.pallas_call(
        functools.partial(_lat_mix_kernel, layer),
        grid=(DEC_BATCH, N_TILES),
        in_specs=[
            pl.BlockSpec(memory_space=pltpu.SMEM),
            tile(D_MODEL),
            per_layer(3, N_GROUPS, D_MODEL),
            pl.BlockSpec((1, D_MODEL), lambda b, i: (0, 0)),
            tile(512), tile(512),
            tile(256), tile(256), cprev, cnext, tile(256), cprev, cnext,
            tile(256), tile(256), dprev, dnext, tile(256), dprev, dnext,
            cache(128), cache(128), cache(256), cache(256),
            pl.BlockSpec((3, QBLK, 3 * QBLK), lambda b, i: (0, 0, 0)),
            per_layer(4, WIN_R, GRID_W, WIN_R * GRID_W),
            per_layer(D_MODEL, D_MODEL),
        ],
        out_specs=tile(D_MODEL),
        out_shape=jax.ShapeDtypeStruct((DEC_BATCH, DEC_SEQ, D_MODEL), F32),
        scratch_shapes=[
            pltpu.VMEM((TILE + 2 * QBLK, 2 * LANES), BF16),
            pltpu.VMEM((TILE + 2 * QBLK, 3 * LANES), BF16),
            pltpu.VMEM((TILE + 8 * GRID_W, 2 * LANES), BF16),
            pltpu.VMEM((TILE + 8 * GRID_W, 3 * LANES), BF16),
            pltpu.VMEM((TILE, 256), F32),
            pltpu.VMEM((TILE, 256), F32),
            pltpu.VMEM((4 * TILE, 1), F32),
            pltpu.VMEM((4 * TILE, 2 * LANES), F32),
        ] + 2 * [pltpu.VMEM((4 * QBLK, 3 * QBLK), F32)] + 2 * [pltpu.VMEM((4 * GRID_W, WIN_R * GRID_W), F32)],
        compiler_params=pltpu.CompilerParams(dimension_semantics=("arbitrary", "arbitrary"),
                                             vmem_limit_bytes=VMEM_LIMIT),
        name="lat_mix",
    )(sink, x, mod, final_g, yab, sz, qc, kc, kc, kc, vc, vc, vc,
      qd, kd, kd, kd, vd, vd, vd, cck, ccv, cdk, cdv, _window_masks(), tab, w_out)


def _window_masks():
    p = np.arange(QBLK)[:, None]
    j = np.arange(3 * QBLK)[None, :]
    band = np.abs(j - QBLK - p) <= WINDOW
    masks = [band, band & (j >= QBLK), band & (j < 2 * QBLK)]
    return jnp.asarray(np.where(np.stack(masks), 0.0, NEG), F32)


def _rope_tables():
    t = np.arange(DEC_SEQ)
    freqs = (np.float32(ROPE_BASE) ** (-np.arange(16, dtype=np.float32) / np.float32(16))).astype(np.float32)
    ang_r = (t // GRID_W).astype(np.float32)[:, None] * freqs
    ang_c = (t % GRID_W).astype(np.float32)[:, None] * freqs
    cos_h = np.concatenate([np.cos(ang_r), np.cos(ang_r), np.cos(ang_c), np.cos(ang_c)], axis=1)
    sin_h = np.concatenate([-np.sin(ang_r), np.sin(ang_r), -np.sin(ang_c), np.sin(ang_c)], axis=1)
    return (jnp.asarray(np.tile(cos_h, (1, 2)), F32), jnp.asarray(np.tile(sin_h, (1, 2)), F32))


def kernel(x_prompt, x_sample, cache_c_k, cache_c_v, cache_d_k, cache_d_v, c, c_ctx, norm_g, w_mod, b_mod,
           w_in, w_out, w_s, b_s, w_conv, sink, rpb, final_g):
    w_in_b = w_in.astype(BF16)
    w_out_b = w_out.astype(BF16)
    ws_cat = jnp.transpose(w_s, (0, 2, 1, 3)).reshape(DEPTH, CHUNK, A_HEADS * CHUNK).astype(BF16)
    bias_a = jnp.repeat(jnp.transpose(b_s, (0, 2, 1)), HEAD_DIM, axis=2)
    norm_g3 = norm_g.reshape(DEPTH, 1, D_MODEL)
    fg = final_g.reshape(1, D_MODEL)
    cos_t, sin_t = _rope_tables()
    cck = cache_c_k.reshape(DEC_BATCH, DEPTH, PAST_LEN, 128)
    ccv = cache_c_v.reshape(DEC_BATCH, DEPTH, PAST_LEN, 128)
    cdk = cache_d_k.reshape(DEC_BATCH, DEPTH, PAST_LEN, 256)
    cdv = cache_d_v.reshape(DEC_BATCH, DEPTH, PAST_LEN, 256)

    cond = jnp.concatenate([c_ctx[None, :], c, jnp.zeros((N_GROUPS - 1 - DEC_BATCH, D_MODEL), F32)], axis=0)
    mod = _modulation(cond, w_mod, b_mod)
    tab = _nbr_tables(rpb)

    y_prompt, s_ck, s_cv, s_dk, s_dv = _ctx_layers(x_prompt, mod, norm_g3, fg, w_in_b, w_out_b, ws_cat,
                                                   bias_a, w_conv, sink)
    xs = x_sample
    for l in range(DEPTH):
        yab, sz, qc, kc, vc, qd, kd, vd = _lat_proj(l, xs, mod, norm_g3, w_in_b, cos_t, sin_t,
                                                    ws_cat, bias_a, w_conv)
        xs = _lat_mix(l, xs, mod, fg, yab, sz, qc, kc, vc, qd, kd, vd, cck, ccv, cdk, cdv, tab, w_out_b, sink)

    shape_c = (BATCH, DEPTH, SEQ, 2, HEAD_DIM)
    shape_d = (BATCH, DEPTH, SEQ, 4, HEAD_DIM)
    return (y_prompt, xs, s_ck.reshape(shape_c), s_cv.reshape(shape_c), s_dk.reshape(shape_d),
            s_dv.reshape(shape_d))
```

```python
import functools

import numpy as np
import jax
import jax.numpy as jnp
from jax import lax
from jax.experimental import pallas as pl
from jax.experimental.pallas import tpu as pltpu

D_MODEL = 1024
BATCH = 16
SEQ = 256
DEPTH = 4
DEC_BATCH = 2
DEC_SEQ = 4096
PAST_LEN = 512
GRID_W = 64
GROUP_W = 256
HEAD_DIM = 64
A_HEADS = 4
CHUNK = 128
WINDOW = 128
QBLK = 128
WIN_R = 8
WIN_C = 16
ROPE_BASE = 10000.0
EPS = 1e-6
NEG = -1e30
D_IN = 3584
LOG2E = 1.4426950408889634
Q_SCALE = HEAD_DIM ** -0.5 * LOG2E

AU, AV, AZ, BB, BC, BH, BZ = 0, 256, 512, 768, 1024, 1280, 1536
CQ, CKV, CZ, DQ, DK, DV, DZ = 1792, 2048, 2304, 2560, 2816, 3072, 3328

LANES = 128
TILE = 512
ROWS_PER_TILE = TILE // GRID_W
N_TILES = DEC_SEQ // TILE
CTX_SEQS = 4
PROJ_TILES = 2
N_GROUPS = 8
VMEM_LIMIT = 56 * 1024 * 1024

F32 = jnp.float32
BF16 = jnp.bfloat16


def _silu(z):
    return z * (1.0 / (1.0 + jnp.exp(-z)))


def _dot(a, b):
    return jnp.dot(a, b, preferred_element_type=F32)


def _dot_nt(a, b):
    return lax.dot_general(a, b, (((1,), (1,)), ((), ())), preferred_element_type=F32)


def _rms(x):
    return x * lax.rsqrt(jnp.mean(x * x, axis=-1, keepdims=True) + EPS)


def _norm_mod(x, g, shift, scale):
    return _rms(x) * (g * (1.0 + scale)) + shift


def _swap64(x):
    return pltpu.roll(x, HEAD_DIM, axis=1)


def _group_mean_matrix():
    r = lax.broadcasted_iota(jnp.int32, (GROUP_W, GROUP_W), 0) // HEAD_DIM
    c = lax.broadcasted_iota(jnp.int32, (GROUP_W, GROUP_W), 1) // HEAD_DIM
    return jnp.where(r == c, 1.0 / HEAD_DIM, 0.0).astype(BF16)


def _branch_a(au, av, az, ws, bias):
    t = av.shape[0]
    sq = av * av
    hi = sq.astype(BF16)
    lo = (sq - hi.astype(F32)).astype(BF16)
    gm = _group_mean_matrix()
    ms = _dot(hi, gm) + _dot(lo, gm)
    vh = (av * lax.rsqrt(ms + EPS)).astype(BF16)
    head = lax.broadcasted_iota(jnp.int32, (CHUNK, GROUP_W), 1) // HEAD_DIM
    outs = []
    for n in range(t // CHUNK):
        v = vh[n * CHUNK:(n + 1) * CHUNK]
        rhs = jnp.concatenate([jnp.where(head == h, v, jnp.zeros_like(v)) for h in range(A_HEADS)], axis=0)
        outs.append(_dot(ws, rhs) + bias)
    mixed = jnp.concatenate(outs, axis=0)
    return au * mixed * _silu(az)


def _branch_b(bb, bc, bh, bz, wc, prev_row, next_row):
    t = bb.shape[0]
    xc = bc * bh
    row = lax.broadcasted_iota(jnp.int32, xc.shape, 0)
    xm = jnp.where(row == 0, prev_row, pltpu.roll(xc, 1, axis=0))
    xp = jnp.where(row == t - 1, next_row, pltpu.roll(xc, t - 1, axis=0))
    y = wc[0:1, :] * xm + wc[1:2, :] * xc + wc[2:3, :] * xp
    return bb * y * _silu(bz)


def _attend(score, value, pos, sink=None):
    m = jnp.max(score, axis=-1, keepdims=True)
    if sink is not None:
        m = jnp.maximum(m, sink * LOG2E)
    acc = _dot(jnp.exp2(score - m).astype(BF16), value)
    o = acc[:, pos * LANES:(pos + 1) * LANES]
    l = acc[:, (1 - pos) * LANES:(2 - pos) * LANES]
    if sink is not None:
        l = l + jnp.exp2(sink * LOG2E - m)
    return o * (1.0 / l)


def _context_pass(q_stacked, k, value):
    s = _dot_nt(q_stacked, k)
    m = jnp.max(s, axis=-1, keepdims=True)
    return m, _dot(jnp.exp2(s - m).astype(BF16), value)


def _local_softmax(s_loc, m_ctx, sink=None):
    m = jnp.maximum(jnp.max(s_loc, axis=-1, keepdims=True), m_ctx)
    if sink is not None:
        m = jnp.maximum(m, sink)
    e = jnp.exp2(s_loc - m).astype(BF16)
    return e, jnp.exp2(m_ctx - m), (None if sink is None else jnp.exp2(sink - m))


def _local_output(e, values, acc_ctx, ctx_scale, sink_term=None):
    rows = e.shape[0] // len(values)
    outs = []
    for p, (v, pos) in enumerate(values):
        r = slice(p * rows, (p + 1) * rows)
        acc = _dot(e[r], v) + acc_ctx[r] * ctx_scale[r]
        o = acc[:, pos * LANES:(pos + 1) * LANES]
        l = acc[:, (1 - pos) * LANES:(2 - pos) * LANES]
        if sink_term is not None:
            l = l + sink_term[r]
        outs.append(o * (1.0 / l))
    return outs


def _skewed(n_steps, stages):
    for tick in range(n_steps + len(stages) - 1):
        for lag, stage in enumerate(stages):
            if 0 <= tick - lag < n_steps:
                stage(tick - lag)


def _low_half(m):
    return lax.broadcasted_iota(jnp.int32, (m, LANES), 1) < HEAD_DIM


def _mask_half(q, low, half):
    z = jnp.zeros_like(q)
    return jnp.where(low, q, z) if half == 0 else jnp.where(low, z, q)


def _out_proj(x, gate, ya, yb, yc, yd, w_out_ref):
    y = _dot(ya.astype(BF16), w_out_ref[0:256, :])
    y += _dot(yb.astype(BF16), w_out_ref[256:512, :])
    y += _dot(yc.astype(BF16), w_out_ref[512:768, :])
    y += _dot(yd.astype(BF16), w_out_ref[768:1024, :])
    return x + gate * y


def _mod_kernel(cond_ref, w_ref, b_ref, o_ref):
    mod = _dot(_silu(cond_ref[...]).astype(BF16), w_ref[...].astype(BF16)) + b_ref[...]
    for j in range(3):
        o_ref[j] = mod[:, j * D_MODEL:(j + 1) * D_MODEL]


def _modulation(cond, w_mod, b_mod):
    return pl.pallas_call(
        _mod_kernel,
        grid=(DEPTH,),
        in_specs=[
            pl.BlockSpec((N_GROUPS, D_MODEL), lambda l: (0, 0)),
            pl.BlockSpec((None, D_MODEL, 3 * D_MODEL), lambda l: (l, 0, 0)),
            pl.BlockSpec((None, 1, 3 * D_MODEL), lambda l: (l, 0, 0)),
        ],
        out_specs=pl.BlockSpec((None, 3, N_GROUPS, D_MODEL), lambda l: (l, 0, 0, 0)),
        out_shape=jax.ShapeDtypeStruct((DEPTH, 3, N_GROUPS, D_MODEL), F32),
        compiler_params=pltpu.CompilerParams(dimension_semantics=("arbitrary",),
                                             vmem_limit_bytes=VMEM_LIMIT),
        name="modulation",
    )(cond, w_mod, b_mod.reshape(DEPTH, 1, 3 * D_MODEL))


N_BIAS_TILES = 2 * WIN_R - 2


def _nbr_table_kernel(rpb_ref, o_ref):
    c = lax.broadcasted_iota(jnp.int32, (GRID_W, LANES), 0)
    lane = lax.broadcasted_iota(jnp.int32, (GRID_W, LANES), 1)
    cc = lane % GRID_W
    cstart = jnp.clip(c - WIN_C // 2, 0, GRID_W - WIN_C)
    for h in range(4):
        for ro in range(N_BIAS_TILES):
            v = pltpu.roll(jnp.broadcast_to(rpb_ref[h, ro:ro + 1, :] * LOG2E, (GRID_W, LANES)),
                           LANES - (WIN_C - 1), 1, stride=1, stride_axis=0)
            o_ref[h, ro] = jnp.where(cc >= cstart, jnp.where(cc < cstart + WIN_C, v, NEG), NEG)


def _nbr_tables(rpb):
    half = LANES // 2 - (2 * WIN_C - 1)
    rows = jnp.pad(rpb, ((0, 0), (0, 0), (0, 1), (0, half)))
    rpb_p = jnp.concatenate([rows[:, :, :-1], rows[:, :, 1:]], axis=-1)
    rpb_p = jnp.pad(rpb_p, ((0, 0), (0, 0), (0, 1), (0, 0)))
    return pl.pallas_call(
        _nbr_table_kernel,
        grid=(DEPTH,),
        in_specs=[pl.BlockSpec((None, 4, 2 * WIN_R, LANES), lambda l: (l, 0, 0, 0))],
        out_specs=pl.BlockSpec((None, 4, N_BIAS_TILES, GRID_W, LANES), lambda l: (l, 0, 0, 0, 0)),
        out_shape=jax.ShapeDtypeStruct((DEPTH, 4, N_BIAS_TILES, GRID_W, LANES), F32),
        compiler_params=pltpu.CompilerParams(dimension_semantics=("arbitrary",)),
        name="nbr_tables",
    )(rpb_p)


def _ctx_kernel(sink_ref, x_ref, mod_ref, g_ref, fg_ref, w_in_ref, w_out_ref, ws_ref, ba_ref, wc_ref,
                y_ref, ck_ref, cv_ref, dk_ref, dv_ref, xs):
    l = pl.program_id(0)
    pair = pl.program_id(1)

    @pl.when(l == 0)
    def _():
        for u in range(CTX_SEQS):
            xs[CTX_SEQS * pair + u] = x_ref[u]

    low = _low_half(SEQ)
    ones = jnp.ones((SEQ, LANES), BF16)

    def project(u):
        x = xs[CTX_SEQS * pair + u]
        h = _norm_mod(x, g_ref[...], mod_ref[0, 0:1, :], mod_ref[1, 0:1, :]).astype(BF16)
        return x, _dot(h, w_in_ref[...])

    def mix(u, p):
        kcf = p[:, CKV:CKV + 128]
        vcf = p[:, CKV + 128:CKV + 256]
        kdf = p[:, DK:DK + 256]
        vdf = p[:, DV:DV + 256]
        ck_ref[u] = kcf
        cv_ref[u] = vcf
        dk_ref[u] = kdf
        dv_ref[u] = vdf

        ya = _branch_a(p[:, AU:AU + 256], p[:, AV:AV + 256], p[:, AZ:AZ + 256], ws_ref[...], ba_ref[...])
        zero_row = jnp.zeros((1, GROUP_W), F32)
        yb = _branch_b(p[:, BB:BB + 256], p[:, BC:BC + 256], p[:, BH:BH + 256], p[:, BZ:BZ + 256],
                       wc_ref[...], zero_row, zero_row)

        kc = (kcf.astype(BF16), _swap64(kcf).astype(BF16))
        vc = (jnp.concatenate([vcf.astype(BF16), ones], axis=1),
              jnp.concatenate([ones, _swap64(vcf).astype(BF16)], axis=1))
        o = []
        for t in range(2):
            q = (p[:, CQ + t * LANES:CQ + (t + 1) * LANES] * Q_SCALE).astype(BF16)
            for half in range(2):
                sw = (t + half) % 2
                o.append(_attend(_dot_nt(_mask_half(q, low, half), kc[sw]), vc[sw], sw,
                                 sink_ref[l, 2 * t + half]))
        yc = jnp.concatenate([jnp.where(low, o[0], o[1]), jnp.where(low, o[2], o[3])], axis=1)
        yc = yc * _silu(p[:, CZ:CZ + 256])

        o = []
        for t in range(2):
            q = (p[:, DQ + t * LANES:DQ + (t + 1) * LANES] * Q_SCALE).astype(BF16)
            k = kdf[:, t * LANES:(t + 1) * LANES].astype(BF16)
            v = jnp.concatenate([vdf[:, t * LANES:(t + 1) * LANES].astype(BF16), ones], axis=1)
            for half in range(2):
                o.append(_attend(_dot_nt(_mask_half(q, low, half), k), v, 0))
        yd = jnp.concatenate([jnp.where(low, o[0], o[1]), jnp.where(low, o[2], o[3])], axis=1)
        yd = yd * _silu(p[:, DZ:DZ + 256])
        return ya, yb, yc, yd

    def finish(u, x, ys):
        x_new = _out_proj(x, mod_ref[2, 0:1, :], *ys, w_out_ref)
        xs[CTX_SEQS * pair + u] = x_new
        return x_new

    staged = [project(0)]
    x_new = []
    for u in range(CTX_SEQS):
        if u + 1 < CTX_SEQS:
            staged.append(project(u + 1))
        x, p = staged[u]
        x_new.append(finish(u, x, mix(u, p)))

    @pl.when(l == DEPTH - 1)
    def _():
        for u in range(CTX_SEQS):
            y_ref[u] = _rms(x_new[u]) * fg_ref[...]


def _ctx_layers(x, mod, norm_g, final_g, w_in, w_out, ws_cat, bias_a, w_conv, sink):
    per_layer = lambda *shape: pl.BlockSpec((None,) + shape, lambda l, b: (l,) + (0,) * len(shape))
    state = lambda w: pl.BlockSpec((CTX_SEQS, None, SEQ, w), lambda l, b: (b, l, 0, 0))
    n_steps = BATCH // CTX_SEQS
    return pl.pallas_call(
        _ctx_kernel,
        grid=(DEPTH, n_steps),
        in_specs=[
            pl.BlockSpec(memory_space=pltpu.SMEM),
            pl.BlockSpec((CTX_SEQS, SEQ, D_MODEL), lambda l, b: (jnp.where(l == 0, b, n_steps - 1), 0, 0)),
            per_layer(3, N_GROUPS, D_MODEL),
            per_layer(1, D_MODEL),
            pl.BlockSpec((1, D_MODEL), lambda l, b: (0, 0)),
            pl.BlockSpec((None, D_MODEL, D_IN), lambda l, b: (l, 0, 0), pipeline_mode=pl.Buffered(1)),
            pl.BlockSpec((None, D_MODEL, D_MODEL), lambda l, b: (l, 0, 0), pipeline_mode=pl.Buffered(1)),
            per_layer(CHUNK, A_HEADS * CHUNK),
            per_layer(CHUNK, GROUP_W),
            per_layer(3, GROUP_W),
        ],
        out_specs=[
            pl.BlockSpec((CTX_SEQS, SEQ, D_MODEL), lambda l, b: (jnp.where(l == DEPTH - 1, b, 0), 0, 0)),
            state(128), state(128), state(256), state(256),
        ],
        out_shape=[
            jax.ShapeDtypeStruct((BATCH, SEQ, D_MODEL), F32),
            jax.ShapeDtypeStruct((BATCH, DEPTH, SEQ, 128), F32),
            jax.ShapeDtypeStruct((BATCH, DEPTH, SEQ, 128), F32),
            jax.ShapeDtypeStruct((BATCH, DEPTH, SEQ, 256), F32),
            jax.ShapeDtypeStruct((BATCH, DEPTH, SEQ, 256), F32),
        ],
        scratch_shapes=[pltpu.VMEM((BATCH, SEQ, D_MODEL), F32)],
        compiler_params=pltpu.CompilerParams(dimension_semantics=("arbitrary", "arbitrary"),
                                             vmem_limit_bytes=VMEM_LIMIT),
        name="ctx_layers",
    )(sink, x, mod, norm_g, final_g, w_in, w_out, ws_cat, bias_a, w_conv)


def _rope(x, cos, sin_signed, first_half):
    swapped = jnp.where(first_half, pltpu.roll(x, LANES - 16, axis=1), pltpu.roll(x, 16, axis=1))
    return x * cos + swapped * sin_signed


def _lat_proj_kernel(x_ref, xp_ref, xn_ref, mod_ref, g_ref, w_ref, cos_ref, sin_ref, ws_ref, ba_ref, wc_ref,
                     yab_ref, sz_ref, qc_ref, kc_ref, vc_ref, qd_ref, kd_ref, vd_ref):
    i = pl.program_id(1)
    grp = pl.ds(1 + pl.program_id(0), 1)
    g = g_ref[...]
    shift = mod_ref[0, grp, :]
    scale = mod_ref[1, grp, :]
    first_half = (lax.broadcasted_iota(jnp.int32, (TILE, LANES), 1) % 32) < 16

    def normalised(x):
        return _norm_mod(x, g, shift, scale).astype(BF16)

    def project(u, h, after_first_dot):
        rows = slice(u * TILE, (u + 1) * TILE)
        tile_index = PROJ_TILES * i + u
        pa = _dot(h, w_ref[:, AU:AU + 3 * GROUP_W])
        after_first_dot()
        ya = _branch_a(pa[:, 0:256], pa[:, 256:512], pa[:, 512:768], ws_ref[...], ba_ref[...])
        yab_ref[rows, 0:GROUP_W] = ya.astype(BF16)

        pb = _dot(h, w_ref[:, BB:BB + 4 * GROUP_W])
        before = xp_ref[...] if u == 0 else x_ref[u * TILE - 8:u * TILE, :]
        after = xn_ref[...] if u == PROJ_TILES - 1 else x_ref[(u + 1) * TILE:(u + 1) * TILE + 8, :]
        e = _dot(normalised(jnp.concatenate([before, after], axis=0)), w_ref[:, BC:BC + 2 * GROUP_W])
        prev_row = jnp.where(tile_index > 0, e[7:8, 0:256] * e[7:8, 256:512], 0.0)
        next_row = jnp.where(tile_index < N_TILES - 1, e[8:9, 0:256] * e[8:9, 256:512], 0.0)
        yb = _branch_b(pb[:, 0:256], pb[:, 256:512], pb[:, 512:768], pb[:, 768:1024], wc_ref[...],
                       prev_row, next_row)
        yab_ref[rows, GROUP_W:2 * GROUP_W] = yb.astype(BF16)

        sz_ref[rows, 0:256] = _silu(_dot(h, w_ref[:, CZ:CZ + 256]))
        sz_ref[rows, 256:512] = _silu(_dot(h, w_ref[:, DZ:DZ + 256]))
        cos = cos_ref[rows, :]
        sin = sin_ref[rows, :]
        qc = _dot(h, w_ref[:, CQ:CQ + 256])
        for t in range(2):
            qt = _rope(qc[:, t * LANES:(t + 1) * LANES], cos, sin, first_half) * Q_SCALE
            qc_ref[rows, t * LANES:(t + 1) * LANES] = qt.astype(BF16)
        kv = _dot(h, w_ref[:, CKV:CKV + 256])
        k = _rope(kv[:, 0:LANES], cos, sin, first_half)
        v = kv[:, LANES:2 * LANES]
        kc_ref[rows, 0:LANES] = k.astype(BF16)
        kc_ref[rows, LANES:2 * LANES] = _swap64(k).astype(BF16)
        vc_ref[rows, 0:LANES] = v.astype(BF16)
        vc_ref[rows, LANES:2 * LANES] = _swap64(v).astype(BF16)
        qd_ref[rows, :] = (_dot(h, w_ref[:, DQ:DQ + 256]) * Q_SCALE).astype(BF16)
        kd_ref[rows, :] = _dot(h, w_ref[:, DK:DK + 256]).astype(BF16)
        vd_ref[rows, :] = _dot(h, w_ref[:, DV:DV + 256]).astype(BF16)

    hs = [normalised(x_ref[0:TILE, :])]

    def normalise_next(u):
        if u + 1 < PROJ_TILES:
            hs.append(normalised(x_ref[(u + 1) * TILE:(u + 2) * TILE, :]))

    for u in range(PROJ_TILES):
        project(u, hs[u], functools.partial(normalise_next, u))


def _lat_proj(layer, x, mod, norm_g, w_in, cos_t, sin_t, ws_cat, bias_a, w_conv):
    step_rows = PROJ_TILES * TILE
    tile = lambda w: pl.BlockSpec((None, step_rows, w), lambda b, i: (b, i, 0))
    per_layer = lambda *shape: pl.BlockSpec((None,) + shape, lambda b, i: (layer,) + (0,) * len(shape))
    sds = lambda w, dt: jax.ShapeDtypeStruct((DEC_BATCH, DEC_SEQ, w), dt)
    edge = step_rows // 8
    last = DEC_SEQ // 8 - 1
    return pl.pallas_call(
        _lat_proj_kernel,
        grid=(DEC_BATCH, N_TILES // PROJ_TILES),
        in_specs=[
            tile(D_MODEL),
            pl.BlockSpec((None, 8, D_MODEL), lambda b, i: (b, jnp.maximum(i * edge - 1, 0), 0)),
            pl.BlockSpec((None, 8, D_MODEL), lambda b, i: (b, jnp.minimum((i + 1) * edge, last), 0)),
            per_layer(3, N_GROUPS, D_MODEL),
            per_layer(1, D_MODEL),
            per_layer(D_MODEL, D_IN),
            pl.BlockSpec((step_rows, LANES), lambda b, i: (i, 0)),
            pl.BlockSpec((step_rows, LANES), lambda b, i: (i, 0)),
            per_layer(CHUNK, A_HEADS * CHUNK),
            per_layer(CHUNK, GROUP_W),
            per_layer(3, GROUP_W),
        ],
        out_specs=[tile(512), tile(512)] + [tile(256)] * 6,
        out_shape=[sds(512, BF16), sds(512, F32)] + [sds(256, BF16)] * 6,
        compiler_params=pltpu.CompilerParams(dimension_semantics=("arbitrary", "arbitrary"),
                                             vmem_limit_bytes=VMEM_LIMIT),
        name="lat_proj",
    )(x, x, x, mod, norm_g, w_in, cos_t, sin_t, ws_cat, bias_a, w_conv)


_C_GROUPS = (((0, 0), (1, 1)), ((0, 1), (1, 0)))
_D_GROUPS = (((0, 0), (0, 1)), ((1, 0), (1, 1)))


def _stacked_queries(q_ref, rows, low, groups):
    return [jnp.concatenate([_mask_half(q_ref[rows, t * LANES:(t + 1) * LANES], low, half)
                             for t, half in group], axis=0) for group in groups]


def _lat_mix_kernel(layer, sink_ref, x_ref, mod_ref, fg_ref, yab_ref, sz_ref,
                    qc_ref, kc_ref, kcp_ref, kcn_ref, vc_ref, vcp_ref, vcn_ref,
                    qd_ref, kd_ref, kdp_ref, kdn_ref, vd_ref, vdp_ref, vdn_ref,
                    cck_ref, ccv_ref, cdk_ref, cdv_ref, cmask_ref, tab_ref, w_out_ref,
                    xo_ref,
                    kce, vce, kde, vde, yc_scr, yd_scr, m_ctx, acc_ctx, *stage_bufs):
    b = pl.program_id(0)
    i = pl.program_id(1)
    s_c, s_d = stage_bufs[0:2], stage_bufs[2:4]

    def file_context(m, acc, group, step_rows, n_steps):
        for which in range(2):
            for k in range(n_steps):
                src = slice(which * TILE + k * step_rows, which * TILE + (k + 1) * step_rows)
                dst = slice((4 * k + 2 * group + which) * step_rows, (4 * k + 2 * group + which + 1) * step_rows)
                m_ctx[dst, :] = m[src]
                acc_ctx[dst, :] = acc[src]

    kce[0:QBLK, :] = kcp_ref[...]
    kce[QBLK:QBLK + TILE, :] = kc_ref[...]
    kce[QBLK + TILE:, :] = kcn_ref[...]
    for c_src, c_dst in ((0, 0), (LANES, 2 * LANES)):
        vce[0:QBLK, c_dst:c_dst + LANES] = vcp_ref[:, c_src:c_src + LANES]
        vce[QBLK:QBLK + TILE, c_dst:c_dst + LANES] = vc_ref[:, c_src:c_src + LANES]
        vce[QBLK + TILE:, c_dst:c_dst + LANES] = vcn_ref[:, c_src:c_src + LANES]
    vce[:, LANES:2 * LANES] = jnp.ones((TILE + 2 * QBLK, LANES), BF16)
    cck = cck_ref[...]
    ccv = ccv_ref[...]
    ones_ctx = jnp.ones((PAST_LEN, LANES), BF16)
    k_ctx = (cck.astype(BF16), _swap64(cck).astype(BF16))
    v_ctx = jnp.concatenate([ccv.astype(BF16), ones_ctx, _swap64(ccv).astype(BF16)], axis=1)
    low_q = _low_half(QBLK)
    low_t = _low_half(TILE)
    n_blocks = DEC_SEQ // QBLK
    blocks_per_tile = TILE // QBLK

    for sw, q in enumerate(_stacked_queries(qc_ref, slice(None), low_t, _C_GROUPS)):
        m, acc = _context_pass(q, k_ctx[sw], v_ctx[:, sw * LANES:(sw + 2) * LANES])
        file_context(m, acc, sw, QBLK, blocks_per_tile)

    def c_scores(j):
        n = i * blocks_per_tile + j
        window = cmask_ref[jnp.where(n == 0, 1, jnp.where(n == n_blocks - 1, 2, 0))]
        window = jnp.concatenate([window, window], axis=0)
        rows = slice(j * QBLK, (j + 1) * QBLK)
        for sw, q in enumerate(_stacked_queries(qc_ref, rows, low_q, _C_GROUPS)):
            k_loc = kce[j * QBLK:(j + 3) * QBLK, sw * LANES:(sw + 1) * LANES]
            s_c[j % 2][2 * sw * QBLK:2 * (sw + 1) * QBLK, :] = _dot_nt(q, k_loc) + window

    def c_softmax(j):
        step = slice(4 * j * QBLK, 4 * (j + 1) * QBLK)
        sink = jnp.concatenate([jnp.full((QBLK, 1), sink_ref[layer, 2 * t + half] * LOG2E, F32)
                                for group in _C_GROUPS for t, half in group], axis=0)
        e, ctx_scale, sink_term = _local_softmax(s_c[j % 2][...], m_ctx[step, :], sink)
        values = [(vce[j * QBLK:(j + 3) * QBLK, sw * LANES:(sw + 2) * LANES], sw) for sw in range(2)]
        r0, r1 = _local_output(e, values, acc_ctx[step, :], ctx_scale, sink_term)
        yc_scr[j * QBLK:(j + 1) * QBLK, :] = jnp.concatenate(
            [jnp.where(low_q, r0[0:QBLK], r1[0:QBLK]), jnp.where(low_q, r1[QBLK:], r0[QBLK:])], axis=1)

    _skewed(blocks_per_tile, (c_scores, c_softmax))

    halo = 4 * GRID_W
    kde[0:halo, :] = kdp_ref[...]
    kde[halo:halo + TILE, :] = kd_ref[...]
    kde[halo + TILE:, :] = kdn_ref[...]
    for c_src, c_dst in ((0, 0), (LANES, 2 * LANES)):
        vde[0:halo, c_dst:c_dst + LANES] = vdp_ref[:, c_src:c_src + LANES]
        vde[halo:halo + TILE, c_dst:c_dst + LANES] = vd_ref[:, c_src:c_src + LANES]
        vde[halo + TILE:, c_dst:c_dst + LANES] = vdn_ref[:, c_src:c_src + LANES]
    vde[:, LANES:2 * LANES] = jnp.ones((TILE + 2 * halo, LANES), BF16)
    kd_ctx = cdk_ref[...].astype(BF16)
    cdv = cdv_ref[...].astype(BF16)
    vd_ctx = jnp.concatenate([cdv[:, 0:LANES], ones_ctx, cdv[:, LANES:2 * LANES]], axis=1)
    low_r = _low_half(GRID_W)
    n_rows = DEC_SEQ // GRID_W

    for t, q in enumerate(_stacked_queries(qd_ref, slice(None), low_t, _D_GROUPS)):
        m, acc = _context_pass(q, kd_ctx[:, t * LANES:(t + 1) * LANES], vd_ctx[:, t * LANES:(t + 2) * LANES])
        file_context(m, acc, t, GRID_W, ROWS_PER_TILE)

    def window_start(rl):
        r = i * ROWS_PER_TILE + rl
        rs = jnp.clip(r - WIN_R // 2, 0, n_rows - WIN_R)
        return rs - r + (WIN_R - 1), pl.multiple_of((rs - i * ROWS_PER_TILE + 4) * GRID_W, GRID_W)

    def d_scores(rl):
        didx, off = window_start(rl)
        rows = slice(rl * GRID_W, (rl + 1) * GRID_W)
        for t, q in enumerate(_stacked_queries(qd_ref, rows, low_r, _D_GROUPS)):
            k_loc = kde[pl.ds(off, WIN_R * GRID_W), t * LANES:(t + 1) * LANES]
            bias = jnp.concatenate(
                [jnp.concatenate([tab_ref[h, didx + 2 * p] for p in range(WIN_R // 2)], axis=1)
                 for h in (2 * t, 2 * t + 1)], axis=0)
            s_d[rl % 2][2 * t * GRID_W:2 * (t + 1) * GRID_W, :] = _dot_nt(q, k_loc) + bias

    def d_softmax(rl):
        _, off = window_start(rl)
        step = slice(4 * rl * GRID_W, 4 * (rl + 1) * GRID_W)
        e, ctx_scale, _ = _local_softmax(s_d[rl % 2][...], m_ctx[step, :])
        values = [(vde[pl.ds(off, WIN_R * GRID_W), t * LANES:(t + 2) * LANES], t) for t in range(2)]
        r = _local_output(e, values, acc_ctx[step, :], ctx_scale)
        yd_scr[rl * GRID_W:(rl + 1) * GRID_W, :] = jnp.concatenate(
            [jnp.where(low_r, r[t][0:GRID_W], r[t][GRID_W:]) for t in range(2)], axis=1)

    _skewed(ROWS_PER_TILE, (d_scores, d_softmax))

    yc = (yc_scr[...] * sz_ref[:, 0:256]).astype(BF16)
    yd = (yd_scr[...] * sz_ref[:, 256:512]).astype(BF16)
    y = (_dot(yab_ref[...], w_out_ref[0:2 * GROUP_W, :]) + _dot(yc, w_out_ref[2 * GROUP_W:3 * GROUP_W, :])
         + _dot(yd, w_out_ref[3 * GROUP_W:4 * GROUP_W, :]))
    x_new = x_ref[...] + mod_ref[2, pl.ds(1 + b, 1), :] * y
    if layer == DEPTH - 1:
        x_new = _rms(x_new) * fg_ref[...]
    xo_ref[...] = x_new


def _lat_mix(layer, x, mod, final_g, yab, sz, qc, kc, vc, qd, kd, vd, cck, ccv, cdk, cdv, tab, w_out, sink):
    tile = lambda w: pl.BlockSpec((None, TILE, w), lambda b, i: (b, i, 0))
    per_layer = lambda *shape: pl.BlockSpec((None,) + shape, lambda b, i: (layer,) + (0,) * len(shape))

    def halo(rows, w):
        per = TILE // rows
        last = DEC_SEQ // rows - 1
        prev = pl.BlockSpec((None, rows, w), lambda b, i: (b, jnp.maximum(i * per - 1, 0), 0))
        nxt = pl.BlockSpec((None, rows, w), lambda b, i: (b, jnp.minimum((i + 1) * per, last), 0))
        return prev, nxt

    cprev, cnext = halo(QBLK, 256)
    dprev, dnext = halo(4 * GRID_W, 256)
    cache = lambda w: pl.BlockSpec((None, None, PAST_LEN, w), lambda b, i: (b, layer, 0, 0))
    return pl.pallas_call(
        functools.partial(_lat_mix_kernel, layer),
        grid=(DEC_BATCH, N_TILES),
        in_specs=[
            pl.BlockSpec(memory_space=pltpu.SMEM),
            tile(D_MODEL),
            per_layer(3, N_GROUPS, D_MODEL),
            pl.BlockSpec((1, D_MODEL), lambda b, i: (0, 0)),
            tile(512), tile(512),
            tile(256), tile(256), cprev, cnext, tile(256), cprev, cnext,
            tile(256), tile(256), dprev, dnext, tile(256), dprev, dnext,
            cache(128), cache(128), cache(256), cache(256),
            pl.BlockSpec((3, QBLK, 3 * QBLK), lambda b, i: (0, 0, 0)),
            per_layer(4, N_BIAS_TILES, GRID_W, LANES),
            per_layer(D_MODEL, D_MODEL),
        ],
        out_specs=tile(D_MODEL),
        out_shape=jax.ShapeDtypeStruct((DEC_BATCH, DEC_SEQ, D_MODEL), F32),
        scratch_shapes=[
            pltpu.VMEM((TILE + 2 * QBLK, 2 * LANES), BF16),
            pltpu.VMEM((TILE + 2 * QBLK, 3 * LANES), BF16),
            pltpu.VMEM((TILE + 8 * GRID_W, 2 * LANES), BF16),
            pltpu.VMEM((TILE + 8 * GRID_W, 3 * LANES), BF16),
            pltpu.VMEM((TILE, 256), F32),
            pltpu.VMEM((TILE, 256), F32),
            pltpu.VMEM((4 * TILE, 1), F32),
            pltpu.VMEM((4 * TILE, 2 * LANES), F32),
        ] + 2 * [pltpu.VMEM((4 * QBLK, 3 * QBLK), F32)] + 2 * [pltpu.VMEM((4 * GRID_W, WIN_R * GRID_W), F32)],
        compiler_params=pltpu.CompilerParams(dimension_semantics=("arbitrary", "arbitrary"),
                                             vmem_limit_bytes=VMEM_LIMIT),
        name="lat_mix",
    )(sink, x, mod, final_g, yab, sz, qc, kc, kc, kc, vc, vc, vc,
      qd, kd, kd, kd, vd, vd, vd, cck, ccv, cdk, cdv, _window_masks(), tab, w_out)


def _window_masks():
    p = np.arange(QBLK)[:, None]
    j = np.arange(3 * QBLK)[None, :]
    band = np.abs(j - QBLK - p) <= WINDOW
    masks = [band, band & (j >= QBLK), band & (j < 2 * QBLK)]
    return jnp.asarray(np.where(np.stack(masks), 0.0, NEG), F32)


def _rope_tables():
    t = np.arange(DEC_SEQ)
    freqs = (np.float32(ROPE_BASE) ** (-np.arange(16, dtype=np.float32) / np.float32(16))).astype(np.float32)
    ang_r = (t // GRID_W).astype(np.float32)[:, None] * freqs
    ang_c = (t % GRID_W).astype(np.float32)[:, None] * freqs
    cos_h = np.concatenate([np.cos(ang_r), np.cos(ang_r), np.cos(ang_c), np.cos(ang_c)], axis=1)
    sin_h = np.concatenate([-np.sin(ang_r), np.sin(ang_r), -np.sin(ang_c), np.sin(ang_c)], axis=1)
    return (jnp.asarray(np.tile(cos_h, (1, 2)), F32), jnp.asarray(np.tile(sin_h, (1, 2)), F32))


def kernel(x_prompt, x_sample, cache_c_k, cache_c_v, cache_d_k, cache_d_v, c, c_ctx, norm_g, w_mod, b_mod,
           w_in, w_out, w_s, b_s, w_conv, sink, rpb, final_g):
    w_in_b = w_in.astype(BF16)
    w_out_b = w_out.astype(BF16)
    ws_cat = jnp.transpose(w_s, (0, 2, 1, 3)).reshape(DEPTH, CHUNK, A_HEADS * CHUNK).astype(BF16)
    bias_a = jnp.repeat(jnp.transpose(b_s, (0, 2, 1)), HEAD_DIM, axis=2)
    norm_g3 = norm_g.reshape(DEPTH, 1, D_MODEL)
    fg = final_g.reshape(1, D_MODEL)
    cos_t, sin_t = _rope_tables()
    cck = cache_c_k.reshape(DEC_BATCH, DEPTH, PAST_LEN, 128)
    ccv = cache_c_v.reshape(DEC_BATCH, DEPTH, PAST_LEN, 128)
    cdk = cache_d_k.reshape(DEC_BATCH, DEPTH, PAST_LEN, 256)
    cdv = cache_d_v.reshape(DEC_BATCH, DEPTH, PAST_LEN, 256)

    cond = jnp.concatenate([c_ctx[None, :], c, jnp.zeros((N_GROUPS - 1 - DEC_BATCH, D_MODEL), F32)], axis=0)
    mod = _modulation(cond, w_mod, b_mod)
    tab = _nbr_tables(rpb)

    y_prompt, s_ck, s_cv, s_dk, s_dv = _ctx_layers(x_prompt, mod, norm_g3, fg, w_in_b, w_out_b, ws_cat,
                                                   bias_a, w_conv, sink)
    xs = x_sample
    for l in range(DEPTH):
        yab, sz, qc, kc, vc, qd, kd, vd = _lat_proj(l, xs, mod, norm_g3, w_in_b, cos_t, sin_t,
                                                    ws_cat, bias_a, w_conv)
        xs = _lat_mix(l, xs, mod, fg, yab, sz, qc, kc, vc, qd, kd, vd, cck, ccv, cdk, cdv, tab, w_out_b, sink)

    shape_c = (BATCH, DEPTH, SEQ, 2, HEAD_DIM)
    shape_d = (BATCH, DEPTH, SEQ, 4, HEAD_DIM)
    return (y_prompt, xs, s_ck.reshape(shape_c), s_cv.reshape(shape_c), s_dk.reshape(shape_d),
            s_dv.reshape(shape_d))
```

```python
import functools

import numpy as np
import jax
import jax.numpy as jnp
from jax import lax
from jax.experimental import pallas as pl
from jax.experimental.pallas import tpu as pltpu

D_MODEL = 1024
BATCH = 16
SEQ = 256
DEPTH = 4
DEC_BATCH = 2
DEC_SEQ = 4096
PAST_LEN = 512
GRID_W = 64
GROUP_W = 256
HEAD_DIM = 64
A_HEADS = 4
CHUNK = 128
WINDOW = 128
QBLK = 128
WIN_R = 8
WIN_C = 16
ROPE_BASE = 10000.0
EPS = 1e-6
NEG = -1e30
D_IN = 3584
LOG2E = 1.4426950408889634
Q_SCALE = HEAD_DIM ** -0.5 * LOG2E

AU, AV, AZ, BB, BC, BH, BZ = 0, 256, 512, 768, 1024, 1280, 1536
CQ, CKV, CZ, DQ, DK, DV, DZ = 1792, 2048, 2304, 2560, 2816, 3072, 3328

LANES = 128
TILE = 512
ROWS_PER_TILE = TILE // GRID_W
N_TILES = DEC_SEQ // TILE
CTX_SEQS = 4
PROJ_TILES = 2
N_GROUPS = 8
VMEM_LIMIT = 56 * 1024 * 1024

F32 = jnp.float32
BF16 = jnp.bfloat16


def _silu(z):
    return z * (1.0 / (1.0 + jnp.exp(-z)))


def _dot(a, b):
    return jnp.dot(a, b, preferred_element_type=F32)


def _dot_nt(a, b):
    return lax.dot_general(a, b, (((1,), (1,)), ((), ())), preferred_element_type=F32)


def _rms(x):
    return x * lax.rsqrt(jnp.mean(x * x, axis=-1, keepdims=True) + EPS)


def _norm_mod(x, g, shift, scale):
    return _rms(x) * (g * (1.0 + scale)) + shift


def _swap64(x):
    return pltpu.roll(x, HEAD_DIM, axis=1)


def _group_mean_matrix():
    r = lax.broadcasted_iota(jnp.int32, (GROUP_W, GROUP_W), 0) // HEAD_DIM
    c = lax.broadcasted_iota(jnp.int32, (GROUP_W, GROUP_W), 1) // HEAD_DIM
    return jnp.where(r == c, 1.0 / HEAD_DIM, 0.0).astype(BF16)


def _branch_a(au, av, az, ws, bias):
    t = av.shape[0]
    sq = av * av
    hi = sq.astype(BF16)
    lo = (sq - hi.astype(F32)).astype(BF16)
    gm = _group_mean_matrix()
    ms = _dot(hi, gm) + _dot(lo, gm)
    vh = (av * lax.rsqrt(ms + EPS)).astype(BF16)
    head = lax.broadcasted_iota(jnp.int32, (CHUNK, GROUP_W), 1) // HEAD_DIM
    outs = []
    for n in range(t // CHUNK):
        v = vh[n * CHUNK:(n + 1) * CHUNK]
        rhs = jnp.concatenate([jnp.where(head == h, v, jnp.zeros_like(v)) for h in range(A_HEADS)], axis=0)
        outs.append(_dot(ws, rhs) + bias)
    mixed = jnp.concatenate(outs, axis=0)
    return au * mixed * _silu(az)


def _branch_b(bb, bc, bh, bz, wc, prev_row, next_row):
    t = bb.shape[0]
    xc = bc * bh
    row = lax.broadcasted_iota(jnp.int32, xc.shape, 0)
    xm = jnp.where(row == 0, prev_row, pltpu.roll(xc, 1, axis=0))
    xp = jnp.where(row == t - 1, next_row, pltpu.roll(xc, t - 1, axis=0))
    y = wc[0:1, :] * xm + wc[1:2, :] * xc + wc[2:3, :] * xp
    return bb * y * _silu(bz)


def _attend(score, value, pos, sink=None):
    m = jnp.max(score, axis=-1, keepdims=True)
    if sink is not None:
        m = jnp.maximum(m, sink * LOG2E)
    acc = _dot(jnp.exp2(score - m).astype(BF16), value)
    o = acc[:, pos * LANES:(pos + 1) * LANES]
    l = acc[:, (1 - pos) * LANES:(2 - pos) * LANES]
    if sink is not None:
        l = l + jnp.exp2(sink * LOG2E - m)
    return o * (1.0 / l)


def _context_pass(q_stacked, k_t, value_t):
    s = _dot(q_stacked, k_t)
    m = jnp.max(s, axis=-1, keepdims=True)
    return m, _dot_nt(jnp.exp2(s - m).astype(BF16), value_t)


def _local_softmax(s_loc, m_ctx, sink=None):
    m = jnp.maximum(jnp.max(s_loc, axis=-1, keepdims=True), m_ctx)
    if sink is not None:
        m = jnp.maximum(m, sink)
    e = jnp.exp2(s_loc - m).astype(BF16)
    return e, jnp.exp2(m_ctx - m), (None if sink is None else jnp.exp2(sink - m))


def _local_output(e, values, acc_ctx, ctx_scale, sink_term=None):
    rows = e.shape[0] // len(values)
    outs = []
    for p, (v, pos) in enumerate(values):
        r = slice(p * rows, (p + 1) * rows)
        acc = _dot(e[r], v) + acc_ctx[r] * ctx_scale[r]
        o = acc[:, pos * LANES:(pos + 1) * LANES]
        l = acc[:, (1 - pos) * LANES:(2 - pos) * LANES]
        if sink_term is not None:
            l = l + sink_term[r]
        outs.append(o * (1.0 / l))
    return outs


def _skewed(n_steps, stages):
    for tick in range(n_steps + len(stages) - 1):
        for lag, stage in enumerate(stages):
            if 0 <= tick - lag < n_steps:
                stage(tick - lag)


def _low_half(m):
    return lax.broadcasted_iota(jnp.int32, (m, LANES), 1) < HEAD_DIM


def _mask_half(q, low, half):
    z = jnp.zeros_like(q)
    return jnp.where(low, q, z) if half == 0 else jnp.where(low, z, q)


def _out_proj(x, gate, ya, yb, yc, yd, w_out_ref):
    y = _dot(ya.astype(BF16), w_out_ref[0:256, :])
    y += _dot(yb.astype(BF16), w_out_ref[256:512, :])
    y += _dot(yc.astype(BF16), w_out_ref[512:768, :])
    y += _dot(yd.astype(BF16), w_out_ref[768:1024, :])
    return x + gate * y


def _mod_kernel(cond_ref, w_ref, b_ref, o_ref):
    mod = _dot(_silu(cond_ref[...]).astype(BF16), w_ref[...].astype(BF16)) + b_ref[...]
    for j in range(3):
        o_ref[j] = mod[:, j * D_MODEL:(j + 1) * D_MODEL]


def _modulation(cond, w_mod, b_mod):
    return pl.pallas_call(
        _mod_kernel,
        grid=(DEPTH,),
        in_specs=[
            pl.BlockSpec((N_GROUPS, D_MODEL), lambda l: (0, 0)),
            pl.BlockSpec((None, D_MODEL, 3 * D_MODEL), lambda l: (l, 0, 0)),
            pl.BlockSpec((None, 1, 3 * D_MODEL), lambda l: (l, 0, 0)),
        ],
        out_specs=pl.BlockSpec((None, 3, N_GROUPS, D_MODEL), lambda l: (l, 0, 0, 0)),
        out_shape=jax.ShapeDtypeStruct((DEPTH, 3, N_GROUPS, D_MODEL), F32),
        compiler_params=pltpu.CompilerParams(dimension_semantics=("arbitrary",),
                                             vmem_limit_bytes=VMEM_LIMIT),
        name="modulation",
    )(cond, w_mod, b_mod.reshape(DEPTH, 1, 3 * D_MODEL))


N_BIAS_TILES = 2 * WIN_R - 2


def _nbr_table_kernel(rpb_ref, o_ref):
    c = lax.broadcasted_iota(jnp.int32, (GRID_W, LANES), 0)
    lane = lax.broadcasted_iota(jnp.int32, (GRID_W, LANES), 1)
    cc = lane % GRID_W
    cstart = jnp.clip(c - WIN_C // 2, 0, GRID_W - WIN_C)
    for h in range(4):
        for ro in range(N_BIAS_TILES):
            v = pltpu.roll(jnp.broadcast_to(rpb_ref[h, ro:ro + 1, :] * LOG2E, (GRID_W, LANES)),
                           LANES - (WIN_C - 1), 1, stride=1, stride_axis=0)
            o_ref[h, ro] = jnp.where(cc >= cstart, jnp.where(cc < cstart + WIN_C, v, NEG), NEG)


def _nbr_tables(rpb):
    half = LANES // 2 - (2 * WIN_C - 1)
    rows = jnp.pad(rpb, ((0, 0), (0, 0), (0, 1), (0, half)))
    rpb_p = jnp.concatenate([rows[:, :, :-1], rows[:, :, 1:]], axis=-1)
    rpb_p = jnp.pad(rpb_p, ((0, 0), (0, 0), (0, 1), (0, 0)))
    return pl.pallas_call(
        _nbr_table_kernel,
        grid=(DEPTH,),
        in_specs=[pl.BlockSpec((None, 4, 2 * WIN_R, LANES), lambda l: (l, 0, 0, 0))],
        out_specs=pl.BlockSpec((None, 4, N_BIAS_TILES, GRID_W, LANES), lambda l: (l, 0, 0, 0, 0)),
        out_shape=jax.ShapeDtypeStruct((DEPTH, 4, N_BIAS_TILES, GRID_W, LANES), F32),
        compiler_params=pltpu.CompilerParams(dimension_semantics=("arbitrary",)),
        name="nbr_tables",
    )(rpb_p)


def _ctx_kernel(sink_ref, x_ref, mod_ref, g_ref, fg_ref, w_in_ref, w_out_ref, ws_ref, ba_ref, wc_ref,
                y_ref, ck_ref, cv_ref, dk_ref, dv_ref, xs):
    l = pl.program_id(0)
    pair = pl.program_id(1)

    @pl.when(l == 0)
    def _():
        for u in range(CTX_SEQS):
            xs[CTX_SEQS * pair + u] = x_ref[u]

    low = _low_half(SEQ)
    ones = jnp.ones((SEQ, LANES), BF16)

    def project(u):
        x = xs[CTX_SEQS * pair + u]
        h = _norm_mod(x, g_ref[...], mod_ref[0, 0:1, :], mod_ref[1, 0:1, :]).astype(BF16)
        return x, _dot(h, w_in_ref[...])

    def mix(u, p):
        kcf = p[:, CKV:CKV + 128]
        vcf = p[:, CKV + 128:CKV + 256]
        kdf = p[:, DK:DK + 256]
        vdf = p[:, DV:DV + 256]
        ck_ref[u] = kcf
        cv_ref[u] = vcf
        dk_ref[u] = kdf
        dv_ref[u] = vdf

        ya = _branch_a(p[:, AU:AU + 256], p[:, AV:AV + 256], p[:, AZ:AZ + 256], ws_ref[...], ba_ref[...])
        zero_row = jnp.zeros((1, GROUP_W), F32)
        yb = _branch_b(p[:, BB:BB + 256], p[:, BC:BC + 256], p[:, BH:BH + 256], p[:, BZ:BZ + 256],
                       wc_ref[...], zero_row, zero_row)

        kc = (kcf.astype(BF16), _swap64(kcf).astype(BF16))
        vc = (jnp.concatenate([vcf.astype(BF16), ones], axis=1),
              jnp.concatenate([ones, _swap64(vcf).astype(BF16)], axis=1))
        o = []
        for t in range(2):
            q = (p[:, CQ + t * LANES:CQ + (t + 1) * LANES] * Q_SCALE).astype(BF16)
            for half in range(2):
                sw = (t + half) % 2
                o.append(_attend(_dot_nt(_mask_half(q, low, half), kc[sw]), vc[sw], sw,
                                 sink_ref[l, 2 * t + half]))
        yc = jnp.concatenate([jnp.where(low, o[0], o[1]), jnp.where(low, o[2], o[3])], axis=1)
        yc = yc * _silu(p[:, CZ:CZ + 256])

        o = []
        for t in range(2):
            q = (p[:, DQ + t * LANES:DQ + (t + 1) * LANES] * Q_SCALE).astype(BF16)
            k = kdf[:, t * LANES:(t + 1) * LANES].astype(BF16)
            v = jnp.concatenate([vdf[:, t * LANES:(t + 1) * LANES].astype(BF16), ones], axis=1)
            for half in range(2):
                o.append(_attend(_dot_nt(_mask_half(q, low, half), k), v, 0))
        yd = jnp.concatenate([jnp.where(low, o[0], o[1]), jnp.where(low, o[2], o[3])], axis=1)
        yd = yd * _silu(p[:, DZ:DZ + 256])
        return ya, yb, yc, yd

    def finish(u, x, ys):
        x_new = _out_proj(x, mod_ref[2, 0:1, :], *ys, w_out_ref)
        xs[CTX_SEQS * pair + u] = x_new
        return x_new

    staged = [project(0)]
    x_new = []
    for u in range(CTX_SEQS):
        if u + 1 < CTX_SEQS:
            staged.append(project(u + 1))
        x, p = staged[u]
        x_new.append(finish(u, x, mix(u, p)))

    @pl.when(l == DEPTH - 1)
    def _():
        for u in range(CTX_SEQS):
            y_ref[u] = _rms(x_new[u]) * fg_ref[...]


def _ctx_layers(x, mod, norm_g, final_g, w_in, w_out, ws_cat, bias_a, w_conv, sink):
    per_layer = lambda *shape: pl.BlockSpec((None,) + shape, lambda l, b: (l,) + (0,) * len(shape))
    state = lambda w: pl.BlockSpec((CTX_SEQS, None, SEQ, w), lambda l, b: (b, l, 0, 0))
    n_steps = BATCH // CTX_SEQS
    return pl.pallas_call(
        _ctx_kernel,
        grid=(DEPTH, n_steps),
        in_specs=[
            pl.BlockSpec(memory_space=pltpu.SMEM),
            pl.BlockSpec((CTX_SEQS, SEQ, D_MODEL), lambda l, b: (jnp.where(l == 0, b, n_steps - 1), 0, 0)),
            per_layer(3, N_GROUPS, D_MODEL),
            per_layer(1, D_MODEL),
            pl.BlockSpec((1, D_MODEL), lambda l, b: (0, 0)),
            pl.BlockSpec((None, D_MODEL, D_IN), lambda l, b: (l, 0, 0), pipeline_mode=pl.Buffered(1)),
            pl.BlockSpec((None, D_MODEL, D_MODEL), lambda l, b: (l, 0, 0), pipeline_mode=pl.Buffered(1)),
            per_layer(CHUNK, A_HEADS * CHUNK),
            per_layer(CHUNK, GROUP_W),
            per_layer(3, GROUP_W),
        ],
        out_specs=[
            pl.BlockSpec((CTX_SEQS, SEQ, D_MODEL), lambda l, b: (jnp.where(l == DEPTH - 1, b, 0), 0, 0)),
            state(128), state(128), state(256), state(256),
        ],
        out_shape=[
            jax.ShapeDtypeStruct((BATCH, SEQ, D_MODEL), F32),
            jax.ShapeDtypeStruct((BATCH, DEPTH, SEQ, 128), F32),
            jax.ShapeDtypeStruct((BATCH, DEPTH, SEQ, 128), F32),
            jax.ShapeDtypeStruct((BATCH, DEPTH, SEQ, 256), F32),
            jax.ShapeDtypeStruct((BATCH, DEPTH, SEQ, 256), F32),
        ],
        scratch_shapes=[pltpu.VMEM((BATCH, SEQ, D_MODEL), F32)],
        compiler_params=pltpu.CompilerParams(dimension_semantics=("arbitrary", "arbitrary"),
                                             vmem_limit_bytes=VMEM_LIMIT),
        name="ctx_layers",
    )(sink, x, mod, norm_g, final_g, w_in, w_out, ws_cat, bias_a, w_conv)


def _rope(x, cos, sin_signed, first_half):
    swapped = jnp.where(first_half, pltpu.roll(x, LANES - 16, axis=1), pltpu.roll(x, 16, axis=1))
    return x * cos + swapped * sin_signed


def _lat_proj_kernel(x_ref, xp_ref, xn_ref, mod_ref, g_ref, w_ref, cos_ref, sin_ref, ws_ref, ba_ref, wc_ref,
                     yab_ref, sz_ref, qc_ref, kc_ref, vc_ref, qd_ref, kd_ref, vd_ref):
    i = pl.program_id(1)
    grp = pl.ds(1 + pl.program_id(0), 1)
    g = g_ref[...]
    shift = mod_ref[0, grp, :]
    scale = mod_ref[1, grp, :]
    first_half = (lax.broadcasted_iota(jnp.int32, (TILE, LANES), 1) % 32) < 16

    def normalised(x):
        return _norm_mod(x, g, shift, scale).astype(BF16)

    def project(u, h, after_first_dot):
        rows = slice(u * TILE, (u + 1) * TILE)
        tile_index = PROJ_TILES * i + u
        pa = _dot(h, w_ref[:, AU:AU + 3 * GROUP_W])
        after_first_dot()
        ya = _branch_a(pa[:, 0:256], pa[:, 256:512], pa[:, 512:768], ws_ref[...], ba_ref[...])
        yab_ref[rows, 0:GROUP_W] = ya.astype(BF16)

        pb = _dot(h, w_ref[:, BB:BB + 4 * GROUP_W])
        before = xp_ref[...] if u == 0 else x_ref[u * TILE - 8:u * TILE, :]
        after = xn_ref[...] if u == PROJ_TILES - 1 else x_ref[(u + 1) * TILE:(u + 1) * TILE + 8, :]
        e = _dot(normalised(jnp.concatenate([before, after], axis=0)), w_ref[:, BC:BC + 2 * GROUP_W])
        prev_row = jnp.where(tile_index > 0, e[7:8, 0:256] * e[7:8, 256:512], 0.0)
        next_row = jnp.where(tile_index < N_TILES - 1, e[8:9, 0:256] * e[8:9, 256:512], 0.0)
        yb = _branch_b(pb[:, 0:256], pb[:, 256:512], pb[:, 512:768], pb[:, 768:1024], wc_ref[...],
                       prev_row, next_row)
        yab_ref[rows, GROUP_W:2 * GROUP_W] = yb.astype(BF16)

        sz_ref[rows, 0:256] = _silu(_dot(h, w_ref[:, CZ:CZ + 256]))
        sz_ref[rows, 256:512] = _silu(_dot(h, w_ref[:, DZ:DZ + 256]))
        cos = cos_ref[rows, :]
        sin = sin_ref[rows, :]
        qc = _dot(h, w_ref[:, CQ:CQ + 256])
        for t in range(2):
            qt = _rope(qc[:, t * LANES:(t + 1) * LANES], cos, sin, first_half) * Q_SCALE
            qc_ref[rows, t * LANES:(t + 1) * LANES] = qt.astype(BF16)
        kv = _dot(h, w_ref[:, CKV:CKV + 256])
        k = _rope(kv[:, 0:LANES], cos, sin, first_half)
        v = kv[:, LANES:2 * LANES]
        kc_ref[rows, 0:LANES] = k.astype(BF16)
        kc_ref[rows, LANES:2 * LANES] = _swap64(k).astype(BF16)
        vc_ref[rows, 0:LANES] = v.astype(BF16)
        vc_ref[rows, LANES:2 * LANES] = _swap64(v).astype(BF16)
        qd_ref[rows, :] = (_dot(h, w_ref[:, DQ:DQ + 256]) * Q_SCALE).astype(BF16)
        kd_ref[rows, :] = _dot(h, w_ref[:, DK:DK + 256]).astype(BF16)
        vd_ref[rows, :] = _dot(h, w_ref[:, DV:DV + 256]).astype(BF16)

    hs = [normalised(x_ref[0:TILE, :])]

    def normalise_next(u):
        if u + 1 < PROJ_TILES:
            hs.append(normalised(x_ref[(u + 1) * TILE:(u + 2) * TILE, :]))

    for u in range(PROJ_TILES):
        project(u, hs[u], functools.partial(normalise_next, u))


def _lat_proj(layer, x, mod, norm_g, w_in, cos_t, sin_t, ws_cat, bias_a, w_conv):
    step_rows = PROJ_TILES * TILE
    tile = lambda w: pl.BlockSpec((None, step_rows, w), lambda b, i: (b, i, 0))
    per_layer = lambda *shape: pl.BlockSpec((None,) + shape, lambda b, i: (layer,) + (0,) * len(shape))
    sds = lambda w, dt: jax.ShapeDtypeStruct((DEC_BATCH, DEC_SEQ, w), dt)
    edge = step_rows // 8
    last = DEC_SEQ // 8 - 1
    return pl.pallas_call(
        _lat_proj_kernel,
        grid=(DEC_BATCH, N_TILES // PROJ_TILES),
        in_specs=[
            tile(D_MODEL),
            pl.BlockSpec((None, 8, D_MODEL), lambda b, i: (b, jnp.maximum(i * edge - 1, 0), 0)),
            pl.BlockSpec((None, 8, D_MODEL), lambda b, i: (b, jnp.minimum((i + 1) * edge, last), 0)),
            per_layer(3, N_GROUPS, D_MODEL),
            per_layer(1, D_MODEL),
            per_layer(D_MODEL, D_IN),
            pl.BlockSpec((step_rows, LANES), lambda b, i: (i, 0)),
            pl.BlockSpec((step_rows, LANES), lambda b, i: (i, 0)),
            per_layer(CHUNK, A_HEADS * CHUNK),
            per_layer(CHUNK, GROUP_W),
            per_layer(3, GROUP_W),
        ],
        out_specs=[tile(512), tile(512)] + [tile(256)] * 6,
        out_shape=[sds(512, BF16), sds(512, F32)] + [sds(256, BF16)] * 6,
        compiler_params=pltpu.CompilerParams(dimension_semantics=("arbitrary", "arbitrary"),
                                             vmem_limit_bytes=VMEM_LIMIT),
        name="lat_proj",
    )(x, x, x, mod, norm_g, w_in, cos_t, sin_t, ws_cat, bias_a, w_conv)


_C_GROUPS = (((0, 0), (1, 1)), ((0, 1), (1, 0)))
_D_GROUPS = (((0, 0), (0, 1)), ((1, 0), (1, 1)))


def _stacked_queries(q_ref, rows, low, groups):
    return [jnp.concatenate([_mask_half(q_ref[rows, t * LANES:(t + 1) * LANES], low, half)
                             for t, half in group], axis=0) for group in groups]


def _lat_mix_kernel(layer, sink_ref, x_ref, mod_ref, fg_ref, yab_ref, sz_ref,
                    qc_ref, kc_ref, kcp_ref, kcn_ref, vc_ref, vcp_ref, vcn_ref,
                    qd_ref, kd_ref, kdp_ref, kdn_ref, vd_ref, vdp_ref, vdn_ref,
                    cck_ref, ccv_ref, cdk_ref, cdv_ref, cmask_ref, tab_ref, w_out_ref,
                    xo_ref,
                    kce, vce, kde, vde, yc_scr, yd_scr, m_ctx, acc_ctx, *stage_bufs):
    b = pl.program_id(0)
    i = pl.program_id(1)
    s_c, s_d = stage_bufs[0:2], stage_bufs[2:4]

    def file_context(m, acc, group, step_rows, n_steps):
        for which in range(2):
            for k in range(n_steps):
                src = slice(which * TILE + k * step_rows, which * TILE + (k + 1) * step_rows)
                dst = slice((4 * k + 2 * group + which) * step_rows, (4 * k + 2 * group + which + 1) * step_rows)
                m_ctx[dst, :] = m[src]
                acc_ctx[dst, :] = acc[src]

    kce[0:QBLK, :] = kcp_ref[...]
    kce[QBLK:QBLK + TILE, :] = kc_ref[...]
    kce[QBLK + TILE:, :] = kcn_ref[...]
    for c_src, c_dst in ((0, 0), (LANES, 2 * LANES)):
        vce[0:QBLK, c_dst:c_dst + LANES] = vcp_ref[:, c_src:c_src + LANES]
        vce[QBLK:QBLK + TILE, c_dst:c_dst + LANES] = vc_ref[:, c_src:c_src + LANES]
        vce[QBLK + TILE:, c_dst:c_dst + LANES] = vcn_ref[:, c_src:c_src + LANES]
    vce[:, LANES:2 * LANES] = jnp.ones((TILE + 2 * QBLK, LANES), BF16)
    swap_rows = lambda x: jnp.concatenate([x[HEAD_DIM:], x[:HEAD_DIM]], axis=0)
    cck = cck_ref[...].astype(BF16)
    ccv = ccv_ref[...].astype(BF16)
    ones_ctx = jnp.ones((LANES, PAST_LEN), BF16)
    k_ctx = (cck, swap_rows(cck))
    v_ctx = jnp.concatenate([ccv, ones_ctx, swap_rows(ccv)], axis=0)
    low_q = _low_half(QBLK)
    low_t = _low_half(TILE)
    n_blocks = DEC_SEQ // QBLK
    blocks_per_tile = TILE // QBLK

    for sw, q in enumerate(_stacked_queries(qc_ref, slice(None), low_t, _C_GROUPS)):
        m, acc = _context_pass(q, k_ctx[sw], v_ctx[sw * LANES:(sw + 2) * LANES, :])
        file_context(m, acc, sw, QBLK, blocks_per_tile)

    def c_scores(j):
        n = i * blocks_per_tile + j
        window = cmask_ref[jnp.where(n == 0, 1, jnp.where(n == n_blocks - 1, 2, 0))]
        window = jnp.concatenate([window, window], axis=0)
        rows = slice(j * QBLK, (j + 1) * QBLK)
        for sw, q in enumerate(_stacked_queries(qc_ref, rows, low_q, _C_GROUPS)):
            k_loc = kce[j * QBLK:(j + 3) * QBLK, sw * LANES:(sw + 1) * LANES]
            s_c[j % 2][2 * sw * QBLK:2 * (sw + 1) * QBLK, :] = _dot_nt(q, k_loc) + window

    def c_softmax(j):
        step = slice(4 * j * QBLK, 4 * (j + 1) * QBLK)
        sink = jnp.concatenate([jnp.full((QBLK, 1), sink_ref[layer, 2 * t + half] * LOG2E, F32)
                                for group in _C_GROUPS for t, half in group], axis=0)
        e, ctx_scale, sink_term = _local_softmax(s_c[j % 2][...], m_ctx[step, :], sink)
        values = [(vce[j * QBLK:(j + 3) * QBLK, sw * LANES:(sw + 2) * LANES], sw) for sw in range(2)]
        r0, r1 = _local_output(e, values, acc_ctx[step, :], ctx_scale, sink_term)
        yc_scr[j * QBLK:(j + 1) * QBLK, :] = jnp.concatenate(
            [jnp.where(low_q, r0[0:QBLK], r1[0:QBLK]), jnp.where(low_q, r1[QBLK:], r0[QBLK:])], axis=1)

    _skewed(blocks_per_tile, (c_scores, c_softmax))

    halo = 4 * GRID_W
    kde[0:halo, :] = kdp_ref[...]
    kde[halo:halo + TILE, :] = kd_ref[...]
    kde[halo + TILE:, :] = kdn_ref[...]
    for c_src, c_dst in ((0, 0), (LANES, 2 * LANES)):
        vde[0:halo, c_dst:c_dst + LANES] = vdp_ref[:, c_src:c_src + LANES]
        vde[halo:halo + TILE, c_dst:c_dst + LANES] = vd_ref[:, c_src:c_src + LANES]
        vde[halo + TILE:, c_dst:c_dst + LANES] = vdn_ref[:, c_src:c_src + LANES]
    vde[:, LANES:2 * LANES] = jnp.ones((TILE + 2 * halo, LANES), BF16)
    kd_ctx = cdk_ref[...].astype(BF16)
    cdv = cdv_ref[...].astype(BF16)
    vd_ctx = jnp.concatenate([cdv[0:LANES], ones_ctx, cdv[LANES:2 * LANES]], axis=0)
    low_r = _low_half(GRID_W)
    n_rows = DEC_SEQ // GRID_W

    for t, q in enumerate(_stacked_queries(qd_ref, slice(None), low_t, _D_GROUPS)):
        m, acc = _context_pass(q, kd_ctx[t * LANES:(t + 1) * LANES], vd_ctx[t * LANES:(t + 2) * LANES])
        file_context(m, acc, t, GRID_W, ROWS_PER_TILE)

    def window_start(rl):
        r = i * ROWS_PER_TILE + rl
        rs = jnp.clip(r - WIN_R // 2, 0, n_rows - WIN_R)
        return rs - r + (WIN_R - 1), pl.multiple_of((rs - i * ROWS_PER_TILE + 4) * GRID_W, GRID_W)

    def d_scores(rl):
        didx, off = window_start(rl)
        rows = slice(rl * GRID_W, (rl + 1) * GRID_W)
        for t, q in enumerate(_stacked_queries(qd_ref, rows, low_r, _D_GROUPS)):
            k_loc = kde[pl.ds(off, WIN_R * GRID_W), t * LANES:(t + 1) * LANES]
            bias = jnp.concatenate(
                [jnp.concatenate([tab_ref[h, didx + 2 * p] for p in range(WIN_R // 2)], axis=1)
                 for h in (2 * t, 2 * t + 1)], axis=0)
            s_d[rl % 2][2 * t * GRID_W:2 * (t + 1) * GRID_W, :] = _dot_nt(q, k_loc) + bias

    def d_softmax(rl):
        _, off = window_start(rl)
        step = slice(4 * rl * GRID_W, 4 * (rl + 1) * GRID_W)
        e, ctx_scale, _ = _local_softmax(s_d[rl % 2][...], m_ctx[step, :])
        values = [(vde[pl.ds(off, WIN_R * GRID_W), t * LANES:(t + 2) * LANES], t) for t in range(2)]
        r = _local_output(e, values, acc_ctx[step, :], ctx_scale)
        yd_scr[rl * GRID_W:(rl + 1) * GRID_W, :] = jnp.concatenate(
            [jnp.where(low_r, r[t][0:GRID_W], r[t][GRID_W:]) for t in range(2)], axis=1)

    _skewed(ROWS_PER_TILE, (d_scores, d_softmax))

    yc = (yc_scr[...] * sz_ref[:, 0:256]).astype(BF16)
    yd = (yd_scr[...] * sz_ref[:, 256:512]).astype(BF16)
    y = (_dot(yab_ref[...], w_out_ref[0:2 * GROUP_W, :]) + _dot(yc, w_out_ref[2 * GROUP_W:3 * GROUP_W, :])
         + _dot(yd, w_out_ref[3 * GROUP_W:4 * GROUP_W, :]))
    x_new = x_ref[...] + mod_ref[2, pl.ds(1 + b, 1), :] * y
    if layer == DEPTH - 1:
        x_new = _rms(x_new) * fg_ref[...]
    xo_ref[...] = x_new


def _lat_mix(layer, x, mod, final_g, yab, sz, qc, kc, vc, qd, kd, vd, cck, ccv, cdk, cdv, tab, w_out, sink):
    tile = lambda w: pl.BlockSpec((None, TILE, w), lambda b, i: (b, i, 0))
    per_layer = lambda *shape: pl.BlockSpec((None,) + shape, lambda b, i: (layer,) + (0,) * len(shape))

    def halo(rows, w):
        per = TILE // rows
        last = DEC_SEQ // rows - 1
        prev = pl.BlockSpec((None, rows, w), lambda b, i: (b, jnp.maximum(i * per - 1, 0), 0))
        nxt = pl.BlockSpec((None, rows, w), lambda b, i: (b, jnp.minimum((i + 1) * per, last), 0))
        return prev, nxt

    cprev, cnext = halo(QBLK, 256)
    dprev, dnext = halo(4 * GRID_W, 256)
    cache = lambda w: pl.BlockSpec((None, None, w, PAST_LEN), lambda b, i: (b, layer, 0, 0))
    return pl.pallas_call(
        functools.partial(_lat_mix_kernel, layer),
        grid=(DEC_BATCH, N_TILES),
        in_specs=[
            pl.BlockSpec(memory_space=pltpu.SMEM),
            tile(D_MODEL),
            per_layer(3, N_GROUPS, D_MODEL),
            pl.BlockSpec((1, D_MODEL), lambda b, i: (0, 0)),
            tile(512), tile(512),
            tile(256), tile(256), cprev, cnext, tile(256), cprev, cnext,
            tile(256), tile(256), dprev, dnext, tile(256), dprev, dnext,
            cache(128), cache(128), cache(256), cache(256),
            pl.BlockSpec((3, QBLK, 3 * QBLK), lambda b, i: (0, 0, 0)),
            per_layer(4, N_BIAS_TILES, GRID_W, LANES),
            per_layer(D_MODEL, D_MODEL),
        ],
        out_specs=tile(D_MODEL),
        out_shape=jax.ShapeDtypeStruct((DEC_BATCH, DEC_SEQ, D_MODEL), F32),
        scratch_shapes=[
            pltpu.VMEM((TILE + 2 * QBLK, 2 * LANES), BF16),
            pltpu.VMEM((TILE + 2 * QBLK, 3 * LANES), BF16),
            pltpu.VMEM((TILE + 8 * GRID_W, 2 * LANES), BF16),
            pltpu.VMEM((TILE + 8 * GRID_W, 3 * LANES), BF16),
            pltpu.VMEM((TILE, 256), F32),
            pltpu.VMEM((TILE, 256), F32),
            pltpu.VMEM((4 * TILE, 1), F32),
            pltpu.VMEM((4 * TILE, 2 * LANES), F32),
        ] + 2 * [pltpu.VMEM((4 * QBLK, 3 * QBLK), F32)] + 2 * [pltpu.VMEM((4 * GRID_W, WIN_R * GRID_W), F32)],
        compiler_params=pltpu.CompilerParams(dimension_semantics=("arbitrary", "arbitrary"),
                                             vmem_limit_bytes=VMEM_LIMIT),
        name="lat_mix",
    )(sink, x, mod, final_g, yab, sz, qc, kc, kc, kc, vc, vc, vc,
      qd, kd, kd, kd, vd, vd, vd, cck, ccv, cdk, cdv, _window_masks(), tab, w_out)


def _window_masks():
    p = np.arange(QBLK)[:, None]
    j = np.arange(3 * QBLK)[None, :]
    band = np.abs(j - QBLK - p) <= WINDOW
    masks = [band, band & (j >= QBLK), band & (j < 2 * QBLK)]
    return jnp.asarray(np.where(np.stack(masks), 0.0, NEG), F32)


def _rope_tables():
    t = np.arange(DEC_SEQ)
    freqs = (np.float32(ROPE_BASE) ** (-np.arange(16, dtype=np.float32) / np.float32(16))).astype(np.float32)
    ang_r = (t // GRID_W).astype(np.float32)[:, None] * freqs
    ang_c = (t % GRID_W).astype(np.float32)[:, None] * freqs
    cos_h = np.concatenate([np.cos(ang_r), np.cos(ang_r), np.cos(ang_c), np.cos(ang_c)], axis=1)
    sin_h = np.concatenate([-np.sin(ang_r), np.sin(ang_r), -np.sin(ang_c), np.sin(ang_c)], axis=1)
    return (jnp.asarray(np.tile(cos_h, (1, 2)), F32), jnp.asarray(np.tile(sin_h, (1, 2)), F32))


def kernel(x_prompt, x_sample, cache_c_k, cache_c_v, cache_d_k, cache_d_v, c, c_ctx, norm_g, w_mod, b_mod,
           w_in, w_out, w_s, b_s, w_conv, sink, rpb, final_g):
    w_in_b = w_in.astype(BF16)
    w_out_b = w_out.astype(BF16)
    ws_cat = jnp.transpose(w_s, (0, 2, 1, 3)).reshape(DEPTH, CHUNK, A_HEADS * CHUNK).astype(BF16)
    bias_a = jnp.repeat(jnp.transpose(b_s, (0, 2, 1)), HEAD_DIM, axis=2)
    norm_g3 = norm_g.reshape(DEPTH, 1, D_MODEL)
    fg = final_g.reshape(1, D_MODEL)
    cos_t, sin_t = _rope_tables()
    feature_major = lambda cache, w: jnp.transpose(cache, (0, 1, 3, 4, 2)).reshape(DEC_BATCH, DEPTH, w, PAST_LEN)
    cck = feature_major(cache_c_k, 128)
    ccv = feature_major(cache_c_v, 128)
    cdk = feature_major(cache_d_k, 256)
    cdv = feature_major(cache_d_v, 256)

    cond = jnp.concatenate([c_ctx[None, :], c, jnp.zeros((N_GROUPS - 1 - DEC_BATCH, D_MODEL), F32)], axis=0)
    mod = _modulation(cond, w_mod, b_mod)
    tab = _nbr_tables(rpb)

    y_prompt, s_ck, s_cv, s_dk, s_dv = _ctx_layers(x_prompt, mod, norm_g3, fg, w_in_b, w_out_b, ws_cat,
                                                   bias_a, w_conv, sink)
    xs = x_sample
    for l in range(DEPTH):
        yab, sz, qc, kc, vc, qd, kd, vd = _lat_proj(l, xs, mod, norm_g3, w_in_b, cos_t, sin_t,
                                                    ws_cat, bias_a, w_conv)
        xs = _lat_mix(l, xs, mod, fg, yab, sz, qc, kc, vc, qd, kd, vd, cck, ccv, cdk, cdv, tab, w_out_b, sink)

    shape_c = (BATCH, DEPTH, SEQ, 2, HEAD_DIM)
    shape_d = (BATCH, DEPTH, SEQ, 4, HEAD_DIM)
    return (y_prompt, xs, s_ck.reshape(shape_c), s_cv.reshape(shape_c), s_dk.reshape(shape_d),
            s_dv.reshape(shape_d))
```

```python
import functools

import numpy as np
import jax
import jax.numpy as jnp
from jax import lax
from jax.experimental import pallas as pl
from jax.experimental.pallas import tpu as pltpu

D_MODEL = 1024
BATCH = 16
SEQ = 256
DEPTH = 4
DEC_BATCH = 2
DEC_SEQ = 4096
PAST_LEN = 512
GRID_W = 64
GROUP_W = 256
HEAD_DIM = 64
A_HEADS = 4
CHUNK = 128
WINDOW = 128
QBLK = 128
WIN_R = 8
WIN_C = 16
ROPE_BASE = 10000.0
EPS = 1e-6
NEG = -1e30
D_IN = 3584
LOG2E = 1.4426950408889634
Q_SCALE = HEAD_DIM ** -0.5 * LOG2E

AU, AV, AZ, BB, BC, BH, BZ = 0, 256, 512, 768, 1024, 1280, 1536
CQ, CKV, CZ, DQ, DK, DV, DZ = 1792, 2048, 2304, 2560, 2816, 3072, 3328

LANES = 128
TILE = 512
ROWS_PER_TILE = TILE // GRID_W
N_TILES = DEC_SEQ // TILE
CTX_SEQS = 4
PROJ_TILES = 2
N_GROUPS = 8
VMEM_LIMIT = 56 * 1024 * 1024

F32 = jnp.float32
BF16 = jnp.bfloat16


def _silu(z):
    return z * (1.0 / (1.0 + jnp.exp(-z)))


def _dot(a, b):
    return jnp.dot(a, b, preferred_element_type=F32)


def _dot_nt(a, b):
    return lax.dot_general(a, b, (((1,), (1,)), ((), ())), preferred_element_type=F32)


def _rms(x):
    return x * lax.rsqrt(jnp.mean(x * x, axis=-1, keepdims=True) + EPS)


def _norm_mod(x, g, shift, scale):
    return _rms(x) * (g * (1.0 + scale)) + shift


def _swap64(x):
    return pltpu.roll(x, HEAD_DIM, axis=1)


def _group_mean_matrix():
    r = lax.broadcasted_iota(jnp.int32, (GROUP_W, GROUP_W), 0) // HEAD_DIM
    c = lax.broadcasted_iota(jnp.int32, (GROUP_W, GROUP_W), 1) // HEAD_DIM
    return jnp.where(r == c, 1.0 / HEAD_DIM, 0.0).astype(BF16)


def _branch_a(au, av, az, ws, bias):
    t = av.shape[0]
    sq = av * av
    hi = sq.astype(BF16)
    lo = (sq - hi.astype(F32)).astype(BF16)
    gm = _group_mean_matrix()
    ms = _dot(hi, gm) + _dot(lo, gm)
    vh = (av * lax.rsqrt(ms + EPS)).astype(BF16)
    head = lax.broadcasted_iota(jnp.int32, (CHUNK, GROUP_W), 1) // HEAD_DIM
    outs = []
    for n in range(t // CHUNK):
        v = vh[n * CHUNK:(n + 1) * CHUNK]
        rhs = jnp.concatenate([jnp.where(head == h, v, jnp.zeros_like(v)) for h in range(A_HEADS)], axis=0)
        outs.append(_dot(ws, rhs) + bias)
    mixed = jnp.concatenate(outs, axis=0)
    return au * mixed * _silu(az)


def _branch_b(bb, bc, bh, bz, wc, prev_row, next_row):
    t = bb.shape[0]
    xc = bc * bh
    row = lax.broadcasted_iota(jnp.int32, xc.shape, 0)
    xm = jnp.where(row == 0, prev_row, pltpu.roll(xc, 1, axis=0))
    xp = jnp.where(row == t - 1, next_row, pltpu.roll(xc, t - 1, axis=0))
    y = wc[0:1, :] * xm + wc[1:2, :] * xc + wc[2:3, :] * xp
    return bb * y * _silu(bz)


def _attend(score, value, pos, sink=None):
    m = jnp.max(score, axis=-1, keepdims=True)
    if sink is not None:
        m = jnp.maximum(m, sink * LOG2E)
    acc = _dot(jnp.exp2(score - m).astype(BF16), value)
    o = acc[:, pos * LANES:(pos + 1) * LANES]
    l = acc[:, (1 - pos) * LANES:(2 - pos) * LANES]
    if sink is not None:
        l = l + jnp.exp2(sink * LOG2E - m)
    return o * (1.0 / l)


def _context_pass(q_stacked, k_t, value):
    s = _dot(q_stacked, k_t)
    m = jnp.max(s, axis=-1, keepdims=True)
    return m, _dot(jnp.exp2(s - m).astype(BF16), value)


def _local_softmax(s_loc, m_ctx, sink=None):
    m = jnp.maximum(jnp.max(s_loc, axis=-1, keepdims=True), m_ctx)
    if sink is not None:
        m = jnp.maximum(m, sink)
    e = jnp.exp2(s_loc - m).astype(BF16)
    return e, jnp.exp2(m_ctx - m), (None if sink is None else jnp.exp2(sink - m))


def _local_output(e, values, acc_ctx, ctx_scale, sink_term=None):
    rows = e.shape[0] // len(values)
    outs = []
    for p, (v, pos) in enumerate(values):
        r = slice(p * rows, (p + 1) * rows)
        acc = _dot(e[r], v) + acc_ctx[r] * ctx_scale[r]
        o = acc[:, pos * LANES:(pos + 1) * LANES]
        l = acc[:, (1 - pos) * LANES:(2 - pos) * LANES]
        if sink_term is not None:
            l = l + sink_term[r]
        outs.append(o * (1.0 / l))
    return outs


def _skewed(n_steps, stages):
    for tick in range(n_steps + len(stages) - 1):
        for lag, stage in enumerate(stages):
            if 0 <= tick - lag < n_steps:
                stage(tick - lag)


def _low_half(m):
    return lax.broadcasted_iota(jnp.int32, (m, LANES), 1) < HEAD_DIM


def _mask_half(q, low, half):
    z = jnp.zeros_like(q)
    return jnp.where(low, q, z) if half == 0 else jnp.where(low, z, q)


def _out_proj(x, gate, ya, yb, yc, yd, w_out_ref):
    y = _dot(ya.astype(BF16), w_out_ref[0:256, :])
    y += _dot(yb.astype(BF16), w_out_ref[256:512, :])
    y += _dot(yc.astype(BF16), w_out_ref[512:768, :])
    y += _dot(yd.astype(BF16), w_out_ref[768:1024, :])
    return x + gate * y


def _mod_kernel(cond_ref, w_ref, b_ref, o_ref):
    mod = _dot(_silu(cond_ref[...]).astype(BF16), w_ref[...].astype(BF16)) + b_ref[...]
    for j in range(3):
        o_ref[j] = mod[:, j * D_MODEL:(j + 1) * D_MODEL]


def _modulation(cond, w_mod, b_mod):
    return pl.pallas_call(
        _mod_kernel,
        grid=(DEPTH,),
        in_specs=[
            pl.BlockSpec((N_GROUPS, D_MODEL), lambda l: (0, 0)),
            pl.BlockSpec((None, D_MODEL, 3 * D_MODEL), lambda l: (l, 0, 0)),
            pl.BlockSpec((None, 1, 3 * D_MODEL), lambda l: (l, 0, 0)),
        ],
        out_specs=pl.BlockSpec((None, 3, N_GROUPS, D_MODEL), lambda l: (l, 0, 0, 0)),
        out_shape=jax.ShapeDtypeStruct((DEPTH, 3, N_GROUPS, D_MODEL), F32),
        compiler_params=pltpu.CompilerParams(dimension_semantics=("arbitrary",),
                                             vmem_limit_bytes=VMEM_LIMIT),
        name="modulation",
    )(cond, w_mod, b_mod.reshape(DEPTH, 1, 3 * D_MODEL))


N_BIAS_TILES = 2 * WIN_R - 2


def _nbr_table_kernel(rpb_ref, o_ref):
    c = lax.broadcasted_iota(jnp.int32, (GRID_W, LANES), 0)
    lane = lax.broadcasted_iota(jnp.int32, (GRID_W, LANES), 1)
    cc = lane % GRID_W
    cstart = jnp.clip(c - WIN_C // 2, 0, GRID_W - WIN_C)
    for h in range(4):
        for ro in range(N_BIAS_TILES):
            v = pltpu.roll(jnp.broadcast_to(rpb_ref[h, ro:ro + 1, :] * LOG2E, (GRID_W, LANES)),
                           LANES - (WIN_C - 1), 1, stride=1, stride_axis=0)
            o_ref[h, ro] = jnp.where(cc >= cstart, jnp.where(cc < cstart + WIN_C, v, NEG), NEG)


def _nbr_tables(rpb):
    half = LANES // 2 - (2 * WIN_C - 1)
    rows = jnp.pad(rpb, ((0, 0), (0, 0), (0, 1), (0, half)))
    rpb_p = jnp.concatenate([rows[:, :, :-1], rows[:, :, 1:]], axis=-1)
    rpb_p = jnp.pad(rpb_p, ((0, 0), (0, 0), (0, 1), (0, 0)))
    return pl.pallas_call(
        _nbr_table_kernel,
        grid=(DEPTH,),
        in_specs=[pl.BlockSpec((None, 4, 2 * WIN_R, LANES), lambda l: (l, 0, 0, 0))],
        out_specs=pl.BlockSpec((None, 4, N_BIAS_TILES, GRID_W, LANES), lambda l: (l, 0, 0, 0, 0)),
        out_shape=jax.ShapeDtypeStruct((DEPTH, 4, N_BIAS_TILES, GRID_W, LANES), F32),
        compiler_params=pltpu.CompilerParams(dimension_semantics=("arbitrary",)),
        name="nbr_tables",
    )(rpb_p)


def _ctx_kernel(sink_ref, x_ref, mod_ref, g_ref, fg_ref, w_in_ref, w_out_ref, ws_ref, ba_ref, wc_ref,
                y_ref, ck_ref, cv_ref, dk_ref, dv_ref, xs):
    l = pl.program_id(0)
    pair = pl.program_id(1)

    @pl.when(l == 0)
    def _():
        for u in range(CTX_SEQS):
            xs[CTX_SEQS * pair + u] = x_ref[u]

    low = _low_half(SEQ)
    ones = jnp.ones((SEQ, LANES), BF16)

    def project(u):
        x = xs[CTX_SEQS * pair + u]
        h = _norm_mod(x, g_ref[...], mod_ref[0, 0:1, :], mod_ref[1, 0:1, :]).astype(BF16)
        return x, _dot(h, w_in_ref[...])

    def mix(u, p):
        kcf = p[:, CKV:CKV + 128]
        vcf = p[:, CKV + 128:CKV + 256]
        kdf = p[:, DK:DK + 256]
        vdf = p[:, DV:DV + 256]
        ck_ref[u] = kcf
        cv_ref[u] = vcf
        dk_ref[u] = kdf
        dv_ref[u] = vdf

        ya = _branch_a(p[:, AU:AU + 256], p[:, AV:AV + 256], p[:, AZ:AZ + 256], ws_ref[...], ba_ref[...])
        zero_row = jnp.zeros((1, GROUP_W), F32)
        yb = _branch_b(p[:, BB:BB + 256], p[:, BC:BC + 256], p[:, BH:BH + 256], p[:, BZ:BZ + 256],
                       wc_ref[...], zero_row, zero_row)

        kc = (kcf.astype(BF16), _swap64(kcf).astype(BF16))
        vc = (jnp.concatenate([vcf.astype(BF16), ones], axis=1),
              jnp.concatenate([ones, _swap64(vcf).astype(BF16)], axis=1))
        o = []
        for t in range(2):
            q = (p[:, CQ + t * LANES:CQ + (t + 1) * LANES] * Q_SCALE).astype(BF16)
            for half in range(2):
                sw = (t + half) % 2
                o.append(_attend(_dot_nt(_mask_half(q, low, half), kc[sw]), vc[sw], sw,
                                 sink_ref[l, 2 * t + half]))
        yc = jnp.concatenate([jnp.where(low, o[0], o[1]), jnp.where(low, o[2], o[3])], axis=1)
        yc = yc * _silu(p[:, CZ:CZ + 256])

        o = []
        for t in range(2):
            q = (p[:, DQ + t * LANES:DQ + (t + 1) * LANES] * Q_SCALE).astype(BF16)
            k = kdf[:, t * LANES:(t + 1) * LANES].astype(BF16)
            v = jnp.concatenate([vdf[:, t * LANES:(t + 1) * LANES].astype(BF16), ones], axis=1)
            for half in range(2):
                o.append(_attend(_dot_nt(_mask_half(q, low, half), k), v, 0))
        yd = jnp.concatenate([jnp.where(low, o[0], o[1]), jnp.where(low, o[2], o[3])], axis=1)
        yd = yd * _silu(p[:, DZ:DZ + 256])
        return ya, yb, yc, yd

    def finish(u, x, ys):
        x_new = _out_proj(x, mod_ref[2, 0:1, :], *ys, w_out_ref)
        xs[CTX_SEQS * pair + u] = x_new
        return x_new

    staged = [project(0)]
    x_new = []
    for u in range(CTX_SEQS):
        if u + 1 < CTX_SEQS:
            staged.append(project(u + 1))
        x, p = staged[u]
        x_new.append(finish(u, x, mix(u, p)))

    @pl.when(l == DEPTH - 1)
    def _():
        for u in range(CTX_SEQS):
            y_ref[u] = _rms(x_new[u]) * fg_ref[...]


def _ctx_layers(x, mod, norm_g, final_g, w_in, w_out, ws_cat, bias_a, w_conv, sink):
    per_layer = lambda *shape: pl.BlockSpec((None,) + shape, lambda l, b: (l,) + (0,) * len(shape))
    state = lambda w: pl.BlockSpec((CTX_SEQS, None, SEQ, w), lambda l, b: (b, l, 0, 0))
    n_steps = BATCH // CTX_SEQS
    return pl.pallas_call(
        _ctx_kernel,
        grid=(DEPTH, n_steps),
        in_specs=[
            pl.BlockSpec(memory_space=pltpu.SMEM),
            pl.BlockSpec((CTX_SEQS, SEQ, D_MODEL), lambda l, b: (jnp.where(l == 0, b, n_steps - 1), 0, 0)),
            per_layer(3, N_GROUPS, D_MODEL),
            per_layer(1, D_MODEL),
            pl.BlockSpec((1, D_MODEL), lambda l, b: (0, 0)),
            pl.BlockSpec((None, D_MODEL, D_IN), lambda l, b: (l, 0, 0), pipeline_mode=pl.Buffered(1)),
            pl.BlockSpec((None, D_MODEL, D_MODEL), lambda l, b: (l, 0, 0), pipeline_mode=pl.Buffered(1)),
            per_layer(CHUNK, A_HEADS * CHUNK),
            per_layer(CHUNK, GROUP_W),
            per_layer(3, GROUP_W),
        ],
        out_specs=[
            pl.BlockSpec((CTX_SEQS, SEQ, D_MODEL), lambda l, b: (jnp.where(l == DEPTH - 1, b, 0), 0, 0)),
            state(128), state(128), state(256), state(256),
        ],
        out_shape=[
            jax.ShapeDtypeStruct((BATCH, SEQ, D_MODEL), F32),
            jax.ShapeDtypeStruct((BATCH, DEPTH, SEQ, 128), F32),
            jax.ShapeDtypeStruct((BATCH, DEPTH, SEQ, 128), F32),
            jax.ShapeDtypeStruct((BATCH, DEPTH, SEQ, 256), F32),
            jax.ShapeDtypeStruct((BATCH, DEPTH, SEQ, 256), F32),
        ],
        scratch_shapes=[pltpu.VMEM((BATCH, SEQ, D_MODEL), F32)],
        compiler_params=pltpu.CompilerParams(dimension_semantics=("arbitrary", "arbitrary"),
                                             vmem_limit_bytes=VMEM_LIMIT),
        name="ctx_layers",
    )(sink, x, mod, norm_g, final_g, w_in, w_out, ws_cat, bias_a, w_conv)


def _rope(x, cos, sin_signed, first_half):
    swapped = jnp.where(first_half, pltpu.roll(x, LANES - 16, axis=1), pltpu.roll(x, 16, axis=1))
    return x * cos + swapped * sin_signed


def _lat_proj_kernel(x_ref, xp_ref, xn_ref, mod_ref, g_ref, w_ref, cos_ref, sin_ref, ws_ref, ba_ref, wc_ref,
                     yab_ref, sz_ref, qc_ref, kc_ref, vc_ref, qd_ref, kd_ref, vd_ref):
    i = pl.program_id(1)
    grp = pl.ds(1 + pl.program_id(0), 1)
    g = g_ref[...]
    shift = mod_ref[0, grp, :]
    scale = mod_ref[1, grp, :]
    first_half = (lax.broadcasted_iota(jnp.int32, (TILE, LANES), 1) % 32) < 16

    def normalised(x):
        return _norm_mod(x, g, shift, scale).astype(BF16)

    def project(u, h, after_first_dot):
        rows = slice(u * TILE, (u + 1) * TILE)
        tile_index = PROJ_TILES * i + u
        pa = _dot(h, w_ref[:, AU:AU + 3 * GROUP_W])
        after_first_dot()
        ya = _branch_a(pa[:, 0:256], pa[:, 256:512], pa[:, 512:768], ws_ref[...], ba_ref[...])
        yab_ref[rows, 0:GROUP_W] = ya.astype(BF16)

        pb = _dot(h, w_ref[:, BB:BB + 4 * GROUP_W])
        before = xp_ref[...] if u == 0 else x_ref[u * TILE - 8:u * TILE, :]
        after = xn_ref[...] if u == PROJ_TILES - 1 else x_ref[(u + 1) * TILE:(u + 1) * TILE + 8, :]
        e = _dot(normalised(jnp.concatenate([before, after], axis=0)), w_ref[:, BC:BC + 2 * GROUP_W])
        prev_row = jnp.where(tile_index > 0, e[7:8, 0:256] * e[7:8, 256:512], 0.0)
        next_row = jnp.where(tile_index < N_TILES - 1, e[8:9, 0:256] * e[8:9, 256:512], 0.0)
        yb = _branch_b(pb[:, 0:256], pb[:, 256:512], pb[:, 512:768], pb[:, 768:1024], wc_ref[...],
                       prev_row, next_row)
        yab_ref[rows, GROUP_W:2 * GROUP_W] = yb.astype(BF16)

        sz_ref[rows, 0:256] = _silu(_dot(h, w_ref[:, CZ:CZ + 256]))
        sz_ref[rows, 256:512] = _silu(_dot(h, w_ref[:, DZ:DZ + 256]))
        cos = cos_ref[rows, :]
        sin = sin_ref[rows, :]
        qc = _dot(h, w_ref[:, CQ:CQ + 256])
        for t in range(2):
            qt = _rope(qc[:, t * LANES:(t + 1) * LANES], cos, sin, first_half) * Q_SCALE
            qc_ref[rows, t * LANES:(t + 1) * LANES] = qt.astype(BF16)
        kv = _dot(h, w_ref[:, CKV:CKV + 256])
        k = _rope(kv[:, 0:LANES], cos, sin, first_half)
        v = kv[:, LANES:2 * LANES]
        kc_ref[rows, 0:LANES] = k.astype(BF16)
        kc_ref[rows, LANES:2 * LANES] = _swap64(k).astype(BF16)
        vc_ref[rows, 0:LANES] = v.astype(BF16)
        vc_ref[rows, LANES:2 * LANES] = _swap64(v).astype(BF16)
        qd_ref[rows, :] = (_dot(h, w_ref[:, DQ:DQ + 256]) * Q_SCALE).astype(BF16)
        kd_ref[rows, :] = _dot(h, w_ref[:, DK:DK + 256]).astype(BF16)
        vd_ref[rows, :] = _dot(h, w_ref[:, DV:DV + 256]).astype(BF16)

    hs = [normalised(x_ref[0:TILE, :])]

    def normalise_next(u):
        if u + 1 < PROJ_TILES:
            hs.append(normalised(x_ref[(u + 1) * TILE:(u + 2) * TILE, :]))

    for u in range(PROJ_TILES):
        project(u, hs[u], functools.partial(normalise_next, u))


def _lat_proj(layer, x, mod, norm_g, w_in, cos_t, sin_t, ws_cat, bias_a, w_conv):
    step_rows = PROJ_TILES * TILE
    tile = lambda w: pl.BlockSpec((None, step_rows, w), lambda b, i: (b, i, 0))
    per_layer = lambda *shape: pl.BlockSpec((None,) + shape, lambda b, i: (layer,) + (0,) * len(shape))
    sds = lambda w, dt: jax.ShapeDtypeStruct((DEC_BATCH, DEC_SEQ, w), dt)
    edge = step_rows // 8
    last = DEC_SEQ // 8 - 1
    return pl.pallas_call(
        _lat_proj_kernel,
        grid=(DEC_BATCH, N_TILES // PROJ_TILES),
        in_specs=[
            tile(D_MODEL),
            pl.BlockSpec((None, 8, D_MODEL), lambda b, i: (b, jnp.maximum(i * edge - 1, 0), 0)),
            pl.BlockSpec((None, 8, D_MODEL), lambda b, i: (b, jnp.minimum((i + 1) * edge, last), 0)),
            per_layer(3, N_GROUPS, D_MODEL),
            per_layer(1, D_MODEL),
            per_layer(D_MODEL, D_IN),
            pl.BlockSpec((step_rows, LANES), lambda b, i: (i, 0)),
            pl.BlockSpec((step_rows, LANES), lambda b, i: (i, 0)),
            per_layer(CHUNK, A_HEADS * CHUNK),
            per_layer(CHUNK, GROUP_W),
            per_layer(3, GROUP_W),
        ],
        out_specs=[tile(512), tile(512)] + [tile(256)] * 6,
        out_shape=[sds(512, BF16), sds(512, F32)] + [sds(256, BF16)] * 6,
        compiler_params=pltpu.CompilerParams(dimension_semantics=("arbitrary", "arbitrary"),
                                             vmem_limit_bytes=VMEM_LIMIT),
        name="lat_proj",
    )(x, x, x, mod, norm_g, w_in, cos_t, sin_t, ws_cat, bias_a, w_conv)


_C_GROUPS = (((0, 0), (1, 1)), ((0, 1), (1, 0)))
_D_GROUPS = (((0, 0), (0, 1)), ((1, 0), (1, 1)))


def _stacked_queries(q_ref, rows, low, groups):
    return [jnp.concatenate([_mask_half(q_ref[rows, t * LANES:(t + 1) * LANES], low, half)
                             for t, half in group], axis=0) for group in groups]


def _lat_mix_kernel(layer, sink_ref, x_ref, mod_ref, fg_ref, yab_ref, sz_ref,
                    qc_ref, kc_ref, kcp_ref, kcn_ref, vc_ref, vcp_ref, vcn_ref,
                    qd_ref, kd_ref, kdp_ref, kdn_ref, vd_ref, vdp_ref, vdn_ref,
                    cck_ref, ccv_ref, cdk_ref, cdv_ref, cmask_ref, tab_ref, w_out_ref,
                    xo_ref,
                    kce, vce, kde, vde, yc_scr, yd_scr, m_ctx, acc_ctx, *stage_bufs):
    b = pl.program_id(0)
    i = pl.program_id(1)
    s_c, s_d = stage_bufs[0:2], stage_bufs[2:4]

    def file_context(m, acc, group, step_rows, n_steps):
        for which in range(2):
            for k in range(n_steps):
                src = slice(which * TILE + k * step_rows, which * TILE + (k + 1) * step_rows)
                dst = slice((4 * k + 2 * group + which) * step_rows, (4 * k + 2 * group + which + 1) * step_rows)
                m_ctx[dst, :] = m[src]
                acc_ctx[dst, :] = acc[src]

    kce[0:QBLK, :] = kcp_ref[...]
    kce[QBLK:QBLK + TILE, :] = kc_ref[...]
    kce[QBLK + TILE:, :] = kcn_ref[...]
    for c_src, c_dst in ((0, 0), (LANES, 2 * LANES)):
        vce[0:QBLK, c_dst:c_dst + LANES] = vcp_ref[:, c_src:c_src + LANES]
        vce[QBLK:QBLK + TILE, c_dst:c_dst + LANES] = vc_ref[:, c_src:c_src + LANES]
        vce[QBLK + TILE:, c_dst:c_dst + LANES] = vcn_ref[:, c_src:c_src + LANES]
    vce[:, LANES:2 * LANES] = jnp.ones((TILE + 2 * QBLK, LANES), BF16)
    swap_rows = lambda x: jnp.concatenate([x[HEAD_DIM:], x[:HEAD_DIM]], axis=0)
    cck = cck_ref[...].astype(BF16)
    ccv = ccv_ref[...].T
    ones_ctx = jnp.ones((PAST_LEN, LANES), BF16)
    k_ctx = (cck, swap_rows(cck))
    v_ctx = jnp.concatenate([ccv.astype(BF16), ones_ctx, _swap64(ccv).astype(BF16)], axis=1)
    low_q = _low_half(QBLK)
    low_t = _low_half(TILE)
    n_blocks = DEC_SEQ // QBLK
    blocks_per_tile = TILE // QBLK

    for sw, q in enumerate(_stacked_queries(qc_ref, slice(None), low_t, _C_GROUPS)):
        m, acc = _context_pass(q, k_ctx[sw], v_ctx[:, sw * LANES:(sw + 2) * LANES])
        file_context(m, acc, sw, QBLK, blocks_per_tile)

    def c_scores(j):
        n = i * blocks_per_tile + j
        window = cmask_ref[jnp.where(n == 0, 1, jnp.where(n == n_blocks - 1, 2, 0))]
        window = jnp.concatenate([window, window], axis=0)
        rows = slice(j * QBLK, (j + 1) * QBLK)
        for sw, q in enumerate(_stacked_queries(qc_ref, rows, low_q, _C_GROUPS)):
            k_loc = kce[j * QBLK:(j + 3) * QBLK, sw * LANES:(sw + 1) * LANES]
            s_c[j % 2][2 * sw * QBLK:2 * (sw + 1) * QBLK, :] = _dot_nt(q, k_loc) + window

    def c_softmax(j):
        step = slice(4 * j * QBLK, 4 * (j + 1) * QBLK)
        sink = jnp.concatenate([jnp.full((QBLK, 1), sink_ref[layer, 2 * t + half] * LOG2E, F32)
                                for group in _C_GROUPS for t, half in group], axis=0)
        e, ctx_scale, sink_term = _local_softmax(s_c[j % 2][...], m_ctx[step, :], sink)
        values = [(vce[j * QBLK:(j + 3) * QBLK, sw * LANES:(sw + 2) * LANES], sw) for sw in range(2)]
        r0, r1 = _local_output(e, values, acc_ctx[step, :], ctx_scale, sink_term)
        yc_scr[j * QBLK:(j + 1) * QBLK, :] = jnp.concatenate(
            [jnp.where(low_q, r0[0:QBLK], r1[0:QBLK]), jnp.where(low_q, r1[QBLK:], r0[QBLK:])], axis=1)

    _skewed(blocks_per_tile, (c_scores, c_softmax))

    halo = 4 * GRID_W
    kde[0:halo, :] = kdp_ref[...]
    kde[halo:halo + TILE, :] = kd_ref[...]
    kde[halo + TILE:, :] = kdn_ref[...]
    for c_src, c_dst in ((0, 0), (LANES, 2 * LANES)):
        vde[0:halo, c_dst:c_dst + LANES] = vdp_ref[:, c_src:c_src + LANES]
        vde[halo:halo + TILE, c_dst:c_dst + LANES] = vd_ref[:, c_src:c_src + LANES]
        vde[halo + TILE:, c_dst:c_dst + LANES] = vdn_ref[:, c_src:c_src + LANES]
    vde[:, LANES:2 * LANES] = jnp.ones((TILE + 2 * halo, LANES), BF16)
    kd_ctx = cdk_ref[...].astype(BF16)
    cdv = cdv_ref[...].T.astype(BF16)
    vd_ctx = jnp.concatenate([cdv[:, 0:LANES], ones_ctx, cdv[:, LANES:2 * LANES]], axis=1)
    low_r = _low_half(GRID_W)
    n_rows = DEC_SEQ // GRID_W

    for t, q in enumerate(_stacked_queries(qd_ref, slice(None), low_t, _D_GROUPS)):
        m, acc = _context_pass(q, kd_ctx[t * LANES:(t + 1) * LANES], vd_ctx[:, t * LANES:(t + 2) * LANES])
        file_context(m, acc, t, GRID_W, ROWS_PER_TILE)

    def window_start(rl):
        r = i * ROWS_PER_TILE + rl
        rs = jnp.clip(r - WIN_R // 2, 0, n_rows - WIN_R)
        return rs - r + (WIN_R - 1), pl.multiple_of((rs - i * ROWS_PER_TILE + 4) * GRID_W, GRID_W)

    def d_scores(rl):
        didx, off = window_start(rl)
        rows = slice(rl * GRID_W, (rl + 1) * GRID_W)
        for t, q in enumerate(_stacked_queries(qd_ref, rows, low_r, _D_GROUPS)):
            k_loc = kde[pl.ds(off, WIN_R * GRID_W), t * LANES:(t + 1) * LANES]
            bias = jnp.concatenate(
                [jnp.concatenate([tab_ref[h, didx + 2 * p] for p in range(WIN_R // 2)], axis=1)
                 for h in (2 * t, 2 * t + 1)], axis=0)
            s_d[rl % 2][2 * t * GRID_W:2 * (t + 1) * GRID_W, :] = _dot_nt(q, k_loc) + bias

    def d_softmax(rl):
        _, off = window_start(rl)
        step = slice(4 * rl * GRID_W, 4 * (rl + 1) * GRID_W)
        e, ctx_scale, _ = _local_softmax(s_d[rl % 2][...], m_ctx[step, :])
        values = [(vde[pl.ds(off, WIN_R * GRID_W), t * LANES:(t + 2) * LANES], t) for t in range(2)]
        r = _local_output(e, values, acc_ctx[step, :], ctx_scale)
        yd_scr[rl * GRID_W:(rl + 1) * GRID_W, :] = jnp.concatenate(
            [jnp.where(low_r, r[t][0:GRID_W], r[t][GRID_W:]) for t in range(2)], axis=1)

    _skewed(ROWS_PER_TILE, (d_scores, d_softmax))

    yc = (yc_scr[...] * sz_ref[:, 0:256]).astype(BF16)
    yd = (yd_scr[...] * sz_ref[:, 256:512]).astype(BF16)
    y = (_dot(yab_ref[...], w_out_ref[0:2 * GROUP_W, :]) + _dot(yc, w_out_ref[2 * GROUP_W:3 * GROUP_W, :])
         + _dot(yd, w_out_ref[3 * GROUP_W:4 * GROUP_W, :]))
    x_new = x_ref[...] + mod_ref[2, pl.ds(1 + b, 1), :] * y
    if layer == DEPTH - 1:
        x_new = _rms(x_new) * fg_ref[...]
    xo_ref[...] = x_new


def _lat_mix(layer, x, mod, final_g, yab, sz, qc, kc, vc, qd, kd, vd, cck, ccv, cdk, cdv, tab, w_out, sink):
    tile = lambda w: pl.BlockSpec((None, TILE, w), lambda b, i: (b, i, 0))
    per_layer = lambda *shape: pl.BlockSpec((None,) + shape, lambda b, i: (layer,) + (0,) * len(shape))

    def halo(rows, w):
        per = TILE // rows
        last = DEC_SEQ // rows - 1
        prev = pl.BlockSpec((None, rows, w), lambda b, i: (b, jnp.maximum(i * per - 1, 0), 0))
        nxt = pl.BlockSpec((None, rows, w), lambda b, i: (b, jnp.minimum((i + 1) * per, last), 0))
        return prev, nxt

    cprev, cnext = halo(QBLK, 256)
    dprev, dnext = halo(4 * GRID_W, 256)
    cache = lambda w: pl.BlockSpec((None, None, w, PAST_LEN), lambda b, i: (b, layer, 0, 0))
    return pl.pallas_call(
        functools.partial(_lat_mix_kernel, layer),
        grid=(DEC_BATCH, N_TILES),
        in_specs=[
            pl.BlockSpec(memory_space=pltpu.SMEM),
            tile(D_MODEL),
            per_layer(3, N_GROUPS, D_MODEL),
            pl.BlockSpec((1, D_MODEL), lambda b, i: (0, 0)),
            tile(512), tile(512),
            tile(256), tile(256), cprev, cnext, tile(256), cprev, cnext,
            tile(256), tile(256), dprev, dnext, tile(256), dprev, dnext,
            cache(128), cache(128), cache(256), cache(256),
            pl.BlockSpec((3, QBLK, 3 * QBLK), lambda b, i: (0, 0, 0)),
            per_layer(4, N_BIAS_TILES, GRID_W, LANES),
            per_layer(D_MODEL, D_MODEL),
        ],
        out_specs=tile(D_MODEL),
        out_shape=jax.ShapeDtypeStruct((DEC_BATCH, DEC_SEQ, D_MODEL), F32),
        scratch_shapes=[
            pltpu.VMEM((TILE + 2 * QBLK, 2 * LANES), BF16),
            pltpu.VMEM((TILE + 2 * QBLK, 3 * LANES), BF16),
            pltpu.VMEM((TILE + 8 * GRID_W, 2 * LANES), BF16),
            pltpu.VMEM((TILE + 8 * GRID_W, 3 * LANES), BF16),
            pltpu.VMEM((TILE, 256), F32),
            pltpu.VMEM((TILE, 256), F32),
            pltpu.VMEM((4 * TILE, 1), F32),
            pltpu.VMEM((4 * TILE, 2 * LANES), F32),
        ] + 2 * [pltpu.VMEM((4 * QBLK, 3 * QBLK), F32)] + 2 * [pltpu.VMEM((4 * GRID_W, WIN_R * GRID_W), F32)],
        compiler_params=pltpu.CompilerParams(dimension_semantics=("arbitrary", "arbitrary"),
                                             vmem_limit_bytes=VMEM_LIMIT),
        name="lat_mix",
    )(sink, x, mod, final_g, yab, sz, qc, kc, kc, kc, vc, vc, vc,
      qd, kd, kd, kd, vd, vd, vd, cck, ccv, cdk, cdv, _window_masks(), tab, w_out)


def _window_masks():
    p = np.arange(QBLK)[:, None]
    j = np.arange(3 * QBLK)[None, :]
    band = np.abs(j - QBLK - p) <= WINDOW
    masks = [band, band & (j >= QBLK), band & (j < 2 * QBLK)]
    return jnp.asarray(np.where(np.stack(masks), 0.0, NEG), F32)


def _rope_tables():
    t = np.arange(DEC_SEQ)
    freqs = (np.float32(ROPE_BASE) ** (-np.arange(16, dtype=np.float32) / np.float32(16))).astype(np.float32)
    ang_r = (t // GRID_W).astype(np.float32)[:, None] * freqs
    ang_c = (t % GRID_W).astype(np.float32)[:, None] * freqs
    cos_h = np.concatenate([np.cos(ang_r), np.cos(ang_r), np.cos(ang_c), np.cos(ang_c)], axis=1)
    sin_h = np.concatenate([-np.sin(ang_r), np.sin(ang_r), -np.sin(ang_c), np.sin(ang_c)], axis=1)
    return (jnp.asarray(np.tile(cos_h, (1, 2)), F32), jnp.asarray(np.tile(sin_h, (1, 2)), F32))


def kernel(x_prompt, x_sample, cache_c_k, cache_c_v, cache_d_k, cache_d_v, c, c_ctx, norm_g, w_mod, b_mod,
           w_in, w_out, w_s, b_s, w_conv, sink, rpb, final_g):
    w_in_b = w_in.astype(BF16)
    w_out_b = w_out.astype(BF16)
    ws_cat = jnp.transpose(w_s, (0, 2, 1, 3)).reshape(DEPTH, CHUNK, A_HEADS * CHUNK).astype(BF16)
    bias_a = jnp.repeat(jnp.transpose(b_s, (0, 2, 1)), HEAD_DIM, axis=2)
    norm_g3 = norm_g.reshape(DEPTH, 1, D_MODEL)
    fg = final_g.reshape(1, D_MODEL)
    cos_t, sin_t = _rope_tables()
    feature_major = lambda cache, w: jnp.transpose(cache, (0, 1, 3, 4, 2)).reshape(DEC_BATCH, DEPTH, w, PAST_LEN)
    cck = feature_major(cache_c_k, 128)
    ccv = feature_major(cache_c_v, 128)
    cdk = feature_major(cache_d_k, 256)
    cdv = feature_major(cache_d_v, 256)

    cond = jnp.concatenate([c_ctx[None, :], c, jnp.zeros((N_GROUPS - 1 - DEC_BATCH, D_MODEL), F32)], axis=0)
    mod = _modulation(cond, w_mod, b_mod)
    tab = _nbr_tables(rpb)

    y_prompt, s_ck, s_cv, s_dk, s_dv = _ctx_layers(x_prompt, mod, norm_g3, fg, w_in_b, w_out_b, ws_cat,
                                                   bias_a, w_conv, sink)
    xs = x_sample
    for l in range(DEPTH):
        yab, sz, qc, kc, vc, qd, kd, vd = _lat_proj(l, xs, mod, norm_g3, w_in_b, cos_t, sin_t,
                                                    ws_cat, bias_a, w_conv)
        xs = _lat_mix(l, xs, mod, fg, yab, sz, qc, kc, vc, qd, kd, vd, cck, ccv, cdk, cdv, tab, w_out_b, sink)

    shape_c = (BATCH, DEPTH, SEQ, 2, HEAD_DIM)
    shape_d = (BATCH, DEPTH, SEQ, 4, HEAD_DIM)
    return (y_prompt, xs, s_ck.reshape(shape_c), s_cv.reshape(shape_c), s_dk.reshape(shape_d),
            s_dv.reshape(shape_d))
```

```python
import functools

import numpy as np
import jax
import jax.numpy as jnp
from jax import lax
from jax.experimental import pallas as pl
from jax.experimental.pallas import tpu as pltpu

D_MODEL = 1024
BATCH = 16
SEQ = 256
DEPTH = 4
DEC_BATCH = 2
DEC_SEQ = 4096
PAST_LEN = 512
GRID_W = 64
GROUP_W = 256
HEAD_DIM = 64
A_HEADS = 4
CHUNK = 128
WINDOW = 128
QBLK = 128
WIN_R = 8
WIN_C = 16
ROPE_BASE = 10000.0
EPS = 1e-6
NEG = -1e30
D_IN = 3584
LOG2E = 1.4426950408889634
Q_SCALE = HEAD_DIM ** -0.5 * LOG2E

AU, AV, AZ, BB, BC, BH, BZ = 0, 256, 512, 768, 1024, 1280, 1536
CQ, CKV, CZ, DQ, DK, DV, DZ = 1792, 2048, 2304, 2560, 2816, 3072, 3328

LANES = 128
TILE = 512
ROWS_PER_TILE = TILE // GRID_W
N_TILES = DEC_SEQ // TILE
CTX_SEQS = 4
PROJ_TILES = 2
N_GROUPS = 8
VMEM_LIMIT = 56 * 1024 * 1024

F32 = jnp.float32
BF16 = jnp.bfloat16


def _silu(z):
    return z * (1.0 / (1.0 + jnp.exp(-z)))


def _dot(a, b):
    return jnp.dot(a, b, preferred_element_type=F32)


def _dot_nt(a, b):
    return lax.dot_general(a, b, (((1,), (1,)), ((), ())), preferred_element_type=F32)


def _rms(x):
    return x * lax.rsqrt(jnp.mean(x * x, axis=-1, keepdims=True) + EPS)


def _norm_mod(x, g, shift, scale):
    return _rms(x) * (g * (1.0 + scale)) + shift


def _swap64(x):
    return pltpu.roll(x, HEAD_DIM, axis=1)


def _group_mean_matrix():
    r = lax.broadcasted_iota(jnp.int32, (GROUP_W, GROUP_W), 0) // HEAD_DIM
    c = lax.broadcasted_iota(jnp.int32, (GROUP_W, GROUP_W), 1) // HEAD_DIM
    return jnp.where(r == c, 1.0 / HEAD_DIM, 0.0).astype(BF16)


def _branch_a(au, av, az, ws, bias):
    t = av.shape[0]
    sq = av * av
    hi = sq.astype(BF16)
    lo = (sq - hi.astype(F32)).astype(BF16)
    gm = _group_mean_matrix()
    ms = _dot(hi, gm) + _dot(lo, gm)
    vh = (av * lax.rsqrt(ms + EPS)).astype(BF16)
    head = lax.broadcasted_iota(jnp.int32, (CHUNK, GROUP_W), 1) // HEAD_DIM
    outs = []
    for n in range(t // CHUNK):
        v = vh[n * CHUNK:(n + 1) * CHUNK]
        rhs = jnp.concatenate([jnp.where(head == h, v, jnp.zeros_like(v)) for h in range(A_HEADS)], axis=0)
        outs.append(_dot(ws, rhs) + bias)
    mixed = jnp.concatenate(outs, axis=0)
    return au * mixed * _silu(az)


def _branch_b(bb, bc, bh, bz, wc, prev_row, next_row):
    t = bb.shape[0]
    xc = bc * bh
    row = lax.broadcasted_iota(jnp.int32, xc.shape, 0)
    xm = jnp.where(row == 0, prev_row, pltpu.roll(xc, 1, axis=0))
    xp = jnp.where(row == t - 1, next_row, pltpu.roll(xc, t - 1, axis=0))
    y = wc[0:1, :] * xm + wc[1:2, :] * xc + wc[2:3, :] * xp
    return bb * y * _silu(bz)


def _attend(score, value, pos, sink=None):
    m = jnp.max(score, axis=-1, keepdims=True)
    if sink is not None:
        m = jnp.maximum(m, sink * LOG2E)
    acc = _dot(jnp.exp2(score - m).astype(BF16), value)
    o = acc[:, pos * LANES:(pos + 1) * LANES]
    l = acc[:, (1 - pos) * LANES:(2 - pos) * LANES]
    if sink is not None:
        l = l + jnp.exp2(sink * LOG2E - m)
    return o * (1.0 / l)


def _context_pass(q_stacked, k_t, value):
    s = _dot(q_stacked, k_t)
    m = jnp.max(s, axis=-1, keepdims=True)
    return m, _dot(jnp.exp2(s - m).astype(BF16), value)


def _local_softmax(s_loc, m_ctx, sink=None):
    m = jnp.maximum(jnp.max(s_loc, axis=-1, keepdims=True), m_ctx)
    if sink is not None:
        m = jnp.maximum(m, sink)
    e = jnp.exp2(s_loc - m).astype(BF16)
    return e, jnp.exp2(m_ctx - m), (None if sink is None else jnp.exp2(sink - m))


def _local_output(e, values, acc_ctx, ctx_scale, sink_term=None):
    rows = e.shape[0] // len(values)
    outs = []
    for p, (v, pos) in enumerate(values):
        r = slice(p * rows, (p + 1) * rows)
        acc = _dot(e[r], v) + acc_ctx[r] * ctx_scale[r]
        o = acc[:, pos * LANES:(pos + 1) * LANES]
        l = acc[:, (1 - pos) * LANES:(2 - pos) * LANES]
        if sink_term is not None:
            l = l + sink_term[r]
        outs.append(o * (1.0 / l))
    return outs


def _skewed(n_steps, stages):
    for tick in range(n_steps + len(stages) - 1):
        for lag, stage in enumerate(stages):
            if 0 <= tick - lag < n_steps:
                stage(tick - lag)


def _low_half(m):
    return lax.broadcasted_iota(jnp.int32, (m, LANES), 1) < HEAD_DIM


def _mask_half(q, low, half):
    z = jnp.zeros_like(q)
    return jnp.where(low, q, z) if half == 0 else jnp.where(low, z, q)


def _out_proj(x, gate, ya, yb, yc, yd, w_out_ref):
    y = _dot(ya.astype(BF16), w_out_ref[0:256, :])
    y += _dot(yb.astype(BF16), w_out_ref[256:512, :])
    y += _dot(yc.astype(BF16), w_out_ref[512:768, :])
    y += _dot(yd.astype(BF16), w_out_ref[768:1024, :])
    return x + gate * y


N_BIAS_TILES = 2 * WIN_R - 2


def _mod_kernel(cond_ref, w_ref, b_ref, rpb_ref, o_ref, tab_ref):
    mod = _dot(_silu(cond_ref[...]).astype(BF16), w_ref[...].astype(BF16)) + b_ref[...]
    for j in range(3):
        o_ref[j] = mod[:, j * D_MODEL:(j + 1) * D_MODEL]
    _nbr_table(rpb_ref, tab_ref)


def _modulation_and_tables(cond, w_mod, b_mod, rpb):
    half = LANES // 2 - (2 * WIN_C - 1)
    rows = jnp.pad(rpb, ((0, 0), (0, 0), (0, 1), (0, half)))
    rpb_p = jnp.concatenate([rows[:, :, :-1], rows[:, :, 1:]], axis=-1)
    rpb_p = jnp.pad(rpb_p, ((0, 0), (0, 0), (0, 1), (0, 0)))
    return pl.pallas_call(
        _mod_kernel,
        grid=(DEPTH,),
        in_specs=[
            pl.BlockSpec((N_GROUPS, D_MODEL), lambda l: (0, 0)),
            pl.BlockSpec((None, D_MODEL, 3 * D_MODEL), lambda l: (l, 0, 0)),
            pl.BlockSpec((None, 1, 3 * D_MODEL), lambda l: (l, 0, 0)),
            pl.BlockSpec((None, 4, 2 * WIN_R, LANES), lambda l: (l, 0, 0, 0)),
        ],
        out_specs=[pl.BlockSpec((None, 3, N_GROUPS, D_MODEL), lambda l: (l, 0, 0, 0)),
                   pl.BlockSpec((None, 4, N_BIAS_TILES, GRID_W, LANES), lambda l: (l, 0, 0, 0, 0))],
        out_shape=[jax.ShapeDtypeStruct((DEPTH, 3, N_GROUPS, D_MODEL), F32),
                   jax.ShapeDtypeStruct((DEPTH, 4, N_BIAS_TILES, GRID_W, LANES), F32)],
        compiler_params=pltpu.CompilerParams(dimension_semantics=("arbitrary",),
                                             vmem_limit_bytes=VMEM_LIMIT),
        name="modulation",
    )(cond, w_mod, b_mod.reshape(DEPTH, 1, 3 * D_MODEL), rpb_p)


def _nbr_table(rpb_ref, o_ref):
    c = lax.broadcasted_iota(jnp.int32, (GRID_W, LANES), 0)
    lane = lax.broadcasted_iota(jnp.int32, (GRID_W, LANES), 1)
    cc = lane % GRID_W
    cstart = jnp.clip(c - WIN_C // 2, 0, GRID_W - WIN_C)
    for h in range(4):
        for ro in range(N_BIAS_TILES):
            v = pltpu.roll(jnp.broadcast_to(rpb_ref[h, ro:ro + 1, :] * LOG2E, (GRID_W, LANES)),
                           LANES - (WIN_C - 1), 1, stride=1, stride_axis=0)
            o_ref[h, ro] = jnp.where(cc >= cstart, jnp.where(cc < cstart + WIN_C, v, NEG), NEG)


def _ctx_kernel(sink_ref, x_ref, mod_ref, g_ref, fg_ref, w_in_ref, w_out_ref, ws_ref, ba_ref, wc_ref,
                y_ref, ck_ref, cv_ref, dk_ref, dv_ref, xs):
    l = pl.program_id(0)
    pair = pl.program_id(1)

    @pl.when(l == 0)
    def _():
        for u in range(CTX_SEQS):
            xs[CTX_SEQS * pair + u] = x_ref[u]

    low = _low_half(SEQ)
    ones = jnp.ones((SEQ, LANES), BF16)

    def project(u):
        x = xs[CTX_SEQS * pair + u]
        h = _norm_mod(x, g_ref[...], mod_ref[0, 0:1, :], mod_ref[1, 0:1, :]).astype(BF16)
        return x, _dot(h, w_in_ref[...])

    def mix(u, p):
        kcf = p[:, CKV:CKV + 128]
        vcf = p[:, CKV + 128:CKV + 256]
        kdf = p[:, DK:DK + 256]
        vdf = p[:, DV:DV + 256]
        ck_ref[u] = kcf
        cv_ref[u] = vcf
        dk_ref[u] = kdf
        dv_ref[u] = vdf

        ya = _branch_a(p[:, AU:AU + 256], p[:, AV:AV + 256], p[:, AZ:AZ + 256], ws_ref[...], ba_ref[...])
        zero_row = jnp.zeros((1, GROUP_W), F32)
        yb = _branch_b(p[:, BB:BB + 256], p[:, BC:BC + 256], p[:, BH:BH + 256], p[:, BZ:BZ + 256],
                       wc_ref[...], zero_row, zero_row)

        kc = (kcf.astype(BF16), _swap64(kcf).astype(BF16))
        vc = (jnp.concatenate([vcf.astype(BF16), ones], axis=1),
              jnp.concatenate([ones, _swap64(vcf).astype(BF16)], axis=1))
        o = []
        for t in range(2):
            q = (p[:, CQ + t * LANES:CQ + (t + 1) * LANES] * Q_SCALE).astype(BF16)
            for half in range(2):
                sw = (t + half) % 2
                o.append(_attend(_dot_nt(_mask_half(q, low, half), kc[sw]), vc[sw], sw,
                                 sink_ref[l, 2 * t + half]))
        yc = jnp.concatenate([jnp.where(low, o[0], o[1]), jnp.where(low, o[2], o[3])], axis=1)
        yc = yc * _silu(p[:, CZ:CZ + 256])

        o = []
        for t in range(2):
            q = (p[:, DQ + t * LANES:DQ + (t + 1) * LANES] * Q_SCALE).astype(BF16)
            k = kdf[:, t * LANES:(t + 1) * LANES].astype(BF16)
            v = jnp.concatenate([vdf[:, t * LANES:(t + 1) * LANES].astype(BF16), ones], axis=1)
            for half in range(2):
                o.append(_attend(_dot_nt(_mask_half(q, low, half), k), v, 0))
        yd = jnp.concatenate([jnp.where(low, o[0], o[1]), jnp.where(low, o[2], o[3])], axis=1)
        yd = yd * _silu(p[:, DZ:DZ + 256])
        return ya, yb, yc, yd

    def finish(u, x, ys):
        x_new = _out_proj(x, mod_ref[2, 0:1, :], *ys, w_out_ref)
        xs[CTX_SEQS * pair + u] = x_new
        return x_new

    staged = [project(0)]
    x_new = []
    for u in range(CTX_SEQS):
        if u + 1 < CTX_SEQS:
            staged.append(project(u + 1))
        x, p = staged[u]
        x_new.append(finish(u, x, mix(u, p)))

    @pl.when(l == DEPTH - 1)
    def _():
        for u in range(CTX_SEQS):
            y_ref[u] = _rms(x_new[u]) * fg_ref[...]


def _ctx_layers(x, mod, norm_g, final_g, w_in, w_out, ws_cat, bias_a, w_conv, sink):
    per_layer = lambda *shape: pl.BlockSpec((None,) + shape, lambda l, b: (l,) + (0,) * len(shape))
    state = lambda w: pl.BlockSpec((CTX_SEQS, None, SEQ, w), lambda l, b: (b, l, 0, 0))
    n_steps = BATCH // CTX_SEQS
    return pl.pallas_call(
        _ctx_kernel,
        grid=(DEPTH, n_steps),
        in_specs=[
            pl.BlockSpec(memory_space=pltpu.SMEM),
            pl.BlockSpec((CTX_SEQS, SEQ, D_MODEL), lambda l, b: (jnp.where(l == 0, b, n_steps - 1), 0, 0)),
            per_layer(3, N_GROUPS, D_MODEL),
            per_layer(1, D_MODEL),
            pl.BlockSpec((1, D_MODEL), lambda l, b: (0, 0)),
            pl.BlockSpec((None, D_MODEL, D_IN), lambda l, b: (l, 0, 0), pipeline_mode=pl.Buffered(1)),
            pl.BlockSpec((None, D_MODEL, D_MODEL), lambda l, b: (l, 0, 0), pipeline_mode=pl.Buffered(1)),
            per_layer(CHUNK, A_HEADS * CHUNK),
            per_layer(CHUNK, GROUP_W),
            per_layer(3, GROUP_W),
        ],
        out_specs=[
            pl.BlockSpec((CTX_SEQS, SEQ, D_MODEL), lambda l, b: (jnp.where(l == DEPTH - 1, b, 0), 0, 0)),
            state(128), state(128), state(256), state(256),
        ],
        out_shape=[
            jax.ShapeDtypeStruct((BATCH, SEQ, D_MODEL), F32),
            jax.ShapeDtypeStruct((BATCH, DEPTH, SEQ, 128), F32),
            jax.ShapeDtypeStruct((BATCH, DEPTH, SEQ, 128), F32),
            jax.ShapeDtypeStruct((BATCH, DEPTH, SEQ, 256), F32),
            jax.ShapeDtypeStruct((BATCH, DEPTH, SEQ, 256), F32),
        ],
        scratch_shapes=[pltpu.VMEM((BATCH, SEQ, D_MODEL), F32)],
        compiler_params=pltpu.CompilerParams(dimension_semantics=("arbitrary", "arbitrary"),
                                             vmem_limit_bytes=VMEM_LIMIT),
        name="ctx_layers",
    )(sink, x, mod, norm_g, final_g, w_in, w_out, ws_cat, bias_a, w_conv)


def _rope(x, cos, sin_signed, first_half):
    swapped = jnp.where(first_half, pltpu.roll(x, LANES - 16, axis=1), pltpu.roll(x, 16, axis=1))
    return x * cos + swapped * sin_signed


def _lat_proj_kernel(x_ref, xp_ref, xn_ref, mod_ref, g_ref, w_ref, cos_ref, sin_ref, ws_ref, ba_ref, wc_ref,
                     yab_ref, sz_ref, qc_ref, kc_ref, vc_ref, qd_ref, kd_ref, vd_ref):
    i = pl.program_id(1)
    grp = pl.ds(1 + pl.program_id(0), 1)
    g = g_ref[...]
    shift = mod_ref[0, grp, :]
    scale = mod_ref[1, grp, :]
    first_half = (lax.broadcasted_iota(jnp.int32, (TILE, LANES), 1) % 32) < 16

    def normalised(x):
        return _norm_mod(x, g, shift, scale).astype(BF16)

    def project(u, h, after_first_dot):
        rows = slice(u * TILE, (u + 1) * TILE)
        tile_index = PROJ_TILES * i + u
        pa = _dot(h, w_ref[:, AU:AU + 3 * GROUP_W])
        after_first_dot()
        ya = _branch_a(pa[:, 0:256], pa[:, 256:512], pa[:, 512:768], ws_ref[...], ba_ref[...])
        yab_ref[rows, 0:GROUP_W] = ya.astype(BF16)

        pb = _dot(h, w_ref[:, BB:BB + 4 * GROUP_W])
        before = xp_ref[...] if u == 0 else x_ref[u * TILE - 8:u * TILE, :]
        after = xn_ref[...] if u == PROJ_TILES - 1 else x_ref[(u + 1) * TILE:(u + 1) * TILE + 8, :]
        e = _dot(normalised(jnp.concatenate([before, after], axis=0)), w_ref[:, BC:BC + 2 * GROUP_W])
        prev_row = jnp.where(tile_index > 0, e[7:8, 0:256] * e[7:8, 256:512], 0.0)
        next_row = jnp.where(tile_index < N_TILES - 1, e[8:9, 0:256] * e[8:9, 256:512], 0.0)
        yb = _branch_b(pb[:, 0:256], pb[:, 256:512], pb[:, 512:768], pb[:, 768:1024], wc_ref[...],
                       prev_row, next_row)
        yab_ref[rows, GROUP_W:2 * GROUP_W] = yb.astype(BF16)

        sz_ref[rows, 0:256] = _silu(_dot(h, w_ref[:, CZ:CZ + 256]))
        sz_ref[rows, 256:512] = _silu(_dot(h, w_ref[:, DZ:DZ + 256]))
        cos = cos_ref[rows, :]
        sin = sin_ref[rows, :]
        qc = _dot(h, w_ref[:, CQ:CQ + 256])
        for t in range(2):
            qt = _rope(qc[:, t * LANES:(t + 1) * LANES], cos, sin, first_half) * Q_SCALE
            qc_ref[rows, t * LANES:(t + 1) * LANES] = qt.astype(BF16)
        kv = _dot(h, w_ref[:, CKV:CKV + 256])
        k = _rope(kv[:, 0:LANES], cos, sin, first_half)
        v = kv[:, LANES:2 * LANES]
        kc_ref[rows, 0:LANES] = k.astype(BF16)
        kc_ref[rows, LANES:2 * LANES] = _swap64(k).astype(BF16)
        vc_ref[rows, 0:LANES] = v.astype(BF16)
        vc_ref[rows, LANES:2 * LANES] = _swap64(v).astype(BF16)
        qd_ref[rows, :] = (_dot(h, w_ref[:, DQ:DQ + 256]) * Q_SCALE).astype(BF16)
        kd_ref[rows, :] = _dot(h, w_ref[:, DK:DK + 256]).astype(BF16)
        vd_ref[rows, :] = _dot(h, w_ref[:, DV:DV + 256]).astype(BF16)

    hs = [normalised(x_ref[0:TILE, :])]

    def normalise_next(u):
        if u + 1 < PROJ_TILES:
            hs.append(normalised(x_ref[(u + 1) * TILE:(u + 2) * TILE, :]))

    for u in range(PROJ_TILES):
        project(u, hs[u], functools.partial(normalise_next, u))


def _lat_proj(layer, x, mod, norm_g, w_in, cos_t, sin_t, ws_cat, bias_a, w_conv):
    step_rows = PROJ_TILES * TILE
    tile = lambda w: pl.BlockSpec((None, step_rows, w), lambda b, i: (b, i, 0))
    per_layer = lambda *shape: pl.BlockSpec((None,) + shape, lambda b, i: (layer,) + (0,) * len(shape))
    sds = lambda w, dt: jax.ShapeDtypeStruct((DEC_BATCH, DEC_SEQ, w), dt)
    edge = step_rows // 8
    last = DEC_SEQ // 8 - 1
    return pl.pallas_call(
        _lat_proj_kernel,
        grid=(DEC_BATCH, N_TILES // PROJ_TILES),
        in_specs=[
            tile(D_MODEL),
            pl.BlockSpec((None, 8, D_MODEL), lambda b, i: (b, jnp.maximum(i * edge - 1, 0), 0)),
            pl.BlockSpec((None, 8, D_MODEL), lambda b, i: (b, jnp.minimum((i + 1) * edge, last), 0)),
            per_layer(3, N_GROUPS, D_MODEL),
            per_layer(1, D_MODEL),
            per_layer(D_MODEL, D_IN),
            pl.BlockSpec((step_rows, LANES), lambda b, i: (i, 0)),
            pl.BlockSpec((step_rows, LANES), lambda b, i: (i, 0)),
            per_layer(CHUNK, A_HEADS * CHUNK),
            per_layer(CHUNK, GROUP_W),
            per_layer(3, GROUP_W),
        ],
        out_specs=[tile(512), tile(512)] + [tile(256)] * 6,
        out_shape=[sds(512, BF16), sds(512, F32)] + [sds(256, BF16)] * 6,
        compiler_params=pltpu.CompilerParams(dimension_semantics=("arbitrary", "arbitrary"),
                                             vmem_limit_bytes=VMEM_LIMIT),
        name="lat_proj",
    )(x, x, x, mod, norm_g, w_in, cos_t, sin_t, ws_cat, bias_a, w_conv)


_C_GROUPS = (((0, 0), (1, 1)), ((0, 1), (1, 0)))
_D_GROUPS = (((0, 0), (0, 1)), ((1, 0), (1, 1)))


def _stacked_queries(q_ref, rows, low, groups):
    return [jnp.concatenate([_mask_half(q_ref[rows, t * LANES:(t + 1) * LANES], low, half)
                             for t, half in group], axis=0) for group in groups]


def _lat_mix_kernel(layer, sink_ref, x_ref, mod_ref, fg_ref, yab_ref, sz_ref,
                    qc_ref, kc_ref, kcp_ref, kcn_ref, vc_ref, vcp_ref, vcn_ref,
                    qd_ref, kd_ref, kdp_ref, kdn_ref, vd_ref, vdp_ref, vdn_ref,
                    cck_ref, ccv_ref, cdk_ref, cdv_ref, cmask_ref, tab_ref, w_out_ref,
                    xo_ref,
                    kce, vce, kde, vde, yc_scr, yd_scr, m_ctx, acc_ctx, *stage_bufs):
    b = pl.program_id(0)
    i = pl.program_id(1)
    s_c, s_d = stage_bufs[0:2], stage_bufs[2:4]

    def file_context(m, acc, group, step_rows, n_steps):
        for which in range(2):
            for k in range(n_steps):
                src = slice(which * TILE + k * step_rows, which * TILE + (k + 1) * step_rows)
                dst = slice((4 * k + 2 * group + which) * step_rows, (4 * k + 2 * group + which + 1) * step_rows)
                m_ctx[dst, :] = m[src]
                acc_ctx[dst, :] = acc[src]

    kce[0:QBLK, :] = kcp_ref[...]
    kce[QBLK:QBLK + TILE, :] = kc_ref[...]
    kce[QBLK + TILE:, :] = kcn_ref[...]
    for c_src, c_dst in ((0, 0), (LANES, 2 * LANES)):
        vce[0:QBLK, c_dst:c_dst + LANES] = vcp_ref[:, c_src:c_src + LANES]
        vce[QBLK:QBLK + TILE, c_dst:c_dst + LANES] = vc_ref[:, c_src:c_src + LANES]
        vce[QBLK + TILE:, c_dst:c_dst + LANES] = vcn_ref[:, c_src:c_src + LANES]
    vce[:, LANES:2 * LANES] = jnp.ones((TILE + 2 * QBLK, LANES), BF16)
    swap_rows = lambda x: jnp.concatenate([x[HEAD_DIM:], x[:HEAD_DIM]], axis=0)
    cck = cck_ref[...].astype(BF16)
    ccv = ccv_ref[...].T
    ones_ctx = jnp.ones((PAST_LEN, LANES), BF16)
    k_ctx = (cck, swap_rows(cck))
    v_ctx = jnp.concatenate([ccv.astype(BF16), ones_ctx, _swap64(ccv).astype(BF16)], axis=1)
    low_q = _low_half(QBLK)
    low_t = _low_half(TILE)
    n_blocks = DEC_SEQ // QBLK
    blocks_per_tile = TILE // QBLK

    for sw, q in enumerate(_stacked_queries(qc_ref, slice(None), low_t, _C_GROUPS)):
        m, acc = _context_pass(q, k_ctx[sw], v_ctx[:, sw * LANES:(sw + 2) * LANES])
        file_context(m, acc, sw, QBLK, blocks_per_tile)

    def c_scores(j):
        n = i * blocks_per_tile + j
        window = cmask_ref[jnp.where(n == 0, 1, jnp.where(n == n_blocks - 1, 2, 0))]
        window = jnp.concatenate([window, window], axis=0)
        rows = slice(j * QBLK, (j + 1) * QBLK)
        for sw, q in enumerate(_stacked_queries(qc_ref, rows, low_q, _C_GROUPS)):
            k_loc = kce[j * QBLK:(j + 3) * QBLK, sw * LANES:(sw + 1) * LANES]
            s_c[j % 2][2 * sw * QBLK:2 * (sw + 1) * QBLK, :] = _dot_nt(q, k_loc) + window

    def c_softmax(j):
        step = slice(4 * j * QBLK, 4 * (j + 1) * QBLK)
        sink = jnp.concatenate([jnp.full((QBLK, 1), sink_ref[layer, 2 * t + half] * LOG2E, F32)
                                for group in _C_GROUPS for t, half in group], axis=0)
        e, ctx_scale, sink_term = _local_softmax(s_c[j % 2][...], m_ctx[step, :], sink)
        values = [(vce[j * QBLK:(j + 3) * QBLK, sw * LANES:(sw + 2) * LANES], sw) for sw in range(2)]
        r0, r1 = _local_output(e, values, acc_ctx[step, :], ctx_scale, sink_term)
        yc_scr[j * QBLK:(j + 1) * QBLK, :] = jnp.concatenate(
            [jnp.where(low_q, r0[0:QBLK], r1[0:QBLK]), jnp.where(low_q, r1[QBLK:], r0[QBLK:])], axis=1)

    _skewed(blocks_per_tile, (c_scores, c_softmax))

    halo = 4 * GRID_W
    kde[0:halo, :] = kdp_ref[...]
    kde[halo:halo + TILE, :] = kd_ref[...]
    kde[halo + TILE:, :] = kdn_ref[...]
    for c_src, c_dst in ((0, 0), (LANES, 2 * LANES)):
        vde[0:halo, c_dst:c_dst + LANES] = vdp_ref[:, c_src:c_src + LANES]
        vde[halo:halo + TILE, c_dst:c_dst + LANES] = vd_ref[:, c_src:c_src + LANES]
        vde[halo + TILE:, c_dst:c_dst + LANES] = vdn_ref[:, c_src:c_src + LANES]
    vde[:, LANES:2 * LANES] = jnp.ones((TILE + 2 * halo, LANES), BF16)
    kd_ctx = cdk_ref[...].astype(BF16)
    cdv = cdv_ref[...].T.astype(BF16)
    vd_ctx = jnp.concatenate([cdv[:, 0:LANES], ones_ctx, cdv[:, LANES:2 * LANES]], axis=1)
    low_r = _low_half(GRID_W)
    n_rows = DEC_SEQ // GRID_W

    for t, q in enumerate(_stacked_queries(qd_ref, slice(None), low_t, _D_GROUPS)):
        m, acc = _context_pass(q, kd_ctx[t * LANES:(t + 1) * LANES], vd_ctx[:, t * LANES:(t + 2) * LANES])
        file_context(m, acc, t, GRID_W, ROWS_PER_TILE)

    def window_start(rl):
        r = i * ROWS_PER_TILE + rl
        rs = jnp.clip(r - WIN_R // 2, 0, n_rows - WIN_R)
        return rs - r + (WIN_R - 1), pl.multiple_of((rs - i * ROWS_PER_TILE + 4) * GRID_W, GRID_W)

    def d_scores(rl):
        didx, off = window_start(rl)
        rows = slice(rl * GRID_W, (rl + 1) * GRID_W)
        for t, q in enumerate(_stacked_queries(qd_ref, rows, low_r, _D_GROUPS)):
            k_loc = kde[pl.ds(off, WIN_R * GRID_W), t * LANES:(t + 1) * LANES]
            bias = jnp.concatenate(
                [jnp.concatenate([tab_ref[h, didx + 2 * p] for p in range(WIN_R // 2)], axis=1)
                 for h in (2 * t, 2 * t + 1)], axis=0)
            s_d[rl % 2][2 * t * GRID_W:2 * (t + 1) * GRID_W, :] = _dot_nt(q, k_loc) + bias

    def d_softmax(rl):
        _, off = window_start(rl)
        step = slice(4 * rl * GRID_W, 4 * (rl + 1) * GRID_W)
        e, ctx_scale, _ = _local_softmax(s_d[rl % 2][...], m_ctx[step, :])
        values = [(vde[pl.ds(off, WIN_R * GRID_W), t * LANES:(t + 2) * LANES], t) for t in range(2)]
        r = _local_output(e, values, acc_ctx[step, :], ctx_scale)
        yd_scr[rl * GRID_W:(rl + 1) * GRID_W, :] = jnp.concatenate(
            [jnp.where(low_r, r[t][0:GRID_W], r[t][GRID_W:]) for t in range(2)], axis=1)

    _skewed(ROWS_PER_TILE, (d_scores, d_softmax))

    yc = (yc_scr[...] * sz_ref[:, 0:256]).astype(BF16)
    yd = (yd_scr[...] * sz_ref[:, 256:512]).astype(BF16)
    y = (_dot(yab_ref[...], w_out_ref[0:2 * GROUP_W, :]) + _dot(yc, w_out_ref[2 * GROUP_W:3 * GROUP_W, :])
         + _dot(yd, w_out_ref[3 * GROUP_W:4 * GROUP_W, :]))
    x_new = x_ref[...] + mod_ref[2, pl.ds(1 + b, 1), :] * y
    if layer == DEPTH - 1:
        x_new = _rms(x_new) * fg_ref[...]
    xo_ref[...] = x_new


def _lat_mix(layer, x, mod, final_g, yab, sz, qc, kc, vc, qd, kd, vd, cck, ccv, cdk, cdv, tab, w_out, sink):
    tile = lambda w: pl.BlockSpec((None, TILE, w), lambda b, i: (b, i, 0))
    per_layer = lambda *shape: pl.BlockSpec((None,) + shape, lambda b, i: (layer,) + (0,) * len(shape))

    def halo(rows, w):
        per = TILE // rows
        last = DEC_SEQ // rows - 1
        prev = pl.BlockSpec((None, rows, w), lambda b, i: (b, jnp.maximum(i * per - 1, 0), 0))
        nxt = pl.BlockSpec((None, rows, w), lambda b, i: (b, jnp.minimum((i + 1) * per, last), 0))
        return prev, nxt

    cprev, cnext = halo(QBLK, 256)
    dprev, dnext = halo(4 * GRID_W, 256)
    cache = lambda w: pl.BlockSpec((None, None, w, PAST_LEN), lambda b, i: (b, layer, 0, 0))
    return pl.pallas_call(
        functools.partial(_lat_mix_kernel, layer),
        grid=(DEC_BATCH, N_TILES),
        in_specs=[
            pl.BlockSpec(memory_space=pltpu.SMEM),
            tile(D_MODEL),
            per_layer(3, N_GROUPS, D_MODEL),
            pl.BlockSpec((1, D_MODEL), lambda b, i: (0, 0)),
            tile(512), tile(512),
            tile(256), tile(256), cprev, cnext, tile(256), cprev, cnext,
            tile(256), tile(256), dprev, dnext, tile(256), dprev, dnext,
            cache(128), cache(128), cache(256), cache(256),
            pl.BlockSpec((3, QBLK, 3 * QBLK), lambda b, i: (0, 0, 0)),
            per_layer(4, N_BIAS_TILES, GRID_W, LANES),
            per_layer(D_MODEL, D_MODEL),
        ],
        out_specs=tile(D_MODEL),
        out_shape=jax.ShapeDtypeStruct((DEC_BATCH, DEC_SEQ, D_MODEL), F32),
        scratch_shapes=[
            pltpu.VMEM((TILE + 2 * QBLK, 2 * LANES), BF16),
            pltpu.VMEM((TILE + 2 * QBLK, 3 * LANES), BF16),
            pltpu.VMEM((TILE + 8 * GRID_W, 2 * LANES), BF16),
            pltpu.VMEM((TILE + 8 * GRID_W, 3 * LANES), BF16),
            pltpu.VMEM((TILE, 256), F32),
            pltpu.VMEM((TILE, 256), F32),
            pltpu.VMEM((4 * TILE, 1), F32),
            pltpu.VMEM((4 * TILE, 2 * LANES), F32),
        ] + 2 * [pltpu.VMEM((4 * QBLK, 3 * QBLK), F32)] + 2 * [pltpu.VMEM((4 * GRID_W, WIN_R * GRID_W), F32)],
        compiler_params=pltpu.CompilerParams(dimension_semantics=("arbitrary", "arbitrary"),
                                             vmem_limit_bytes=VMEM_LIMIT),
        name="lat_mix",
    )(sink, x, mod, final_g, yab, sz, qc, kc, kc, kc, vc, vc, vc,
      qd, kd, kd, kd, vd, vd, vd, cck, ccv, cdk, cdv, _window_masks(), tab, w_out)


def _window_masks():
    p = np.arange(QBLK)[:, None]
    j = np.arange(3 * QBLK)[None, :]
    band = np.abs(j - QBLK - p) <= WINDOW
    masks = [band, band & (j >= QBLK), band & (j < 2 * QBLK)]
    return jnp.asarray(np.where(np.stack(masks), 0.0, NEG), F32)


def _rope_tables():
    t = np.arange(DEC_SEQ)
    freqs = (np.float32(ROPE_BASE) ** (-np.arange(16, dtype=np.float32) / np.float32(16))).astype(np.float32)
    ang_r = (t // GRID_W).astype(np.float32)[:, None] * freqs
    ang_c = (t % GRID_W).astype(np.float32)[:, None] * freqs
    cos_h = np.concatenate([np.cos(ang_r), np.cos(ang_r), np.cos(ang_c), np.cos(ang_c)], axis=1)
    sin_h = np.concatenate([-np.sin(ang_r), np.sin(ang_r), -np.sin(ang_c), np.sin(ang_c)], axis=1)
    return (jnp.asarray(np.tile(cos_h, (1, 2)), F32), jnp.asarray(np.tile(sin_h, (1, 2)), F32))


def kernel(x_prompt, x_sample, cache_c_k, cache_c_v, cache_d_k, cache_d_v, c, c_ctx, norm_g, w_mod, b_mod,
           w_in, w_out, w_s, b_s, w_conv, sink, rpb, final_g):
    w_in_b = w_in.astype(BF16)
    w_out_b = w_out.astype(BF16)
    ws_cat = jnp.transpose(w_s, (0, 2, 1, 3)).reshape(DEPTH, CHUNK, A_HEADS * CHUNK).astype(BF16)
    bias_a = jnp.repeat(jnp.transpose(b_s, (0, 2, 1)), HEAD_DIM, axis=2)
    norm_g3 = norm_g.reshape(DEPTH, 1, D_MODEL)
    fg = final_g.reshape(1, D_MODEL)
    cos_t, sin_t = _rope_tables()
    feature_major = lambda cache, w: jnp.transpose(cache, (0, 1, 3, 4, 2)).reshape(DEC_BATCH, DEPTH, w, PAST_LEN)
    cck = feature_major(cache_c_k, 128)
    ccv = feature_major(cache_c_v, 128)
    cdk = feature_major(cache_d_k, 256)
    cdv = feature_major(cache_d_v, 256)

    cond = jnp.concatenate([c_ctx[None, :], c, jnp.zeros((N_GROUPS - 1 - DEC_BATCH, D_MODEL), F32)], axis=0)
    mod, tab = _modulation_and_tables(cond, w_mod, b_mod, rpb)

    y_prompt, s_ck, s_cv, s_dk, s_dv = _ctx_layers(x_prompt, mod, norm_g3, fg, w_in_b, w_out_b, ws_cat,
                                                   bias_a, w_conv, sink)
    xs = x_sample
    for l in range(DEPTH):
        yab, sz, qc, kc, vc, qd, kd, vd = _lat_proj(l, xs, mod, norm_g3, w_in_b, cos_t, sin_t,
                                                    ws_cat, bias_a, w_conv)
        xs = _lat_mix(l, xs, mod, fg, yab, sz, qc, kc, vc, qd, kd, vd, cck, ccv, cdk, cdv, tab, w_out_b, sink)

    shape_c = (BATCH, DEPTH, SEQ, 2, HEAD_DIM)
    shape_d = (BATCH, DEPTH, SEQ, 4, HEAD_DIM)
    return (y_prompt, xs, s_ck.reshape(shape_c), s_cv.reshape(shape_c), s_dk.reshape(shape_d),
            s_dv.reshape(shape_d))
```

```python
import functools

import numpy as np
import jax
import jax.numpy as jnp
from jax import lax
from jax.experimental import pallas as pl
from jax.experimental.pallas import tpu as pltpu

D_MODEL = 1024
BATCH = 16
SEQ = 256
DEPTH = 4
DEC_BATCH = 2
DEC_SEQ = 4096
PAST_LEN = 512
GRID_W = 64
GROUP_W = 256
HEAD_DIM = 64
A_HEADS = 4
CHUNK = 128
WINDOW = 128
QBLK = 128
WIN_R = 8
WIN_C = 16
ROPE_BASE = 10000.0
EPS = 1e-6
NEG = -1e30
D_IN = 3584
LOG2E = 1.4426950408889634
Q_SCALE = HEAD_DIM ** -0.5 * LOG2E

AU, AV, AZ, BB, BC, BH, BZ = 0, 256, 512, 768, 1024, 1280, 1536
CQ, CKV, CZ, DQ, DK, DV, DZ = 1792, 2048, 2304, 2560, 2816, 3072, 3328

LANES = 128
TILE = 512
ROWS_PER_TILE = TILE // GRID_W
N_TILES = DEC_SEQ // TILE
CTX_SEQS = 4
PROJ_TILES = 2
N_GROUPS = 8
VMEM_LIMIT = 56 * 1024 * 1024

F32 = jnp.float32
BF16 = jnp.bfloat16


def _silu(z):
    return z * (1.0 / (1.0 + jnp.exp(-z)))


def _dot(a, b):
    return jnp.dot(a, b, preferred_element_type=F32)


def _dot_nt(a, b):
    return lax.dot_general(a, b, (((1,), (1,)), ((), ())), preferred_element_type=F32)


def _rms(x):
    return x * lax.rsqrt(jnp.mean(x * x, axis=-1, keepdims=True) + EPS)


def _norm_mod(x, g, shift, scale):
    return _rms(x) * (g * (1.0 + scale)) + shift


def _swap64(x):
    return pltpu.roll(x, HEAD_DIM, axis=1)


def _group_mean_matrix():
    r = lax.broadcasted_iota(jnp.int32, (GROUP_W, GROUP_W), 0) // HEAD_DIM
    c = lax.broadcasted_iota(jnp.int32, (GROUP_W, GROUP_W), 1) // HEAD_DIM
    return jnp.where(r == c, 1.0 / HEAD_DIM, 0.0).astype(BF16)


def _branch_a(au, av, az, ws, bias):
    t = av.shape[0]
    sq = av * av
    hi = sq.astype(BF16)
    lo = (sq - hi.astype(F32)).astype(BF16)
    gm = _group_mean_matrix()
    ms = _dot(hi, gm) + _dot(lo, gm)
    vh = (av * lax.rsqrt(ms + EPS)).astype(BF16)
    head = lax.broadcasted_iota(jnp.int32, (CHUNK, GROUP_W), 1) // HEAD_DIM
    outs = []
    for n in range(t // CHUNK):
        v = vh[n * CHUNK:(n + 1) * CHUNK]
        rhs = jnp.concatenate([jnp.where(head == h, v, jnp.zeros_like(v)) for h in range(A_HEADS)], axis=0)
        outs.append(_dot(ws, rhs) + bias)
    mixed = jnp.concatenate(outs, axis=0)
    return au * mixed * _silu(az)


def _branch_b(bb, bc, bh, bz, wc, prev_row, next_row):
    t = bb.shape[0]
    xc = bc * bh
    row = lax.broadcasted_iota(jnp.int32, xc.shape, 0)
    xm = jnp.where(row == 0, prev_row, pltpu.roll(xc, 1, axis=0))
    xp = jnp.where(row == t - 1, next_row, pltpu.roll(xc, t - 1, axis=0))
    y = wc[0:1, :] * xm + wc[1:2, :] * xc + wc[2:3, :] * xp
    return bb * y * _silu(bz)


def _attend(score, value, pos, sink=None):
    m = jnp.max(score, axis=-1, keepdims=True)
    if sink is not None:
        m = jnp.maximum(m, sink * LOG2E)
    acc = _dot(jnp.exp2(score - m).astype(BF16), value)
    o = acc[:, pos * LANES:(pos + 1) * LANES]
    l = acc[:, (1 - pos) * LANES:(2 - pos) * LANES]
    if sink is not None:
        l = l + jnp.exp2(sink * LOG2E - m)
    return o * (1.0 / l)


def _context_pass(q_stacked, k_t, value):
    s = _dot(q_stacked, k_t)
    m = jnp.max(s, axis=-1, keepdims=True)
    return m, _dot(jnp.exp2(s - m).astype(BF16), value)


def _local_softmax(s_loc, m_ctx, sink=None):
    m = jnp.maximum(jnp.max(s_loc, axis=-1, keepdims=True), m_ctx)
    if sink is not None:
        m = jnp.maximum(m, sink)
    e = jnp.exp2(s_loc - m).astype(BF16)
    return e, jnp.exp2(m_ctx - m), (None if sink is None else jnp.exp2(sink - m))


def _local_output(e, values, acc_ctx, ctx_scale, sink_term=None):
    rows = e.shape[0] // len(values)
    outs = []
    for p, (v, pos) in enumerate(values):
        r = slice(p * rows, (p + 1) * rows)
        acc = _dot(e[r], v) + acc_ctx[r] * ctx_scale[r]
        o = acc[:, pos * LANES:(pos + 1) * LANES]
        l = acc[:, (1 - pos) * LANES:(2 - pos) * LANES]
        if sink_term is not None:
            l = l + sink_term[r]
        outs.append(o * (1.0 / l))
    return outs


def _skewed(n_steps, stages):
    for tick in range(n_steps + len(stages) - 1):
        for lag, stage in enumerate(stages):
            if 0 <= tick - lag < n_steps:
                stage(tick - lag)


def _low_half(m):
    return lax.broadcasted_iota(jnp.int32, (m, LANES), 1) < HEAD_DIM


def _mask_half(q, low, half):
    z = jnp.zeros_like(q)
    return jnp.where(low, q, z) if half == 0 else jnp.where(low, z, q)


def _out_proj(x, gate, ya, yb, yc, yd, w_out_ref):
    y = _dot(ya.astype(BF16), w_out_ref[0:256, :])
    y += _dot(yb.astype(BF16), w_out_ref[256:512, :])
    y += _dot(yc.astype(BF16), w_out_ref[512:768, :])
    y += _dot(yd.astype(BF16), w_out_ref[768:1024, :])
    return x + gate * y


N_BIAS_TILES = 2 * WIN_R - 2


def _mod_kernel(cond_ref, w0_ref, w1_ref, w2_ref, b_ref, rpb_ref, o_ref, tab_ref):
    cond = _silu(cond_ref[...]).astype(BF16)
    for j, w_ref in enumerate((w0_ref, w1_ref, w2_ref)):
        o_ref[j] = _dot(cond, w_ref[...].astype(BF16)) + b_ref[:, j * D_MODEL:(j + 1) * D_MODEL]
    _nbr_table(rpb_ref, tab_ref)


def _modulation_and_tables(cond, w_mod, b_mod, rpb):
    half = LANES // 2 - (2 * WIN_C - 1)
    rows = jnp.pad(rpb, ((0, 0), (0, 0), (0, 1), (0, half)))
    rpb_p = jnp.concatenate([rows[:, :, :-1], rows[:, :, 1:]], axis=-1)
    rpb_p = jnp.pad(rpb_p, ((0, 0), (0, 0), (0, 1), (0, 0)))
    return pl.pallas_call(
        _mod_kernel,
        grid=(DEPTH,),
        in_specs=[
            pl.BlockSpec((N_GROUPS, D_MODEL), lambda l: (0, 0)),
            pl.BlockSpec((None, D_MODEL, D_MODEL), lambda l: (l, 0, 0)),
            pl.BlockSpec((None, D_MODEL, D_MODEL), lambda l: (l, 0, 1)),
            pl.BlockSpec((None, D_MODEL, D_MODEL), lambda l: (l, 0, 2)),
            pl.BlockSpec((None, 1, 3 * D_MODEL), lambda l: (l, 0, 0)),
            pl.BlockSpec((None, 4, 2 * WIN_R, LANES), lambda l: (l, 0, 0, 0)),
        ],
        out_specs=[pl.BlockSpec((None, 3, N_GROUPS, D_MODEL), lambda l: (l, 0, 0, 0)),
                   pl.BlockSpec((None, 4, N_BIAS_TILES, GRID_W, LANES), lambda l: (l, 0, 0, 0, 0))],
        out_shape=[jax.ShapeDtypeStruct((DEPTH, 3, N_GROUPS, D_MODEL), F32),
                   jax.ShapeDtypeStruct((DEPTH, 4, N_BIAS_TILES, GRID_W, LANES), F32)],
        compiler_params=pltpu.CompilerParams(dimension_semantics=("arbitrary",),
                                             vmem_limit_bytes=VMEM_LIMIT),
        name="modulation",
    )(cond, w_mod, w_mod, w_mod, b_mod.reshape(DEPTH, 1, 3 * D_MODEL), rpb_p)


def _nbr_table(rpb_ref, o_ref):
    c = lax.broadcasted_iota(jnp.int32, (GRID_W, LANES), 0)
    lane = lax.broadcasted_iota(jnp.int32, (GRID_W, LANES), 1)
    cc = lane % GRID_W
    cstart = jnp.clip(c - WIN_C // 2, 0, GRID_W - WIN_C)
    for h in range(4):
        for ro in range(N_BIAS_TILES):
            v = pltpu.roll(jnp.broadcast_to(rpb_ref[h, ro:ro + 1, :] * LOG2E, (GRID_W, LANES)),
                           LANES - (WIN_C - 1), 1, stride=1, stride_axis=0)
            o_ref[h, ro] = jnp.where(cc >= cstart, jnp.where(cc < cstart + WIN_C, v, NEG), NEG)


def _ctx_kernel(sink_ref, x_ref, mod_ref, g_ref, fg_ref, w_in_ref, w_out_ref, ws_ref, ba_ref, wc_ref,
                y_ref, ck_ref, cv_ref, dk_ref, dv_ref, xs):
    l = pl.program_id(0)
    pair = pl.program_id(1)

    @pl.when(l == 0)
    def _():
        for u in range(CTX_SEQS):
            xs[CTX_SEQS * pair + u] = x_ref[u]

    low = _low_half(SEQ)
    ones = jnp.ones((SEQ, LANES), BF16)

    def project(u):
        x = xs[CTX_SEQS * pair + u]
        h = _norm_mod(x, g_ref[...], mod_ref[0, 0:1, :], mod_ref[1, 0:1, :]).astype(BF16)
        return x, _dot(h, w_in_ref[...])

    def mix(u, p):
        kcf = p[:, CKV:CKV + 128]
        vcf = p[:, CKV + 128:CKV + 256]
        kdf = p[:, DK:DK + 256]
        vdf = p[:, DV:DV + 256]
        ck_ref[u] = kcf
        cv_ref[u] = vcf
        dk_ref[u] = kdf
        dv_ref[u] = vdf

        ya = _branch_a(p[:, AU:AU + 256], p[:, AV:AV + 256], p[:, AZ:AZ + 256], ws_ref[...], ba_ref[...])
        zero_row = jnp.zeros((1, GROUP_W), F32)
        yb = _branch_b(p[:, BB:BB + 256], p[:, BC:BC + 256], p[:, BH:BH + 256], p[:, BZ:BZ + 256],
                       wc_ref[...], zero_row, zero_row)

        kc = (kcf.astype(BF16), _swap64(kcf).astype(BF16))
        vc = (jnp.concatenate([vcf.astype(BF16), ones], axis=1),
              jnp.concatenate([ones, _swap64(vcf).astype(BF16)], axis=1))
        o = []
        for t in range(2):
            q = (p[:, CQ + t * LANES:CQ + (t + 1) * LANES] * Q_SCALE).astype(BF16)
            for half in range(2):
                sw = (t + half) % 2
                o.append(_attend(_dot_nt(_mask_half(q, low, half), kc[sw]), vc[sw], sw,
                                 sink_ref[l, 2 * t + half]))
        yc = jnp.concatenate([jnp.where(low, o[0], o[1]), jnp.where(low, o[2], o[3])], axis=1)
        yc = yc * _silu(p[:, CZ:CZ + 256])

        o = []
        for t in range(2):
            q = (p[:, DQ + t * LANES:DQ + (t + 1) * LANES] * Q_SCALE).astype(BF16)
            k = kdf[:, t * LANES:(t + 1) * LANES].astype(BF16)
            v = jnp.concatenate([vdf[:, t * LANES:(t + 1) * LANES].astype(BF16), ones], axis=1)
            for half in range(2):
                o.append(_attend(_dot_nt(_mask_half(q, low, half), k), v, 0))
        yd = jnp.concatenate([jnp.where(low, o[0], o[1]), jnp.where(low, o[2], o[3])], axis=1)
        yd = yd * _silu(p[:, DZ:DZ + 256])
        return ya, yb, yc, yd

    def finish(u, x, ys):
        x_new = _out_proj(x, mod_ref[2, 0:1, :], *ys, w_out_ref)
        xs[CTX_SEQS * pair + u] = x_new
        return x_new

    staged = [project(0)]
    x_new = []
    for u in range(CTX_SEQS):
        if u + 1 < CTX_SEQS:
            staged.append(project(u + 1))
        x, p = staged[u]
        x_new.append(finish(u, x, mix(u, p)))

    @pl.when(l == DEPTH - 1)
    def _():
        for u in range(CTX_SEQS):
            y_ref[u] = _rms(x_new[u]) * fg_ref[...]


def _ctx_layers(x, mod, norm_g, final_g, w_in, w_out, ws_cat, bias_a, w_conv, sink):
    per_layer = lambda *shape: pl.BlockSpec((None,) + shape, lambda l, b: (l,) + (0,) * len(shape))
    state = lambda w: pl.BlockSpec((CTX_SEQS, None, SEQ, w), lambda l, b: (b, l, 0, 0))
    n_steps = BATCH // CTX_SEQS
    return pl.pallas_call(
        _ctx_kernel,
        grid=(DEPTH, n_steps),
        in_specs=[
            pl.BlockSpec(memory_space=pltpu.SMEM),
            pl.BlockSpec((CTX_SEQS, SEQ, D_MODEL), lambda l, b: (jnp.where(l == 0, b, n_steps - 1), 0, 0)),
            per_layer(3, N_GROUPS, D_MODEL),
            per_layer(1, D_MODEL),
            pl.BlockSpec((1, D_MODEL), lambda l, b: (0, 0)),
            pl.BlockSpec((None, D_MODEL, D_IN), lambda l, b: (l, 0, 0), pipeline_mode=pl.Buffered(1)),
            pl.BlockSpec((None, D_MODEL, D_MODEL), lambda l, b: (l, 0, 0), pipeline_mode=pl.Buffered(1)),
            per_layer(CHUNK, A_HEADS * CHUNK),
            per_layer(CHUNK, GROUP_W),
            per_layer(3, GROUP_W),
        ],
        out_specs=[
            pl.BlockSpec((CTX_SEQS, SEQ, D_MODEL), lambda l, b: (jnp.where(l == DEPTH - 1, b, 0), 0, 0)),
            state(128), state(128), state(256), state(256),
        ],
        out_shape=[
            jax.ShapeDtypeStruct((BATCH, SEQ, D_MODEL), F32),
            jax.ShapeDtypeStruct((BATCH, DEPTH, SEQ, 128), F32),
            jax.ShapeDtypeStruct((BATCH, DEPTH, SEQ, 128), F32),
            jax.ShapeDtypeStruct((BATCH, DEPTH, SEQ, 256), F32),
            jax.ShapeDtypeStruct((BATCH, DEPTH, SEQ, 256), F32),
        ],
        scratch_shapes=[pltpu.VMEM((BATCH, SEQ, D_MODEL), F32)],
        compiler_params=pltpu.CompilerParams(dimension_semantics=("arbitrary", "arbitrary"),
                                             vmem_limit_bytes=VMEM_LIMIT),
        name="ctx_layers",
    )(sink, x, mod, norm_g, final_g, w_in, w_out, ws_cat, bias_a, w_conv)


def _rope(x, cos, sin_signed, first_half):
    swapped = jnp.where(first_half, pltpu.roll(x, LANES - 16, axis=1), pltpu.roll(x, 16, axis=1))
    return x * cos + swapped * sin_signed


def _lat_proj_kernel(x_ref, xp_ref, xn_ref, mod_ref, g_ref, w_ref, cos_ref, sin_ref, ws_ref, ba_ref, wc_ref,
                     yab_ref, sz_ref, qc_ref, kc_ref, vc_ref, qd_ref, kd_ref, vd_ref):
    i = pl.program_id(1)
    grp = pl.ds(1 + pl.program_id(0), 1)
    g = g_ref[...]
    shift = mod_ref[0, grp, :]
    scale = mod_ref[1, grp, :]
    first_half = (lax.broadcasted_iota(jnp.int32, (TILE, LANES), 1) % 32) < 16

    def normalised(x):
        return _norm_mod(x, g, shift, scale).astype(BF16)

    def project(u, h, after_first_dot):
        rows = slice(u * TILE, (u + 1) * TILE)
        tile_index = PROJ_TILES * i + u
        pa = _dot(h, w_ref[:, AU:AU + 3 * GROUP_W])
        after_first_dot()
        ya = _branch_a(pa[:, 0:256], pa[:, 256:512], pa[:, 512:768], ws_ref[...], ba_ref[...])
        yab_ref[rows, 0:GROUP_W] = ya.astype(BF16)

        pb = _dot(h, w_ref[:, BB:BB + 4 * GROUP_W])
        before = xp_ref[...] if u == 0 else x_ref[u * TILE - 8:u * TILE, :]
        after = xn_ref[...] if u == PROJ_TILES - 1 else x_ref[(u + 1) * TILE:(u + 1) * TILE + 8, :]
        e = _dot(normalised(jnp.concatenate([before, after], axis=0)), w_ref[:, BC:BC + 2 * GROUP_W])
        prev_row = jnp.where(tile_index > 0, e[7:8, 0:256] * e[7:8, 256:512], 0.0)
        next_row = jnp.where(tile_index < N_TILES - 1, e[8:9, 0:256] * e[8:9, 256:512], 0.0)
        yb = _branch_b(pb[:, 0:256], pb[:, 256:512], pb[:, 512:768], pb[:, 768:1024], wc_ref[...],
                       prev_row, next_row)
        yab_ref[rows, GROUP_W:2 * GROUP_W] = yb.astype(BF16)

        sz_ref[rows, 0:256] = _silu(_dot(h, w_ref[:, CZ:CZ + 256]))
        sz_ref[rows, 256:512] = _silu(_dot(h, w_ref[:, DZ:DZ + 256]))
        cos = cos_ref[rows, :]
        sin = sin_ref[rows, :]
        qc = _dot(h, w_ref[:, CQ:CQ + 256])
        for t in range(2):
            qt = _rope(qc[:, t * LANES:(t + 1) * LANES], cos, sin, first_half) * Q_SCALE
            qc_ref[rows, t * LANES:(t + 1) * LANES] = qt.astype(BF16)
        kv = _dot(h, w_ref[:, CKV:CKV + 256])
        k = _rope(kv[:, 0:LANES], cos, sin, first_half)
        v = kv[:, LANES:2 * LANES]
        kc_ref[rows, 0:LANES] = k.astype(BF16)
        kc_ref[rows, LANES:2 * LANES] = _swap64(k).astype(BF16)
        vc_ref[rows, 0:LANES] = v.astype(BF16)
        vc_ref[rows, LANES:2 * LANES] = _swap64(v).astype(BF16)
        qd_ref[rows, :] = (_dot(h, w_ref[:, DQ:DQ + 256]) * Q_SCALE).astype(BF16)
        kd_ref[rows, :] = _dot(h, w_ref[:, DK:DK + 256]).astype(BF16)
        vd_ref[rows, :] = _dot(h, w_ref[:, DV:DV + 256]).astype(BF16)

    hs = [normalised(x_ref[0:TILE, :])]

    def normalise_next(u):
        if u + 1 < PROJ_TILES:
            hs.append(normalised(x_ref[(u + 1) * TILE:(u + 2) * TILE, :]))

    for u in range(PROJ_TILES):
        project(u, hs[u], functools.partial(normalise_next, u))


def _lat_proj(layer, x, mod, norm_g, w_in, cos_t, sin_t, ws_cat, bias_a, w_conv):
    step_rows = PROJ_TILES * TILE
    tile = lambda w: pl.BlockSpec((None, step_rows, w), lambda b, i: (b, i, 0))
    per_layer = lambda *shape: pl.BlockSpec((None,) + shape, lambda b, i: (layer,) + (0,) * len(shape))
    sds = lambda w, dt: jax.ShapeDtypeStruct((DEC_BATCH, DEC_SEQ, w), dt)
    edge = step_rows // 8
    last = DEC_SEQ // 8 - 1
    return pl.pallas_call(
        _lat_proj_kernel,
        grid=(DEC_BATCH, N_TILES // PROJ_TILES),
        in_specs=[
            tile(D_MODEL),
            pl.BlockSpec((None, 8, D_MODEL), lambda b, i: (b, jnp.maximum(i * edge - 1, 0), 0)),
            pl.BlockSpec((None, 8, D_MODEL), lambda b, i: (b, jnp.minimum((i + 1) * edge, last), 0)),
            per_layer(3, N_GROUPS, D_MODEL),
            per_layer(1, D_MODEL),
            per_layer(D_MODEL, D_IN),
            pl.BlockSpec((step_rows, LANES), lambda b, i: (i, 0)),
            pl.BlockSpec((step_rows, LANES), lambda b, i: (i, 0)),
            per_layer(CHUNK, A_HEADS * CHUNK),
            per_layer(CHUNK, GROUP_W),
            per_layer(3, GROUP_W),
        ],
        out_specs=[tile(512), tile(512)] + [tile(256)] * 6,
        out_shape=[sds(512, BF16), sds(512, F32)] + [sds(256, BF16)] * 6,
        compiler_params=pltpu.CompilerParams(dimension_semantics=("arbitrary", "arbitrary"),
                                             vmem_limit_bytes=VMEM_LIMIT),
        name="lat_proj",
    )(x, x, x, mod, norm_g, w_in, cos_t, sin_t, ws_cat, bias_a, w_conv)


_C_GROUPS = (((0, 0), (1, 1)), ((0, 1), (1, 0)))
_D_GROUPS = (((0, 0), (0, 1)), ((1, 0), (1, 1)))


def _stacked_queries(q_ref, rows, low, groups):
    return [jnp.concatenate([_mask_half(q_ref[rows, t * LANES:(t + 1) * LANES], low, half)
                             for t, half in group], axis=0) for group in groups]


def _lat_mix_kernel(layer, sink_ref, x_ref, mod_ref, fg_ref, yab_ref, sz_ref,
                    qc_ref, kc_ref, kcp_ref, kcn_ref, vc_ref, vcp_ref, vcn_ref,
                    qd_ref, kd_ref, kdp_ref, kdn_ref, vd_ref, vdp_ref, vdn_ref,
                    cck_ref, ccv_ref, cdk_ref, cdv_ref, cmask_ref, tab_ref, w_out_ref,
                    xo_ref,
                    kce, vce, kde, vde, yc_scr, yd_scr, m_ctx, acc_ctx, *stage_bufs):
    b = pl.program_id(0)
    i = pl.program_id(1)
    s_c, s_d = stage_bufs[0:2], stage_bufs[2:4]

    def file_context(m, acc, group, step_rows, n_steps):
        for which in range(2):
            for k in range(n_steps):
                src = slice(which * TILE + k * step_rows, which * TILE + (k + 1) * step_rows)
                dst = slice((4 * k + 2 * group + which) * step_rows, (4 * k + 2 * group + which + 1) * step_rows)
                m_ctx[dst, :] = m[src]
                acc_ctx[dst, :] = acc[src]

    kce[0:QBLK, :] = kcp_ref[...]
    kce[QBLK:QBLK + TILE, :] = kc_ref[...]
    kce[QBLK + TILE:, :] = kcn_ref[...]
    for c_src, c_dst in ((0, 0), (LANES, 2 * LANES)):
        vce[0:QBLK, c_dst:c_dst + LANES] = vcp_ref[:, c_src:c_src + LANES]
        vce[QBLK:QBLK + TILE, c_dst:c_dst + LANES] = vc_ref[:, c_src:c_src + LANES]
        vce[QBLK + TILE:, c_dst:c_dst + LANES] = vcn_ref[:, c_src:c_src + LANES]
    vce[:, LANES:2 * LANES] = jnp.ones((TILE + 2 * QBLK, LANES), BF16)
    swap_rows = lambda x: jnp.concatenate([x[HEAD_DIM:], x[:HEAD_DIM]], axis=0)
    cck = cck_ref[...].astype(BF16)
    ccv = ccv_ref[...].T
    ones_ctx = jnp.ones((PAST_LEN, LANES), BF16)
    k_ctx = (cck, swap_rows(cck))
    v_ctx = jnp.concatenate([ccv.astype(BF16), ones_ctx, _swap64(ccv).astype(BF16)], axis=1)
    low_q = _low_half(QBLK)
    low_t = _low_half(TILE)
    n_blocks = DEC_SEQ // QBLK
    blocks_per_tile = TILE // QBLK

    for sw, q in enumerate(_stacked_queries(qc_ref, slice(None), low_t, _C_GROUPS)):
        m, acc = _context_pass(q, k_ctx[sw], v_ctx[:, sw * LANES:(sw + 2) * LANES])
        file_context(m, acc, sw, QBLK, blocks_per_tile)

    def c_scores(j):
        n = i * blocks_per_tile + j
        window = cmask_ref[jnp.where(n == 0, 1, jnp.where(n == n_blocks - 1, 2, 0))]
        window = jnp.concatenate([window, window], axis=0)
        rows = slice(j * QBLK, (j + 1) * QBLK)
        for sw, q in enumerate(_stacked_queries(qc_ref, rows, low_q, _C_GROUPS)):
            k_loc = kce[j * QBLK:(j + 3) * QBLK, sw * LANES:(sw + 1) * LANES]
            s_c[j % 2][2 * sw * QBLK:2 * (sw + 1) * QBLK, :] = _dot_nt(q, k_loc) + window

    def c_softmax(j):
        step = slice(4 * j * QBLK, 4 * (j + 1) * QBLK)
        sink = jnp.concatenate([jnp.full((QBLK, 1), sink_ref[layer, 2 * t + half] * LOG2E, F32)
                                for group in _C_GROUPS for t, half in group], axis=0)
        e, ctx_scale, sink_term = _local_softmax(s_c[j % 2][...], m_ctx[step, :], sink)
        values = [(vce[j * QBLK:(j + 3) * QBLK, sw * LANES:(sw + 2) * LANES], sw) for sw in range(2)]
        r0, r1 = _local_output(e, values, acc_ctx[step, :], ctx_scale, sink_term)
        yc_scr[j * QBLK:(j + 1) * QBLK, :] = jnp.concatenate(
            [jnp.where(low_q, r0[0:QBLK], r1[0:QBLK]), jnp.where(low_q, r1[QBLK:], r0[QBLK:])], axis=1)

    _skewed(blocks_per_tile, (c_scores, c_softmax))

    halo = 4 * GRID_W
    kde[0:halo, :] = kdp_ref[...]
    kde[halo:halo + TILE, :] = kd_ref[...]
    kde[halo + TILE:, :] = kdn_ref[...]
    for c_src, c_dst in ((0, 0), (LANES, 2 * LANES)):
        vde[0:halo, c_dst:c_dst + LANES] = vdp_ref[:, c_src:c_src + LANES]
        vde[halo:halo + TILE, c_dst:c_dst + LANES] = vd_ref[:, c_src:c_src + LANES]
        vde[halo + TILE:, c_dst:c_dst + LANES] = vdn_ref[:, c_src:c_src + LANES]
    vde[:, LANES:2 * LANES] = jnp.ones((TILE + 2 * halo, LANES), BF16)
    kd_ctx = cdk_ref[...].astype(BF16)
    cdv = cdv_ref[...].T.astype(BF16)
    vd_ctx = jnp.concatenate([cdv[:, 0:LANES], ones_ctx, cdv[:, LANES:2 * LANES]], axis=1)
    low_r = _low_half(GRID_W)
    n_rows = DEC_SEQ // GRID_W

    for t, q in enumerate(_stacked_queries(qd_ref, slice(None), low_t, _D_GROUPS)):
        m, acc = _context_pass(q, kd_ctx[t * LANES:(t + 1) * LANES], vd_ctx[:, t * LANES:(t + 2) * LANES])
        file_context(m, acc, t, GRID_W, ROWS_PER_TILE)

    def window_start(rl):
        r = i * ROWS_PER_TILE + rl
        rs = jnp.clip(r - WIN_R // 2, 0, n_rows - WIN_R)
        return rs - r + (WIN_R - 1), pl.multiple_of((rs - i * ROWS_PER_TILE + 4) * GRID_W, GRID_W)

    def d_scores(rl):
        didx, off = window_start(rl)
        rows = slice(rl * GRID_W, (rl + 1) * GRID_W)
        for t, q in enumerate(_stacked_queries(qd_ref, rows, low_r, _D_GROUPS)):
            k_loc = kde[pl.ds(off, WIN_R * GRID_W), t * LANES:(t + 1) * LANES]
            bias = jnp.concatenate(
                [jnp.concatenate([tab_ref[h, didx + 2 * p] for p in range(WIN_R // 2)], axis=1)
                 for h in (2 * t, 2 * t + 1)], axis=0)
            s_d[rl % 2][2 * t * GRID_W:2 * (t + 1) * GRID_W, :] = _dot_nt(q, k_loc) + bias

    def d_softmax(rl):
        _, off = window_start(rl)
        step = slice(4 * rl * GRID_W, 4 * (rl + 1) * GRID_W)
        e, ctx_scale, _ = _local_softmax(s_d[rl % 2][...], m_ctx[step, :])
        values = [(vde[pl.ds(off, WIN_R * GRID_W), t * LANES:(t + 2) * LANES], t) for t in range(2)]
        r = _local_output(e, values, acc_ctx[step, :], ctx_scale)
        yd_scr[rl * GRID_W:(rl + 1) * GRID_W, :] = jnp.concatenate(
            [jnp.where(low_r, r[t][0:GRID_W], r[t][GRID_W:]) for t in range(2)], axis=1)

    _skewed(ROWS_PER_TILE, (d_scores, d_softmax))

    yc = (yc_scr[...] * sz_ref[:, 0:256]).astype(BF16)
    yd = (yd_scr[...] * sz_ref[:, 256:512]).astype(BF16)
    y = (_dot(yab_ref[...], w_out_ref[0:2 * GROUP_W, :]) + _dot(yc, w_out_ref[2 * GROUP_W:3 * GROUP_W, :])
         + _dot(yd, w_out_ref[3 * GROUP_W:4 * GROUP_W, :]))
    x_new = x_ref[...] + mod_ref[2, pl.ds(1 + b, 1), :] * y
    if layer == DEPTH - 1:
        x_new = _rms(x_new) * fg_ref[...]
    xo_ref[...] = x_new


def _lat_mix(layer, x, mod, final_g, yab, sz, qc, kc, vc, qd, kd, vd, cck, ccv, cdk, cdv, tab, w_out, sink):
    tile = lambda w: pl.BlockSpec((None, TILE, w), lambda b, i: (b, i, 0))
    per_layer = lambda *shape: pl.BlockSpec((None,) + shape, lambda b, i: (layer,) + (0,) * len(shape))

    def halo(rows, w):
        per = TILE // rows
        last = DEC_SEQ // rows - 1
        prev = pl.BlockSpec((None, rows, w), lambda b, i: (b, jnp.maximum(i * per - 1, 0), 0))
        nxt = pl.BlockSpec((None, rows, w), lambda b, i: (b, jnp.minimum((i + 1) * per, last), 0))
        return prev, nxt

    cprev, cnext = halo(QBLK, 256)
    dprev, dnext = halo(4 * GRID_W, 256)
    cache = lambda w: pl.BlockSpec((None, None, w, PAST_LEN), lambda b, i: (b, layer, 0, 0))
    return pl.pallas_call(
        functools.partial(_lat_mix_kernel, layer),
        grid=(DEC_BATCH, N_TILES),
        in_specs=[
            pl.BlockSpec(memory_space=pltpu.SMEM),
            tile(D_MODEL),
            per_layer(3, N_GROUPS, D_MODEL),
            pl.BlockSpec((1, D_MODEL), lambda b, i: (0, 0)),
            tile(512), tile(512),
            tile(256), tile(256), cprev, cnext, tile(256), cprev, cnext,
            tile(256), tile(256), dprev, dnext, tile(256), dprev, dnext,
            cache(128), cache(128), cache(256), cache(256),
            pl.BlockSpec((3, QBLK, 3 * QBLK), lambda b, i: (0, 0, 0)),
            per_layer(4, N_BIAS_TILES, GRID_W, LANES),
            per_layer(D_MODEL, D_MODEL),
        ],
        out_specs=tile(D_MODEL),
        out_shape=jax.ShapeDtypeStruct((DEC_BATCH, DEC_SEQ, D_MODEL), F32),
        scratch_shapes=[
            pltpu.VMEM((TILE + 2 * QBLK, 2 * LANES), BF16),
            pltpu.VMEM((TILE + 2 * QBLK, 3 * LANES), BF16),
            pltpu.VMEM((TILE + 8 * GRID_W, 2 * LANES), BF16),
            pltpu.VMEM((TILE + 8 * GRID_W, 3 * LANES), BF16),
            pltpu.VMEM((TILE, 256), F32),
            pltpu.VMEM((TILE, 256), F32),
            pltpu.VMEM((4 * TILE, 1), F32),
            pltpu.VMEM((4 * TILE, 2 * LANES), F32),
        ] + 2 * [pltpu.VMEM((4 * QBLK, 3 * QBLK), F32)] + 2 * [pltpu.VMEM((4 * GRID_W, WIN_R * GRID_W), F32)],
        compiler_params=pltpu.CompilerParams(dimension_semantics=("arbitrary", "arbitrary"),
                                             vmem_limit_bytes=VMEM_LIMIT),
        name="lat_mix",
    )(sink, x, mod, final_g, yab, sz, qc, kc, kc, kc, vc, vc, vc,
      qd, kd, kd, kd, vd, vd, vd, cck, ccv, cdk, cdv, _window_masks(), tab, w_out)


def _window_masks():
    p = np.arange(QBLK)[:, None]
    j = np.arange(3 * QBLK)[None, :]
    band = np.abs(j - QBLK - p) <= WINDOW
    masks = [band, band & (j >= QBLK), band & (j < 2 * QBLK)]
    return jnp.asarray(np.where(np.stack(masks), 0.0, NEG), F32)


def _rope_tables():
    t = np.arange(DEC_SEQ)
    freqs = (np.float32(ROPE_BASE) ** (-np.arange(16, dtype=np.float32) / np.float32(16))).astype(np.float32)
    ang_r = (t // GRID_W).astype(np.float32)[:, None] * freqs
    ang_c = (t % GRID_W).astype(np.float32)[:, None] * freqs
    cos_h = np.concatenate([np.cos(ang_r), np.cos(ang_r), np.cos(ang_c), np.cos(ang_c)], axis=1)
    sin_h = np.concatenate([-np.sin(ang_r), np.sin(ang_r), -np.sin(ang_c), np.sin(ang_c)], axis=1)
    return (jnp.asarray(np.tile(cos_h, (1, 2)), F32), jnp.asarray(np.tile(sin_h, (1, 2)), F32))


def kernel(x_prompt, x_sample, cache_c_k, cache_c_v, cache_d_k, cache_d_v, c, c_ctx, norm_g, w_mod, b_mod,
           w_in, w_out, w_s, b_s, w_conv, sink, rpb, final_g):
    w_in_b = w_in.astype(BF16)
    w_out_b = w_out.astype(BF16)
    ws_cat = jnp.transpose(w_s, (0, 2, 1, 3)).reshape(DEPTH, CHUNK, A_HEADS * CHUNK).astype(BF16)
    bias_a = jnp.repeat(jnp.transpose(b_s, (0, 2, 1)), HEAD_DIM, axis=2)
    norm_g3 = norm_g.reshape(DEPTH, 1, D_MODEL)
    fg = final_g.reshape(1, D_MODEL)
    cos_t, sin_t = _rope_tables()
    feature_major = lambda cache, w: jnp.transpose(cache, (0, 1, 3, 4, 2)).reshape(DEC_BATCH, DEPTH, w, PAST_LEN)
    cck = feature_major(cache_c_k, 128)
    ccv = feature_major(cache_c_v, 128)
    cdk = feature_major(cache_d_k, 256)
    cdv = feature_major(cache_d_v, 256)

    cond = jnp.concatenate([c_ctx[None, :], c, jnp.zeros((N_GROUPS - 1 - DEC_BATCH, D_MODEL), F32)], axis=0)
    mod, tab = _modulation_and_tables(cond, w_mod, b_mod, rpb)

    y_prompt, s_ck, s_cv, s_dk, s_dv = _ctx_layers(x_prompt, mod, norm_g3, fg, w_in_b, w_out_b, ws_cat,
                                                   bias_a, w_conv, sink)
    xs = x_sample
    for l in range(DEPTH):
        yab, sz, qc, kc, vc, qd, kd, vd = _lat_proj(l, xs, mod, norm_g3, w_in_b, cos_t, sin_t,
                                                    ws_cat, bias_a, w_conv)
        xs = _lat_mix(l, xs, mod, fg, yab, sz, qc, kc, vc, qd, kd, vd, cck, ccv, cdk, cdv, tab, w_out_b, sink)

    shape_c = (BATCH, DEPTH, SEQ, 2, HEAD_DIM)
    shape_d = (BATCH, DEPTH, SEQ, 4, HEAD_DIM)
    return (y_prompt, xs, s_ck.reshape(shape_c), s_cv.reshape(shape_c), s_dk.reshape(shape_d),
            s_dv.reshape(shape_d))
```

```python
import functools

import numpy as np
import jax
import jax.numpy as jnp
from jax import lax
from jax.experimental import pallas as pl
from jax.experimental.pallas import tpu as pltpu

D_MODEL = 1024
BATCH = 16
SEQ = 256
DEPTH = 4
DEC_BATCH = 2
DEC_SEQ = 4096
PAST_LEN = 512
GRID_W = 64
GROUP_W = 256
HEAD_DIM = 64
A_HEADS = 4
CHUNK = 128
WINDOW = 128
QBLK = 128
WIN_R = 8
WIN_C = 16
ROPE_BASE = 10000.0
EPS = 1e-6
NEG = -1e30
D_IN = 3584
LOG2E = 1.4426950408889634
Q_SCALE = HEAD_DIM ** -0.5 * LOG2E

AU, AV, AZ, BB, BC, BH, BZ = 0, 256, 512, 768, 1024, 1280, 1536
CQ, CKV, CZ, DQ, DK, DV, DZ = 1792, 2048, 2304, 2560, 2816, 3072, 3328

LANES = 128
TILE = 512
ROWS_PER_TILE = TILE // GRID_W
N_TILES = DEC_SEQ // TILE
CTX_SEQS = 4
PROJ_TILES = 2
N_GROUPS = 8
VMEM_LIMIT = 56 * 1024 * 1024

F32 = jnp.float32
BF16 = jnp.bfloat16


def _silu(z):
    return z * (1.0 / (1.0 + jnp.exp(-z)))


def _dot(a, b):
    return jnp.dot(a, b, preferred_element_type=F32)


def _dot_nt(a, b):
    return lax.dot_general(a, b, (((1,), (1,)), ((), ())), preferred_element_type=F32)


def _rms(x):
    return x * lax.rsqrt(jnp.mean(x * x, axis=-1, keepdims=True) + EPS)


def _norm_mod(x, g, shift, scale):
    return _rms(x) * (g * (1.0 + scale)) + shift


def _swap64(x):
    return pltpu.roll(x, HEAD_DIM, axis=1)


def _group_mean_matrix():
    r = lax.broadcasted_iota(jnp.int32, (GROUP_W, GROUP_W), 0) // HEAD_DIM
    c = lax.broadcasted_iota(jnp.int32, (GROUP_W, GROUP_W), 1) // HEAD_DIM
    return jnp.where(r == c, 1.0 / HEAD_DIM, 0.0).astype(BF16)


def _branch_a(au, av, az, ws, bias):
    t = av.shape[0]
    sq = av * av
    hi = sq.astype(BF16)
    lo = (sq - hi.astype(F32)).astype(BF16)
    gm = _group_mean_matrix()
    ms = _dot(hi, gm) + _dot(lo, gm)
    vh = (av * lax.rsqrt(ms + EPS)).astype(BF16)
    head = lax.broadcasted_iota(jnp.int32, (CHUNK, GROUP_W), 1) // HEAD_DIM
    outs = []
    for n in range(t // CHUNK):
        v = vh[n * CHUNK:(n + 1) * CHUNK]
        rhs = jnp.concatenate([jnp.where(head == h, v, jnp.zeros_like(v)) for h in range(A_HEADS)], axis=0)
        outs.append(_dot(ws, rhs) + bias)
    mixed = jnp.concatenate(outs, axis=0)
    return au * mixed * _silu(az)


def _branch_b(bb, bc, bh, bz, wc, prev_row, next_row):
    t = bb.shape[0]
    xc = bc * bh
    row = lax.broadcasted_iota(jnp.int32, xc.shape, 0)
    xm = jnp.where(row == 0, prev_row, pltpu.roll(xc, 1, axis=0))
    xp = jnp.where(row == t - 1, next_row, pltpu.roll(xc, t - 1, axis=0))
    y = wc[0:1, :] * xm + wc[1:2, :] * xc + wc[2:3, :] * xp
    return bb * y * _silu(bz)


def _attend(score, value, pos, sink=None):
    m = jnp.max(score, axis=-1, keepdims=True)
    if sink is not None:
        m = jnp.maximum(m, sink * LOG2E)
    acc = _dot(jnp.exp2(score - m).astype(BF16), value)
    o = acc[:, pos * LANES:(pos + 1) * LANES]
    l = acc[:, (1 - pos) * LANES:(2 - pos) * LANES]
    if sink is not None:
        l = l + jnp.exp2(sink * LOG2E - m)
    return o * (1.0 / l)


def _context_pass(q_stacked, k_t, value):
    s = _dot(q_stacked, k_t)
    m = jnp.max(s, axis=-1, keepdims=True)
    return m, _dot(jnp.exp2(s - m).astype(BF16), value)


def _local_softmax(s_loc, m_ctx, sink=None):
    m = jnp.maximum(jnp.max(s_loc, axis=-1, keepdims=True), m_ctx)
    if sink is not None:
        m = jnp.maximum(m, sink)
    e = jnp.exp2(s_loc - m).astype(BF16)
    return e, jnp.exp2(m_ctx - m), (None if sink is None else jnp.exp2(sink - m))


def _local_output(e, values, acc_ctx, ctx_scale, sink_term=None):
    rows = e.shape[0] // len(values)
    outs = []
    for p, (v, pos) in enumerate(values):
        r = slice(p * rows, (p + 1) * rows)
        acc = _dot(e[r], v) + acc_ctx[r] * ctx_scale[r]
        o = acc[:, pos * LANES:(pos + 1) * LANES]
        l = acc[:, (1 - pos) * LANES:(2 - pos) * LANES]
        if sink_term is not None:
            l = l + sink_term[r]
        outs.append(o * (1.0 / l))
    return outs


def _skewed(n_steps, stages):
    for tick in range(n_steps + len(stages) - 1):
        for lag, stage in enumerate(stages):
            if 0 <= tick - lag < n_steps:
                stage(tick - lag)


def _low_half(m):
    return lax.broadcasted_iota(jnp.int32, (m, LANES), 1) < HEAD_DIM


def _mask_half(q, low, half):
    z = jnp.zeros_like(q)
    return jnp.where(low, q, z) if half == 0 else jnp.where(low, z, q)


def _out_proj(x, gate, ya, yb, yc, yd, w_out_ref):
    y = _dot(ya.astype(BF16), w_out_ref[0:256, :])
    y += _dot(yb.astype(BF16), w_out_ref[256:512, :])
    y += _dot(yc.astype(BF16), w_out_ref[512:768, :])
    y += _dot(yd.astype(BF16), w_out_ref[768:1024, :])
    return x + gate * y


N_BIAS_TILES = 2 * WIN_R - 2


def _mod_kernel(cond_ref, w_ref, b_ref, rpb_ref, o_ref, tab_ref):
    mod = _dot(_silu(cond_ref[...]).astype(BF16), w_ref[...].astype(BF16)) + b_ref[...]
    for j in range(3):
        o_ref[j] = mod[:, j * D_MODEL:(j + 1) * D_MODEL]
    _nbr_table(rpb_ref, tab_ref)


def _modulation_and_tables(cond, w_mod, b_mod, rpb):
    half = LANES // 2 - (2 * WIN_C - 1)
    rows = jnp.pad(rpb, ((0, 0), (0, 0), (0, 1), (0, half)))
    rpb_p = jnp.concatenate([rows[:, :, :-1], rows[:, :, 1:]], axis=-1)
    rpb_p = jnp.pad(rpb_p, ((0, 0), (0, 0), (0, 1), (0, 0)))
    return pl.pallas_call(
        _mod_kernel,
        grid=(DEPTH,),
        in_specs=[
            pl.BlockSpec((N_GROUPS, D_MODEL), lambda l: (0, 0)),
            pl.BlockSpec((None, D_MODEL, 3 * D_MODEL), lambda l: (l, 0, 0)),
            pl.BlockSpec((None, 1, 3 * D_MODEL), lambda l: (l, 0, 0)),
            pl.BlockSpec((None, 4, 2 * WIN_R, LANES), lambda l: (l, 0, 0, 0)),
        ],
        out_specs=[pl.BlockSpec((None, 3, N_GROUPS, D_MODEL), lambda l: (l, 0, 0, 0)),
                   pl.BlockSpec((None, 4, N_BIAS_TILES, GRID_W, LANES), lambda l: (l, 0, 0, 0, 0))],
        out_shape=[jax.ShapeDtypeStruct((DEPTH, 3, N_GROUPS, D_MODEL), F32),
                   jax.ShapeDtypeStruct((DEPTH, 4, N_BIAS_TILES, GRID_W, LANES), F32)],
        compiler_params=pltpu.CompilerParams(dimension_semantics=("arbitrary",),
                                             vmem_limit_bytes=VMEM_LIMIT),
        name="modulation",
    )(cond, w_mod, b_mod.reshape(DEPTH, 1, 3 * D_MODEL), rpb_p)


def _nbr_table(rpb_ref, o_ref):
    c = lax.broadcasted_iota(jnp.int32, (GRID_W, LANES), 0)
    lane = lax.broadcasted_iota(jnp.int32, (GRID_W, LANES), 1)
    cc = lane % GRID_W
    cstart = jnp.clip(c - WIN_C // 2, 0, GRID_W - WIN_C)
    for h in range(4):
        for ro in range(N_BIAS_TILES):
            v = pltpu.roll(jnp.broadcast_to(rpb_ref[h, ro:ro + 1, :] * LOG2E, (GRID_W, LANES)),
                           LANES - (WIN_C - 1), 1, stride=1, stride_axis=0)
            o_ref[h, ro] = jnp.where(cc >= cstart, jnp.where(cc < cstart + WIN_C, v, NEG), NEG)


def _ctx_kernel(sink_ref, x_ref, mod_ref, g_ref, fg_ref, w_in_ref, w_out_ref, ws_ref, ba_ref, wc_ref,
                y_ref, ck_ref, cv_ref, dk_ref, dv_ref, xs):
    l = pl.program_id(0)
    pair = pl.program_id(1)

    @pl.when(l == 0)
    def _():
        for u in range(CTX_SEQS):
            xs[CTX_SEQS * pair + u] = x_ref[u]

    low = _low_half(SEQ)
    ones = jnp.ones((SEQ, LANES), BF16)

    def project(u):
        x = xs[CTX_SEQS * pair + u]
        h = _norm_mod(x, g_ref[...], mod_ref[0, 0:1, :], mod_ref[1, 0:1, :]).astype(BF16)
        return x, _dot(h, w_in_ref[...])

    def mix(u, p):
        kcf = p[:, CKV:CKV + 128]
        vcf = p[:, CKV + 128:CKV + 256]
        kdf = p[:, DK:DK + 256]
        vdf = p[:, DV:DV + 256]
        ck_ref[u] = kcf
        cv_ref[u] = vcf
        dk_ref[u] = kdf
        dv_ref[u] = vdf

        ya = _branch_a(p[:, AU:AU + 256], p[:, AV:AV + 256], p[:, AZ:AZ + 256], ws_ref[...], ba_ref[...])
        zero_row = jnp.zeros((1, GROUP_W), F32)
        yb = _branch_b(p[:, BB:BB + 256], p[:, BC:BC + 256], p[:, BH:BH + 256], p[:, BZ:BZ + 256],
                       wc_ref[...], zero_row, zero_row)

        kc = (kcf.astype(BF16), _swap64(kcf).astype(BF16))
        vc = (jnp.concatenate([vcf.astype(BF16), ones], axis=1),
              jnp.concatenate([ones, _swap64(vcf).astype(BF16)], axis=1))
        o = []
        for t in range(2):
            q = (p[:, CQ + t * LANES:CQ + (t + 1) * LANES] * Q_SCALE).astype(BF16)
            for half in range(2):
                sw = (t + half) % 2
                o.append(_attend(_dot_nt(_mask_half(q, low, half), kc[sw]), vc[sw], sw,
                                 sink_ref[l, 2 * t + half]))
        yc = jnp.concatenate([jnp.where(low, o[0], o[1]), jnp.where(low, o[2], o[3])], axis=1)
        yc = yc * _silu(p[:, CZ:CZ + 256])

        o = []
        for t in range(2):
            q = (p[:, DQ + t * LANES:DQ + (t + 1) * LANES] * Q_SCALE).astype(BF16)
            k = kdf[:, t * LANES:(t + 1) * LANES].astype(BF16)
            v = jnp.concatenate([vdf[:, t * LANES:(t + 1) * LANES].astype(BF16), ones], axis=1)
            for half in range(2):
                o.append(_attend(_dot_nt(_mask_half(q, low, half), k), v, 0))
        yd = jnp.concatenate([jnp.where(low, o[0], o[1]), jnp.where(low, o[2], o[3])], axis=1)
        yd = yd * _silu(p[:, DZ:DZ + 256])
        return ya, yb, yc, yd

    def finish(u, x, ys):
        x_new = _out_proj(x, mod_ref[2, 0:1, :], *ys, w_out_ref)
        xs[CTX_SEQS * pair + u] = x_new
        return x_new

    staged = [project(0)]
    x_new = []
    for u in range(CTX_SEQS):
        if u + 1 < CTX_SEQS:
            staged.append(project(u + 1))
        x, p = staged[u]
        x_new.append(finish(u, x, mix(u, p)))

    @pl.when(l == DEPTH - 1)
    def _():
        for u in range(CTX_SEQS):
            y_ref[u] = _rms(x_new[u]) * fg_ref[...]


def _ctx_layers(x, mod, norm_g, final_g, w_in, w_out, ws_cat, bias_a, w_conv, sink):
    per_layer = lambda *shape: pl.BlockSpec((None,) + shape, lambda l, b: (l,) + (0,) * len(shape))
    state = lambda w: pl.BlockSpec((CTX_SEQS, None, SEQ, w), lambda l, b: (b, l, 0, 0))
    n_steps = BATCH // CTX_SEQS
    return pl.pallas_call(
        _ctx_kernel,
        grid=(DEPTH, n_steps),
        in_specs=[
            pl.BlockSpec(memory_space=pltpu.SMEM),
            pl.BlockSpec((CTX_SEQS, SEQ, D_MODEL), lambda l, b: (jnp.where(l == 0, b, n_steps - 1), 0, 0)),
            per_layer(3, N_GROUPS, D_MODEL),
            per_layer(1, D_MODEL),
            pl.BlockSpec((1, D_MODEL), lambda l, b: (0, 0)),
            pl.BlockSpec((None, D_MODEL, D_IN), lambda l, b: (l, 0, 0), pipeline_mode=pl.Buffered(1)),
            pl.BlockSpec((None, D_MODEL, D_MODEL), lambda l, b: (l, 0, 0), pipeline_mode=pl.Buffered(1)),
            per_layer(CHUNK, A_HEADS * CHUNK),
            per_layer(CHUNK, GROUP_W),
            per_layer(3, GROUP_W),
        ],
        out_specs=[
            pl.BlockSpec((CTX_SEQS, SEQ, D_MODEL), lambda l, b: (jnp.where(l == DEPTH - 1, b, 0), 0, 0)),
            state(128), state(128), state(256), state(256),
        ],
        out_shape=[
            jax.ShapeDtypeStruct((BATCH, SEQ, D_MODEL), F32),
            jax.ShapeDtypeStruct((BATCH, DEPTH, SEQ, 128), F32),
            jax.ShapeDtypeStruct((BATCH, DEPTH, SEQ, 128), F32),
            jax.ShapeDtypeStruct((BATCH, DEPTH, SEQ, 256), F32),
            jax.ShapeDtypeStruct((BATCH, DEPTH, SEQ, 256), F32),
        ],
        scratch_shapes=[pltpu.VMEM((BATCH, SEQ, D_MODEL), F32)],
        compiler_params=pltpu.CompilerParams(dimension_semantics=("arbitrary", "arbitrary"),
                                             vmem_limit_bytes=VMEM_LIMIT),
        name="ctx_layers",
    )(sink, x, mod, norm_g, final_g, w_in, w_out, ws_cat, bias_a, w_conv)


def _rope(x, cos, sin_signed, first_half):
    swapped = jnp.where(first_half, pltpu.roll(x, LANES - 16, axis=1), pltpu.roll(x, 16, axis=1))
    return x * cos + swapped * sin_signed


def _lat_proj_kernel(x_ref, xp_ref, xn_ref, mod_ref, g_ref, w_ref, cos_ref, sin_ref, ws_ref, ba_ref, wc_ref,
                     yab_ref, sz_ref, qc_ref, kc_ref, vc_ref, qd_ref, kd_ref, vd_ref):
    i = pl.program_id(1)
    grp = pl.ds(1 + pl.program_id(0), 1)
    g = g_ref[...]
    shift = mod_ref[0, grp, :]
    scale = mod_ref[1, grp, :]
    first_half = (lax.broadcasted_iota(jnp.int32, (TILE, LANES), 1) % 32) < 16

    def normalised(x):
        return _norm_mod(x, g, shift, scale).astype(BF16)

    def project(u, h, after_first_dot):
        rows = slice(u * TILE, (u + 1) * TILE)
        tile_index = PROJ_TILES * i + u
        pa = _dot(h, w_ref[:, AU:AU + 3 * GROUP_W])
        after_first_dot()
        ya = _branch_a(pa[:, 0:256], pa[:, 256:512], pa[:, 512:768], ws_ref[...], ba_ref[...])
        yab_ref[rows, 0:GROUP_W] = ya.astype(BF16)

        pb = _dot(h, w_ref[:, BB:BB + 4 * GROUP_W])
        before = xp_ref[...] if u == 0 else x_ref[u * TILE - 8:u * TILE, :]
        after = xn_ref[...] if u == PROJ_TILES - 1 else x_ref[(u + 1) * TILE:(u + 1) * TILE + 8, :]
        e = _dot(normalised(jnp.concatenate([before, after], axis=0)), w_ref[:, BC:BC + 2 * GROUP_W])
        prev_row = jnp.where(tile_index > 0, e[7:8, 0:256] * e[7:8, 256:512], 0.0)
        next_row = jnp.where(tile_index < N_TILES - 1, e[8:9, 0:256] * e[8:9, 256:512], 0.0)
        yb = _branch_b(pb[:, 0:256], pb[:, 256:512], pb[:, 512:768], pb[:, 768:1024], wc_ref[...],
                       prev_row, next_row)
        yab_ref[rows, GROUP_W:2 * GROUP_W] = yb.astype(BF16)

        sz_ref[rows, 0:256] = _silu(_dot(h, w_ref[:, CZ:CZ + 256]))
        sz_ref[rows, 256:512] = _silu(_dot(h, w_ref[:, DZ:DZ + 256]))
        cos = cos_ref[rows, :]
        sin = sin_ref[rows, :]
        qc = _dot(h, w_ref[:, CQ:CQ + 256])
        for t in range(2):
            qt = _rope(qc[:, t * LANES:(t + 1) * LANES], cos, sin, first_half) * Q_SCALE
            qc_ref[rows, t * LANES:(t + 1) * LANES] = qt.astype(BF16)
        kv = _dot(h, w_ref[:, CKV:CKV + 256])
        k = _rope(kv[:, 0:LANES], cos, sin, first_half)
        v = kv[:, LANES:2 * LANES]
        kc_ref[rows, 0:LANES] = k.astype(BF16)
        kc_ref[rows, LANES:2 * LANES] = _swap64(k).astype(BF16)
        vc_ref[rows, 0:LANES] = v.astype(BF16)
        vc_ref[rows, LANES:2 * LANES] = _swap64(v).astype(BF16)
        qd_ref[rows, :] = (_dot(h, w_ref[:, DQ:DQ + 256]) * Q_SCALE).astype(BF16)
        kd_ref[rows, :] = _dot(h, w_ref[:, DK:DK + 256]).astype(BF16)
        vd_ref[rows, :] = _dot(h, w_ref[:, DV:DV + 256]).astype(BF16)

    hs = [normalised(x_ref[0:TILE, :])]

    def normalise_next(u):
        if u + 1 < PROJ_TILES:
            hs.append(normalised(x_ref[(u + 1) * TILE:(u + 2) * TILE, :]))

    for u in range(PROJ_TILES):
        project(u, hs[u], functools.partial(normalise_next, u))


def _lat_proj(layer, x, mod, norm_g, w_in, cos_t, sin_t, ws_cat, bias_a, w_conv):
    step_rows = PROJ_TILES * TILE
    tile = lambda w: pl.BlockSpec((None, step_rows, w), lambda b, i: (b, i, 0))
    per_layer = lambda *shape: pl.BlockSpec((None,) + shape, lambda b, i: (layer,) + (0,) * len(shape))
    sds = lambda w, dt: jax.ShapeDtypeStruct((DEC_BATCH, DEC_SEQ, w), dt)
    edge = step_rows // 8
    last = DEC_SEQ // 8 - 1
    return pl.pallas_call(
        _lat_proj_kernel,
        grid=(DEC_BATCH, N_TILES // PROJ_TILES),
        in_specs=[
            tile(D_MODEL),
            pl.BlockSpec((None, 8, D_MODEL), lambda b, i: (b, jnp.maximum(i * edge - 1, 0), 0)),
            pl.BlockSpec((None, 8, D_MODEL), lambda b, i: (b, jnp.minimum((i + 1) * edge, last), 0)),
            per_layer(3, N_GROUPS, D_MODEL),
            per_layer(1, D_MODEL),
            per_layer(D_MODEL, D_IN),
            pl.BlockSpec((step_rows, LANES), lambda b, i: (i, 0)),
            pl.BlockSpec((step_rows, LANES), lambda b, i: (i, 0)),
            per_layer(CHUNK, A_HEADS * CHUNK),
            per_layer(CHUNK, GROUP_W),
            per_layer(3, GROUP_W),
        ],
        out_specs=[tile(512), tile(512)] + [tile(256)] * 6,
        out_shape=[sds(512, BF16), sds(512, F32)] + [sds(256, BF16)] * 6,
        compiler_params=pltpu.CompilerParams(dimension_semantics=("arbitrary", "arbitrary"),
                                             vmem_limit_bytes=VMEM_LIMIT),
        name="lat_proj",
    )(x, x, x, mod, norm_g, w_in, cos_t, sin_t, ws_cat, bias_a, w_conv)


_C_GROUPS = (((0, 0), (1, 1)), ((0, 1), (1, 0)))
_D_GROUPS = (((0, 0), (0, 1)), ((1, 0), (1, 1)))


def _stacked_queries(q_ref, rows, low, groups):
    return [jnp.concatenate([_mask_half(q_ref[rows, t * LANES:(t + 1) * LANES], low, half)
                             for t, half in group], axis=0) for group in groups]


def _lat_mix_kernel(layer, sink_ref, x_ref, mod_ref, fg_ref, yab_ref, sz_ref,
                    qc_ref, kc_ref, kcp_ref, kcn_ref, vc_ref, vcp_ref, vcn_ref,
                    qd_ref, kd_ref, kdp_ref, kdn_ref, vd_ref, vdp_ref, vdn_ref,
                    cck_ref, ccv_ref, cdk_ref, cdv_ref, cmask_ref, tab_ref, w_out_ref,
                    xo_ref,
                    kce, vce, kde, vde, yc_scr, yd_scr, m_ctx, acc_ctx, *stage_bufs):
    b = pl.program_id(0)
    i = pl.program_id(1)
    s_c, s_d = stage_bufs[0:2], stage_bufs[2:4]

    def file_context(m, acc, group, step_rows, n_steps):
        for which in range(2):
            for k in range(n_steps):
                src = slice(which * TILE + k * step_rows, which * TILE + (k + 1) * step_rows)
                dst = slice((4 * k + 2 * group + which) * step_rows, (4 * k + 2 * group + which + 1) * step_rows)
                m_ctx[dst, :] = m[src]
                acc_ctx[dst, :] = acc[src]

    kce[0:QBLK, :] = kcp_ref[...]
    kce[QBLK:QBLK + TILE, :] = kc_ref[...]
    kce[QBLK + TILE:, :] = kcn_ref[...]
    for c_src, c_dst in ((0, 0), (LANES, 2 * LANES)):
        vce[0:QBLK, c_dst:c_dst + LANES] = vcp_ref[:, c_src:c_src + LANES]
        vce[QBLK:QBLK + TILE, c_dst:c_dst + LANES] = vc_ref[:, c_src:c_src + LANES]
        vce[QBLK + TILE:, c_dst:c_dst + LANES] = vcn_ref[:, c_src:c_src + LANES]
    vce[:, LANES:2 * LANES] = jnp.ones((TILE + 2 * QBLK, LANES), BF16)
    swap_rows = lambda x: jnp.concatenate([x[HEAD_DIM:], x[:HEAD_DIM]], axis=0)
    cck = cck_ref[...].astype(BF16)
    ccv = ccv_ref[...].T
    ones_ctx = jnp.ones((PAST_LEN, LANES), BF16)
    k_ctx = (cck, swap_rows(cck))
    v_ctx = jnp.concatenate([ccv.astype(BF16), ones_ctx, _swap64(ccv).astype(BF16)], axis=1)
    low_q = _low_half(QBLK)
    low_t = _low_half(TILE)
    n_blocks = DEC_SEQ // QBLK
    blocks_per_tile = TILE // QBLK

    for sw, q in enumerate(_stacked_queries(qc_ref, slice(None), low_t, _C_GROUPS)):
        m, acc = _context_pass(q, k_ctx[sw], v_ctx[:, sw * LANES:(sw + 2) * LANES])
        file_context(m, acc, sw, QBLK, blocks_per_tile)

    def c_scores(j):
        n = i * blocks_per_tile + j
        window = cmask_ref[jnp.where(n == 0, 1, jnp.where(n == n_blocks - 1, 2, 0))]
        window = jnp.concatenate([window, window], axis=0)
        rows = slice(j * QBLK, (j + 1) * QBLK)
        for sw, q in enumerate(_stacked_queries(qc_ref, rows, low_q, _C_GROUPS)):
            k_loc = kce[j * QBLK:(j + 3) * QBLK, sw * LANES:(sw + 1) * LANES]
            s_c[j % 2][2 * sw * QBLK:2 * (sw + 1) * QBLK, :] = _dot_nt(q, k_loc) + window

    def c_softmax(j):
        step = slice(4 * j * QBLK, 4 * (j + 1) * QBLK)
        sink = jnp.concatenate([jnp.full((QBLK, 1), sink_ref[layer, 2 * t + half] * LOG2E, F32)
                                for group in _C_GROUPS for t, half in group], axis=0)
        e, ctx_scale, sink_term = _local_softmax(s_c[j % 2][...], m_ctx[step, :], sink)
        values = [(vce[j * QBLK:(j + 3) * QBLK, sw * LANES:(sw + 2) * LANES], sw) for sw in range(2)]
        r0, r1 = _local_output(e, values, acc_ctx[step, :], ctx_scale, sink_term)
        yc_scr[j * QBLK:(j + 1) * QBLK, :] = jnp.concatenate(
            [jnp.where(low_q, r0[0:QBLK], r1[0:QBLK]), jnp.where(low_q, r1[QBLK:], r0[QBLK:])], axis=1)

    _skewed(blocks_per_tile, (c_scores, c_softmax))

    halo = 4 * GRID_W
    kde[0:halo, :] = kdp_ref[...]
    kde[halo:halo + TILE, :] = kd_ref[...]
    kde[halo + TILE:, :] = kdn_ref[...]
    for c_src, c_dst in ((0, 0), (LANES, 2 * LANES)):
        vde[0:halo, c_dst:c_dst + LANES] = vdp_ref[:, c_src:c_src + LANES]
        vde[halo:halo + TILE, c_dst:c_dst + LANES] = vd_ref[:, c_src:c_src + LANES]
        vde[halo + TILE:, c_dst:c_dst + LANES] = vdn_ref[:, c_src:c_src + LANES]
    vde[:, LANES:2 * LANES] = jnp.ones((TILE + 2 * halo, LANES), BF16)
    kd_ctx = cdk_ref[...].astype(BF16)
    cdv = cdv_ref[...].T.astype(BF16)
    vd_ctx = jnp.concatenate([cdv[:, 0:LANES], ones_ctx, cdv[:, LANES:2 * LANES]], axis=1)
    low_r = _low_half(GRID_W)
    n_rows = DEC_SEQ // GRID_W

    for t, q in enumerate(_stacked_queries(qd_ref, slice(None), low_t, _D_GROUPS)):
        m, acc = _context_pass(q, kd_ctx[t * LANES:(t + 1) * LANES], vd_ctx[:, t * LANES:(t + 2) * LANES])
        file_context(m, acc, t, GRID_W, ROWS_PER_TILE)

    def window_start(rl):
        r = i * ROWS_PER_TILE + rl
        rs = jnp.clip(r - WIN_R // 2, 0, n_rows - WIN_R)
        return rs - r + (WIN_R - 1), pl.multiple_of((rs - i * ROWS_PER_TILE + 4) * GRID_W, GRID_W)

    def d_scores(rl):
        didx, off = window_start(rl)
        rows = slice(rl * GRID_W, (rl + 1) * GRID_W)
        for t, q in enumerate(_stacked_queries(qd_ref, rows, low_r, _D_GROUPS)):
            k_loc = kde[pl.ds(off, WIN_R * GRID_W), t * LANES:(t + 1) * LANES]
            bias = jnp.concatenate(
                [jnp.concatenate([tab_ref[h, didx + 2 * p] for p in range(WIN_R // 2)], axis=1)
                 for h in (2 * t, 2 * t + 1)], axis=0)
            s_d[rl % 2][2 * t * GRID_W:2 * (t + 1) * GRID_W, :] = _dot_nt(q, k_loc) + bias

    def d_softmax(rl):
        _, off = window_start(rl)
        step = slice(4 * rl * GRID_W, 4 * (rl + 1) * GRID_W)
        e, ctx_scale, _ = _local_softmax(s_d[rl % 2][...], m_ctx[step, :])
        values = [(vde[pl.ds(off, WIN_R * GRID_W), t * LANES:(t + 2) * LANES], t) for t in range(2)]
        r = _local_output(e, values, acc_ctx[step, :], ctx_scale)
        yd_scr[rl * GRID_W:(rl + 1) * GRID_W, :] = jnp.concatenate(
            [jnp.where(low_r, r[t][0:GRID_W], r[t][GRID_W:]) for t in range(2)], axis=1)

    _skewed(ROWS_PER_TILE, (d_scores, d_softmax))

    yc = (yc_scr[...] * sz_ref[:, 0:256]).astype(BF16)
    yd = (yd_scr[...] * sz_ref[:, 256:512]).astype(BF16)
    halves = []
    for cols in (slice(0, D_MODEL // 2), slice(D_MODEL // 2, D_MODEL)):
        y = (_dot(yab_ref[...], w_out_ref[0:2 * GROUP_W, cols]) + _dot(yc, w_out_ref[2 * GROUP_W:3 * GROUP_W, cols])
             + _dot(yd, w_out_ref[3 * GROUP_W:4 * GROUP_W, cols]))
        x_half = x_ref[:, cols] + mod_ref[2, pl.ds(1 + b, 1), cols] * y
        if layer == DEPTH - 1:
            halves.append(x_half)
        else:
            xo_ref[:, cols] = x_half
    if layer == DEPTH - 1:
        xo_ref[...] = _rms(jnp.concatenate(halves, axis=1)) * fg_ref[...]


def _lat_mix(layer, x, mod, final_g, yab, sz, qc, kc, vc, qd, kd, vd, cck, ccv, cdk, cdv, tab, w_out, sink):
    tile = lambda w: pl.BlockSpec((None, TILE, w), lambda b, i: (b, i, 0))
    per_layer = lambda *shape: pl.BlockSpec((None,) + shape, lambda b, i: (layer,) + (0,) * len(shape))

    def halo(rows, w):
        per = TILE // rows
        last = DEC_SEQ // rows - 1
        prev = pl.BlockSpec((None, rows, w), lambda b, i: (b, jnp.maximum(i * per - 1, 0), 0))
        nxt = pl.BlockSpec((None, rows, w), lambda b, i: (b, jnp.minimum((i + 1) * per, last), 0))
        return prev, nxt

    cprev, cnext = halo(QBLK, 256)
    dprev, dnext = halo(4 * GRID_W, 256)
    cache = lambda w: pl.BlockSpec((None, None, w, PAST_LEN), lambda b, i: (b, layer, 0, 0))
    return pl.pallas_call(
        functools.partial(_lat_mix_kernel, layer),
        grid=(DEC_BATCH, N_TILES),
        in_specs=[
            pl.BlockSpec(memory_space=pltpu.SMEM),
            tile(D_MODEL),
            per_layer(3, N_GROUPS, D_MODEL),
            pl.BlockSpec((1, D_MODEL), lambda b, i: (0, 0)),
            tile(512), tile(512),
            tile(256), tile(256), cprev, cnext, tile(256), cprev, cnext,
            tile(256), tile(256), dprev, dnext, tile(256), dprev, dnext,
            cache(128), cache(128), cache(256), cache(256),
            pl.BlockSpec((3, QBLK, 3 * QBLK), lambda b, i: (0, 0, 0)),
            per_layer(4, N_BIAS_TILES, GRID_W, LANES),
            per_layer(D_MODEL, D_MODEL),
        ],
        out_specs=tile(D_MODEL),
        out_shape=jax.ShapeDtypeStruct((DEC_BATCH, DEC_SEQ, D_MODEL), F32),
        scratch_shapes=[
            pltpu.VMEM((TILE + 2 * QBLK, 2 * LANES), BF16),
            pltpu.VMEM((TILE + 2 * QBLK, 3 * LANES), BF16),
            pltpu.VMEM((TILE + 8 * GRID_W, 2 * LANES), BF16),
            pltpu.VMEM((TILE + 8 * GRID_W, 3 * LANES), BF16),
            pltpu.VMEM((TILE, 256), F32),
            pltpu.VMEM((TILE, 256), F32),
            pltpu.VMEM((4 * TILE, 1), F32),
            pltpu.VMEM((4 * TILE, 2 * LANES), F32),
        ] + 2 * [pltpu.VMEM((4 * QBLK, 3 * QBLK), F32)] + 2 * [pltpu.VMEM((4 * GRID_W, WIN_R * GRID_W), F32)],
        compiler_params=pltpu.CompilerParams(dimension_semantics=("arbitrary", "arbitrary"),
                                             vmem_limit_bytes=VMEM_LIMIT),
        name="lat_mix",
    )(sink, x, mod, final_g, yab, sz, qc, kc, kc, kc, vc, vc, vc,
      qd, kd, kd, kd, vd, vd, vd, cck, ccv, cdk, cdv, _window_masks(), tab, w_out)


def _window_masks():
    p = np.arange(QBLK)[:, None]
    j = np.arange(3 * QBLK)[None, :]
    band = np.abs(j - QBLK - p) <= WINDOW
    masks = [band, band & (j >= QBLK), band & (j < 2 * QBLK)]
    return jnp.asarray(np.where(np.stack(masks), 0.0, NEG), F32)


def _rope_tables():
    t = np.arange(DEC_SEQ)
    freqs = (np.float32(ROPE_BASE) ** (-np.arange(16, dtype=np.float32) / np.float32(16))).astype(np.float32)
    ang_r = (t // GRID_W).astype(np.float32)[:, None] * freqs
    ang_c = (t % GRID_W).astype(np.float32)[:, None] * freqs
    cos_h = np.concatenate([np.cos(ang_r), np.cos(ang_r), np.cos(ang_c), np.cos(ang_c)], axis=1)
    sin_h = np.concatenate([-np.sin(ang_r), np.sin(ang_r), -np.sin(ang_c), np.sin(ang_c)], axis=1)
    return (jnp.asarray(np.tile(cos_h, (1, 2)), F32), jnp.asarray(np.tile(sin_h, (1, 2)), F32))


def kernel(x_prompt, x_sample, cache_c_k, cache_c_v, cache_d_k, cache_d_v, c, c_ctx, norm_g, w_mod, b_mod,
           w_in, w_out, w_s, b_s, w_conv, sink, rpb, final_g):
    w_in_b = w_in.astype(BF16)
    w_out_b = w_out.astype(BF16)
    ws_cat = jnp.transpose(w_s, (0, 2, 1, 3)).reshape(DEPTH, CHUNK, A_HEADS * CHUNK).astype(BF16)
    bias_a = jnp.repeat(jnp.transpose(b_s, (0, 2, 1)), HEAD_DIM, axis=2)
    norm_g3 = norm_g.reshape(DEPTH, 1, D_MODEL)
    fg = final_g.reshape(1, D_MODEL)
    cos_t, sin_t = _rope_tables()
    feature_major = lambda cache, w: jnp.transpose(cache, (0, 1, 3, 4, 2)).reshape(DEC_BATCH, DEPTH, w, PAST_LEN)
    cck = feature_major(cache_c_k, 128)
    ccv = feature_major(cache_c_v, 128)
    cdk = feature_major(cache_d_k, 256)
    cdv = feature_major(cache_d_v, 256)

    cond = jnp.concatenate([c_ctx[None, :], c, jnp.zeros((N_GROUPS - 1 - DEC_BATCH, D_MODEL), F32)], axis=0)
    mod, tab = _modulation_and_tables(cond, w_mod, b_mod, rpb)

    y_prompt, s_ck, s_cv, s_dk, s_dv = _ctx_layers(x_prompt, mod, norm_g3, fg, w_in_b, w_out_b, ws_cat,
                                                   bias_a, w_conv, sink)
    xs = x_sample
    for l in range(DEPTH):
        yab, sz, qc, kc, vc, qd, kd, vd = _lat_proj(l, xs, mod, norm_g3, w_in_b, cos_t, sin_t,
                                                    ws_cat, bias_a, w_conv)
        xs = _lat_mix(l, xs, mod, fg, yab, sz, qc, kc, vc, qd, kd, vd, cck, ccv, cdk, cdv, tab, w_out_b, sink)

    shape_c = (BATCH, DEPTH, SEQ, 2, HEAD_DIM)
    shape_d = (BATCH, DEPTH, SEQ, 4, HEAD_DIM)
    return (y_prompt, xs, s_ck.reshape(shape_c), s_cv.reshape(shape_c), s_dk.reshape(shape_d),
            s_dv.reshape(shape_d))
```
